```python
import math
import jax, jax.numpy as jnp
from jax import lax
import numpy as np

D_MODEL = 1024
BATCH = 2
SEQ = 8192
DEPTH = 1
DEC_BATCH = 128
DEC_SEQ = 4
PAST_LEN = 2048
PAGE_SIZE = 128

R_HEADS = 8
R_HD = 64
R_WIDTH = R_HEADS * R_HD
LORA_W = 64
LORA_A = 64
SHIFT_COLS = 3 * R_WIDTH + LORA_W + LORA_A
GN_EPS = 64e-5
A_GROUPS = ((128, 1), (512, 4), (2048, 16))
N_GROUPS = 3
A_HEADS = 4
A_HD = 128
A_QKV_W = N_GROUPS * A_HEADS * A_HD
A_WIDTH = A_HEADS * A_HD
N_IN = SHIFT_COLS + R_WIDTH + 3 * A_QKV_W + A_WIDTH + 2 * D_MODEL
ALPHA = (2 * DEPTH) ** 0.25
BETA = (8 * DEPTH) ** -0.25
LN_EPS = 1e-5
NEG = -1e30

kernel_name = "rwkv7_dilated_attn_hybrid_step"

F32 = jnp.float32


def _split(t, sizes):
    return jnp.split(t, [int(s) for s in np.cumsum(sizes)[:-1]], axis=-1)


def layer_norm(x, g, b):
    xf = x.astype(F32)
    mu = jnp.mean(xf, -1, keepdims=True)
    var = jnp.mean(jnp.square(xf - mu), -1, keepdims=True)
    return ((xf - mu) * lax.rsqrt(var + LN_EPS) * g.astype(F32) + b.astype(F32)).astype(x.dtype)


def rwkv_mix(zs, z_gate, wkv0, w0, w_w2, a0, w_a2, k_k, k_a, r_k, lnx_g, lnx_b):
    B, T, _ = zs.shape
    r, k, v, wd, ad = _split(zs.astype(F32), [R_WIDTH, R_WIDTH, R_WIDTH, LORA_W, LORA_A])
    w = -jax.nn.softplus(-(w0.astype(F32) + jnp.tanh(wd) @ w_w2.astype(F32))) - 0.5
    decay = jnp.exp(-jnp.exp(w))
    a = jax.nn.sigmoid(a0.astype(F32) + ad @ w_a2.astype(F32))
    kk = (k * k_k.astype(F32)).reshape(B, T, R_HEADS, R_HD)
    kk = kk / jnp.maximum(jnp.linalg.norm(kk, axis=-1, keepdims=True), 1e-12)
    k = k * (1.0 + (a - 1.0) * k_a.astype(F32))
    hs = lambda t: t.reshape(B, T, R_HEADS, R_HD)
    r, decay, k, v, a = hs(r), hs(decay), hs(k), hs(v), hs(a)
    xs = tuple(jnp.moveaxis(t, 1, 0) for t in (r, decay, k, v, kk, a))

    def step(S, inp):
        r_t, w_t, k_t, v_t, kk_t, a_t = inp
        sa = jnp.einsum('bhvk,bhk->bhv', S, -kk_t)
        S = (S * w_t[:, :, None, :] + sa[..., None] * (kk_t * a_t)[:, :, None, :]
             + v_t[..., None] * k_t[:, :, None, :])
        return S, jnp.einsum('bhvk,bhk->bhv', S, r_t)

    S_fin, o = lax.scan(step, wkv0.astype(F32), xs)
    o = jnp.moveaxis(o, 0, 1)
    mu = jnp.mean(o, -1, keepdims=True)
    var = jnp.mean(jnp.square(o - mu), -1, keepdims=True)
    o = ((o - mu) * lax.rsqrt(var + GN_EPS)).reshape(B, T, R_WIDTH) * lnx_g.astype(F32) + lnx_b.astype(F32)
    bonus = (jnp.sum(r * k * r_k.astype(F32), -1, keepdims=True) * v).reshape(B, T, R_WIDTH)
    y = (o + bonus) * jax.nn.silu(z_gate.astype(F32))
    return y.astype(zs.dtype), S_fin.astype(wkv0.dtype)


def dilated_attn_prompt(q, k, v, d, span):
    B, S, H, E = q.shape
    L = -(-S // d)
    nb = -(-L // span)
    Lp = nb * span
    pad = Lp * d - S

    def to_blocks(t):
        t = jnp.pad(t, ((0, 0), (0, pad), (0, 0), (0, 0)))
        t = jnp.swapaxes(t.reshape(B, Lp, d, H, E), 1, 2)
        return t.reshape(B, d, nb, span, H, E)

    def with_prev(t):
        prev = jnp.pad(t[:, :, :-1], ((0, 0), (0, 0), (1, 0), (0, 0), (0, 0), (0, 0)))
        return jnp.concatenate([prev, t], axis=3)

    qb = to_blocks(q)
    kc, vc = with_prev(to_blocks(k)), with_prev(to_blocks(v))
    s = jnp.einsum('bdnqhe,bdnkhe->bdnhqk', qb, kc).astype(F32) * (1.0 / math.sqrt(E))
    i = jnp.arange(span)[:, None]
    j = jnp.arange(2 * span)[None, :]
    dist = span + i - j
    valid = (dist >= 0) & (dist <= span)
    first = (jnp.arange(nb)[:, None, None] == 0) & (j[None] < span)
    valid = valid[None] & ~first
    s = jnp.where(valid[None, None, :, None], s, NEG)
    m = jnp.max(s, -1, keepdims=True)
    p = jnp.exp(s - m)
    l = jnp.sum(p, -1)
    o = jnp.einsum('bdnhqk,bdnkhe->bdnqhe', p, vc.astype(F32))
    o = o / jnp.swapaxes(l, 3, 4)[..., None]
    lse = jnp.swapaxes(m[..., 0] + jnp.log(l), 3, 4)

    def from_blocks(t):
        rest = t.shape[4:]
        t = jnp.swapaxes(t.reshape((B, d, Lp) + rest), 1, 2)
        return t.reshape((B, Lp * d) + rest)[:, :S]

    return from_blocks(o), from_blocks(lse)


def dilated_attn_cached(q, k, v, buf, d, span):
    B, T, H, E = q.shape
    Wb = buf.shape[1]
    kall = jnp.concatenate([buf[:, :, 0].astype(k.dtype), k], axis=1)
    vall = jnp.concatenate([buf[:, :, 1].astype(v.dtype), v], axis=1)
    idx = Wb + jnp.arange(T)[:, None] - jnp.arange(span + 1)[None, :] * d
    valid = idx >= 0
    idxc = jnp.maximum(idx, 0)
    kg = jnp.take(kall, idxc, axis=1)
    vg = jnp.take(vall, idxc, axis=1)
    s = jnp.einsum('bthe,btjhe->bthj', q, kg).astype(F32) * (1.0 / math.sqrt(E))
    s = jnp.where(valid[None, :, None, :], s, NEG)
    m = jnp.max(s, -1, keepdims=True)
    p = jnp.exp(s - m)
    l = jnp.sum(p, -1)
    o = jnp.einsum('bthj,btjhe->bthe', p, vg.astype(F32)) / l[..., None]
    return o, m[..., 0] + jnp.log(l)


def layer(x, shift_prev, wkv0, kv_bufs, w_in, b_gate, mu_shift, w0, w_w2, a0, w_a2, k_k, k_a, r_k,
          lnx_g, lnx_b, w_oa, w_ob, w_out, ln_g, ln_b):
    B, T, _ = x.shape
    h = x @ w_in
    zs, z_r, q, k, v, z_a, g_r, g_a = _split(
        h, [SHIFT_COLS, R_WIDTH, A_QKV_W, A_QKV_W, A_QKV_W, A_WIDTH, D_MODEL, D_MODEL])
    prev = jnp.concatenate([shift_prev[:, None].astype(zs.dtype), zs[:, :-1]], axis=1)
    zs_mixed = zs + mu_shift * (prev - zs)
    y_r, wkv_new = rwkv_mix(zs_mixed, z_r, wkv0, w0, w_w2, a0, w_a2, k_k, k_a, r_k, lnx_g, lnx_b)
    q = q.reshape(B, T, N_GROUPS, A_HEADS, A_HD)
    k = k.reshape(B, T, N_GROUPS, A_HEADS, A_HD)
    v = v.reshape(B, T, N_GROUPS, A_HEADS, A_HD)
    outs, lses, new_kv = [], [], []
    for g, (win, dil) in enumerate(A_GROUPS):
        span = win // dil
        qg, kg, vg = q[:, :, g], k[:, :, g], v[:, :, g]
        if kv_bufs is None:
            o, lse = dilated_attn_prompt(qg, kg, vg, dil, span)
            keep = min(win, T)
            new_kv.append(jnp.stack([kg[:, T - keep:], vg[:, T - keep:]], axis=2))
        else:
            o, lse = dilated_attn_cached(qg, kg, vg, kv_bufs[g], dil, span)
            new_kv.append(jnp.stack([kg, vg], axis=2))
        outs.append(o)
        lses.append(lse)
    wts = jax.nn.softmax(jnp.stack(lses, 0), axis=0)
    o = jnp.einsum('gbth,gbthe->bthe', wts, jnp.stack(outs, 0))
    y_a = (o.reshape(B, T, A_WIDTH) * jax.nn.silu(z_a.astype(F32))).astype(x.dtype)
    gate_r = jax.nn.sigmoid(g_r + b_gate[:D_MODEL])
    gate_a = jax.nn.sigmoid(g_a + b_gate[D_MODEL:])
    mix = gate_r * (y_r @ w_oa) + gate_a * (y_a @ w_ob)
    y = layer_norm(ALPHA * x + mix @ w_out, ln_g, ln_b)
    return y, new_kv, wkv_new, zs[:, -1]


def setup_inputs(seed: int = 0) -> dict:
    key = jax.random.key(seed)
    ks = jax.random.split(key, 32)
    n = lambda i, shape: jax.random.normal(ks[i], shape, F32)
    L = DEPTH
    col_scale = np.ones((N_IN,), np.float32)
    col_scale[2 * R_WIDTH:3 * R_WIDTH] = BETA
    v0 = SHIFT_COLS + R_WIDTH + 2 * A_QKV_W
    col_scale[v0:v0 + A_QKV_W] = BETA
    wb = [min(w, PAST_LEN) for (w, _) in A_GROUPS]
    return {
        "x_prompt": n(0, (BATCH, SEQ, D_MODEL)),
        "x_sample": n(1, (DEC_BATCH, DEC_SEQ, D_MODEL)),
        "cache_kv_g1": n(2, (L, DEC_BATCH, wb[0], 2, A_HEADS, A_HD)),
        "cache_kv_g2": n(3, (L, DEC_BATCH, wb[1], 2, A_HEADS, A_HD)),
        "cache_kv_g3": n(4, (L, DEC_BATCH, wb[2], 2, A_HEADS, A_HD)),
        "state_rwkv_wkv": 0.5 * n(5, (L, DEC_BATCH, R_HEADS, R_HD, R_HD)),
        "state_rwkv_shift": n(6, (L, DEC_BATCH, SHIFT_COLS)),
        "w_in": n(7, (L, D_MODEL, N_IN)) * (D_MODEL ** -0.5) * jnp.asarray(col_scale),
        "b_gate": 0.1 * n(8, (L, 2 * D_MODEL)),
        "mu_shift": jax.random.uniform(ks[9], (L, SHIFT_COLS), F32),
        "w0": jax.random.uniform(ks[10], (L, R_WIDTH), F32, -3.0, 1.0),
        "w_w2": n(11, (L, LORA_W, R_WIDTH)) * (LORA_W ** -0.5),
        "a0": 0.5 * n(12, (L, R_WIDTH)),
        "w_a2": n(13, (L, LORA_A, R_WIDTH)) * (LORA_A ** -0.5),
        "k_k": 0.85 + 0.05 * n(14, (L, R_WIDTH)),
        "k_a": 1.0 + 0.05 * n(15, (L, R_WIDTH)),
        "r_k": 0.1 * n(16, (L, R_HEADS, R_HD)),
        "lnx_g": 1.0 + 0.05 * n(17, (L, R_WIDTH)),
        "lnx_b": 0.02 * n(18, (L, R_WIDTH)),
        "w_oa": n(19, (L, R_WIDTH, D_MODEL)) * (R_WIDTH ** -0.5),
        "w_ob": n(20, (L, A_WIDTH, D_MODEL)) * (A_WIDTH ** -0.5),
        "w_out": n(21, (L, D_MODEL, D_MODEL)) * (D_MODEL ** -0.5) * BETA,
        "ln_g": 1.0 + 0.05 * n(22, (L, D_MODEL)),
        "ln_b": 0.02 * n(23, (L, D_MODEL)),
    }


def reference(x_prompt, x_sample, cache_kv_g1, cache_kv_g2, cache_kv_g3, state_rwkv_wkv, state_rwkv_shift,
              w_in, b_gate, mu_shift, w0, w_w2, a0, w_a2, k_k, k_a, r_k, lnx_g, lnx_b,
              w_oa, w_ob, w_out, ln_g, ln_b):
    xp, xs = x_prompt, x_sample
    Bp = xp.shape[0]
    kv1p, kv1s, kv2p, kv2s, kv3p, kv3s = [], [], [], [], [], []
    wkvp, wkvs, shp, shs = [], [], [], []
    for l in range(DEPTH):
        wts = (w_in[l], b_gate[l], mu_shift[l], w0[l], w_w2[l], a0[l], w_a2[l], k_k[l], k_a[l], r_k[l],
               lnx_g[l], lnx_b[l], w_oa[l], w_ob[l], w_out[l], ln_g[l], ln_b[l])
        shift0 = jnp.zeros((Bp, SHIFT_COLS), xp.dtype)
        wkv0 = jnp.zeros((Bp, R_HEADS, R_HD, R_HD), state_rwkv_wkv.dtype)
        xp, kv_p, s_p, sh_p = layer(xp, shift0, wkv0, None, *wts)
        xs, kv_s, s_s, sh_s = layer(xs, state_rwkv_shift[l], state_rwkv_wkv[l],
                                    (cache_kv_g1[l], cache_kv_g2[l], cache_kv_g3[l]), *wts)
        kv1p.append(kv_p[0]); kv2p.append(kv_p[1]); kv3p.append(kv_p[2])
        kv1s.append(kv_s[0]); kv2s.append(kv_s[1]); kv3s.append(kv_s[2])
        wkvp.append(s_p); wkvs.append(s_s); shp.append(sh_p); shs.append(sh_s)
    return (xp, xs, jnp.stack(kv1p), jnp.stack(kv1s), jnp.stack(kv2p), jnp.stack(kv2s),
            jnp.stack(kv3p), jnp.stack(kv3s), jnp.stack(wkvp), jnp.stack(wkvs), jnp.stack(shp), jnp.stack(shs))
```

```python
import functools
import math

import jax
import jax.numpy as jnp
from jax import lax
from jax.experimental import pallas as pl
from jax.experimental.pallas import tpu as pltpu

F32 = jnp.float32
BF16 = jnp.bfloat16
HIGHEST = lax.Precision.HIGHEST

D_MODEL = 1024
R_HEADS = 8
R_HD = 64
R_WIDTH = R_HEADS * R_HD
LORA = 64
SHIFT_COLS = 3 * R_WIDTH + 2 * LORA
GN_EPS = 64e-5
A_GROUPS = ((128, 1), (512, 4), (2048, 16))
N_GROUPS = 3
A_HEADS = 4
A_HD = 128
A_WIDTH = A_HEADS * A_HD
A_QKV_W = N_GROUPS * A_WIDTH
SPAN = 128
DEPTH = 1
ALPHA = (2 * DEPTH) ** 0.25
LN_EPS = 1e-5
NEG = -1e30

LANES = 128
CHUNK = 64
VMEM_LIMIT = 48 * 1024 * 1024


def _params(*sem):
    return pltpu.CompilerParams(dimension_semantics=sem, vmem_limit_bytes=VMEM_LIMIT)


def _mm_kernel(x_ref, w_ref, o_ref):
    o_ref[...] = jnp.dot(x_ref[...].astype(BF16), w_ref[...],
                         preferred_element_type=F32).astype(o_ref.dtype)


def _matmul(x, w, tm, tn):
    m, k = x.shape
    n = w.shape[1]
    return pl.pallas_call(
        _mm_kernel,
        grid=(m // tm, n // tn),
        in_specs=[pl.BlockSpec((tm, k), lambda i, j: (i, 0)),
                  pl.BlockSpec((k, tn), lambda i, j: (0, j))],
        out_specs=pl.BlockSpec((tm, tn), lambda i, j: (i, j)),
        out_shape=jax.ShapeDtypeStruct((m, n), F32),
        compiler_params=_params("parallel", "parallel"),
        name="in_proj",
    )(x, w)


def _prep_math(zs, prev, mu, w0, a0, wl, k_k, k_a, r_k, seg, outs):
    r_o, kp_o, v_o, kk_o, b_o, lw_o, bon_o = outs
    zsm = zs + mu * (prev - zs)
    r = zsm[:, 0:R_WIDTH]
    k = zsm[:, R_WIDTH:2 * R_WIDTH]
    v = zsm[:, 2 * R_WIDTH:3 * R_WIDTH]
    slab = zsm[:, 3 * R_WIDTH:SHIFT_COLS]
    lane = lax.broadcasted_iota(jnp.int32, slab.shape, 1)
    slab = jnp.where(lane < LORA, jnp.tanh(slab), slab)
    lin = jnp.dot(slab, wl, precision=HIGHEST, preferred_element_type=F32)
    wlin = w0 + lin[:, 0:R_WIDTH]
    alin = a0 + lin[:, R_WIDTH:2 * R_WIDTH]
    nw = -wlin
    softplus = jnp.maximum(nw, 0.0) + jnp.log1p(jnp.exp(-jnp.abs(nw)))
    w = -softplus - 0.5
    lw = -jnp.exp(w)
    a = jax.nn.sigmoid(alin)
    kkr = k * k_k
    ss = jnp.dot(kkr * kkr, seg, precision=HIGHEST, preferred_element_type=F32)
    kk = kkr / jnp.maximum(jnp.sqrt(ss), 1e-12)
    kp = k * (1.0 + (a - 1.0) * k_a)
    rks = jnp.dot(r * kp * r_k, seg, precision=HIGHEST, preferred_element_type=F32)
    r_o[...] = r
    kp_o[...] = kp
    v_o[...] = v
    kk_o[...] = kk
    b_o[...] = kk * a
    lw_o[...] = lw
    bon_o[...] = rks * v


def _prep_prompt_kernel(zs_ref, p8_ref, mu_ref, w0_ref, a0_ref, wl_ref, kk_ref, ka_ref, rk_ref,
                        seg_ref, *outs, tiles_per_seq):
    i = pl.program_id(0)
    zs = zs_ref[...]
    rolled = pltpu.roll(zs, 1, 0)
    carry = jnp.where(i % tiles_per_seq == 0, 0.0, p8_ref[7:8, :])
    row = lax.broadcasted_iota(jnp.int32, zs.shape, 0)
    prev = jnp.where(row == 0, carry, rolled)
    _prep_math(zs, prev, mu_ref[...], w0_ref[...], a0_ref[...], wl_ref[...], kk_ref[...],
               ka_ref[...], rk_ref[...], seg_ref[...], outs)


def _prep_sample_kernel(zs_ref, first_ref, mu_ref, w0_ref, a0_ref, wl_ref, kk_ref, ka_ref, rk_ref,
                        seg_ref, *outs, seq_len):
    zs = zs_ref[...]
    rolled = pltpu.roll(zs, 1, 0)
    row = lax.broadcasted_iota(jnp.int32, zs.shape, 0)
    prev = jnp.where(row % seq_len == 0, first_ref[...], rolled)
    _prep_math(zs, prev, mu_ref[...], w0_ref[...], a0_ref[...], wl_ref[...], kk_ref[...],
               ka_ref[...], rk_ref[...], seg_ref[...], outs)


def _rwkv_prep(zs, first, seq_len, weights, tm):
    m = zs.shape[0]
    const = lambda shape: pl.BlockSpec(shape, lambda i: (0, 0))
    w_specs = [const((1, SHIFT_COLS)), const((1, R_WIDTH)), const((1, R_WIDTH)),
               const((2 * LORA, 2 * R_WIDTH)), const((1, R_WIDTH)), const((1, R_WIDTH)),
               const((1, R_WIDTH)), const((R_WIDTH, R_WIDTH))]
    row_spec = pl.BlockSpec((tm, SHIFT_COLS), lambda i: (i, 0))
    if first is None:
        kern = functools.partial(_prep_prompt_kernel, tiles_per_seq=seq_len // tm)
        second = zs
        second_spec = pl.BlockSpec((8, SHIFT_COLS), lambda i: (jnp.maximum(i * (tm // 8) - 1, 0), 0))
    else:
        kern = functools.partial(_prep_sample_kernel, seq_len=seq_len)
        second = first
        second_spec = row_spec
    out_spec = pl.BlockSpec((tm, R_WIDTH), lambda i: (i, 0))
    return pl.pallas_call(
        kern,
        grid=(m // tm,),
        in_specs=[row_spec, second_spec] + w_specs,
        out_specs=[out_spec] * 7,
        out_shape=[jax.ShapeDtypeStruct((m, R_WIDTH), F32)] * 7,
        compiler_params=_params("parallel"),
        name="rwkv_prep",
    )(zs, second, *weights)


def _rwkv_chunk_kernel(r_ref, kp_ref, v_ref, kk_ref, b_ref, lw_ref, s0_ref, o_ref, st_ref, *, nb):
    n = pl.program_id(1)

    @pl.when(n == 0)
    def _():
        st_ref[...] = s0_ref[...]

    c = CHUNK
    c2 = 2 * c
    lane = lax.broadcasted_iota(jnp.int32, (1, LANES), 1)
    m_lo = (lane < R_HD).astype(F32)
    m_hi = 1.0 - m_lo
    row_c = lax.broadcasted_iota(jnp.int32, (c, c), 0)
    col_c = lax.broadcasted_iota(jnp.int32, (c, c), 1)
    tri = (row_c >= col_c).astype(F32)
    row2 = lax.broadcasted_iota(jnp.int32, (c2, c2), 0)
    col2 = lax.broadcasted_iota(jnp.int32, (c2, c2), 1)
    strict = row2 > col2
    incl = row2 >= col2
    eye = (row2 == col2).astype(F32)

    def stack(x):
        return jnp.concatenate([x * m_lo, x * m_hi], axis=0)

    def mm(a, b):
        return jnp.dot(a.astype(BF16), b.astype(BF16), preferred_element_type=F32)

    for bi in range(nb):
        for p in range(R_HEADS // 2):
            sl = slice(p * LANES, (p + 1) * LANES)
            r = r_ref[bi, :, sl]
            kp = kp_ref[bi, :, sl]
            v = v_ref[bi, :, sl]
            kk = kk_ref[bi, :, sl]
            b = b_ref[bi, :, sl]
            lw = lw_ref[bi, :, sl]
            st = st_ref[bi, p]

            g = jnp.dot(tri, lw, precision=HIGHEST, preferred_element_type=F32)
            g_last = g[c - 1:c, :]
            e_g = jnp.exp(g)
            e_ng = jnp.exp(-g)
            e_gm = jnp.exp(g - lw)
            e_cg = jnp.exp(g_last - g)
            ad_s = stack(-(kk * e_gm))
            rd_s = stack(r * e_g)
            bd_s = stack(b * e_ng)
            kd_s = stack(kp * e_ng)
            be_s = stack(b * e_cg)
            ke_s = stack(kp * e_cg)
            v_s = stack(v)

            lhs = jnp.concatenate([ad_s, rd_s], axis=0).astype(BF16)
            rhs = jnp.concatenate([bd_s, kd_s], axis=0).astype(BF16)
            z = lax.dot_general(lhs, rhs, (((1,), (1,)), ((), ())), preferred_element_type=F32)
            n_b = jnp.where(strict, z[0:c2, 0:c2], 0.0)
            n_k = jnp.where(strict, z[0:c2, c2:2 * c2], 0.0)
            m_b = jnp.where(incl, z[c2:2 * c2, 0:c2], 0.0)
            m_k = jnp.where(incl, z[c2:2 * c2, c2:2 * c2], 0.0)

            t_inv = eye + n_b
            q = n_b
            for _ in range(int(math.log2(c)) - 1):
                q = mm(q, q)
                t_inv = t_inv + mm(t_inv, q)

            u = mm(t_inv, mm(ad_s, st) + mm(n_k, v_s))
            o_bd = mm(rd_s, st) + mm(m_b, u) + mm(m_k, v_s)
            o_ref[bi, :, sl] = o_bd[0:c] + o_bd[c:c2]

            pc = jnp.transpose(jnp.broadcast_to(jnp.exp(g_last), (LANES, LANES)))
            lhs_t = jnp.concatenate([jnp.transpose(be_s), jnp.transpose(ke_s)], axis=1)
            rhs_t = jnp.concatenate([u, v_s], axis=0)
            st_ref[bi, p] = pc * st + mm(lhs_t, rhs_t)


def _rwkv_recurrence(r, kp, v, kk, b, lw, s0, nb):
    bsz, t, _ = r.shape
    seq_spec = pl.BlockSpec((nb, CHUNK, R_WIDTH), lambda i, n: (i, n, 0))
    st_spec = pl.BlockSpec((nb, R_HEADS // 2, LANES, LANES), lambda i, n: (i, 0, 0, 0))
    return pl.pallas_call(
        functools.partial(_rwkv_chunk_kernel, nb=nb),
        grid=(bsz // nb, t // CHUNK),
        in_specs=[seq_spec] * 6 + [st_spec],
        out_specs=[seq_spec, st_spec],
        out_shape=[jax.ShapeDtypeStruct((bsz, t, R_WIDTH), F32),
                   jax.ShapeDtypeStruct(s0.shape, F32)],
        compiler_params=_params("parallel", "arbitrary"),
        name="rwkv_chunks",
    )(r, kp, v, kk, b, lw, s0)


def _state_to_pairs(s):
    bsz = s.shape[0]
    st = jnp.swapaxes(s, -1, -2).reshape(bsz, R_HEADS // 2, 2, R_HD, R_HD)
    out = jnp.zeros((bsz, R_HEADS // 2, 2, R_HD, 2, R_HD), F32)
    out = out.at[:, :, 0, :, 0, :].set(st[:, :, 0])
    out = out.at[:, :, 1, :, 1, :].set(st[:, :, 1])
    return out.reshape(bsz, R_HEADS // 2, LANES, LANES)


def _pairs_to_state(st):
    bsz = st.shape[0]
    s6 = st.reshape(bsz, R_HEADS // 2, 2, R_HD, 2, R_HD)
    s = jnp.stack([s6[:, :, 0, :, 0, :], s6[:, :, 1, :, 1, :]], axis=2)
    return jnp.swapaxes(s.reshape(bsz, R_HEADS, R_HD, R_HD), -1, -2)


def _attn_prompt_kernel(q_ref, kp_ref, kc_ref, vp_ref, vc_ref, o_ref, l_ref):
    n = pl.program_id(2)
    i = lax.broadcasted_iota(jnp.int32, (SPAN, 2 * SPAN), 0)
    j = lax.broadcasted_iota(jnp.int32, (SPAN, 2 * SPAN), 1)
    dist = SPAN + i - j
    valid = (dist >= 0) & (dist <= SPAN) & ((n > 0) | (j >= SPAN))
    scale = 1.0 / math.sqrt(A_HD)
    for h in range(A_HEADS):
        sl = slice(h * A_HD, (h + 1) * A_HD)
        q = q_ref[0, :, sl].astype(BF16)
        k = jnp.concatenate([kp_ref[0, :, sl], kc_ref[0, :, sl]], axis=0).astype(BF16)
        v = jnp.concatenate([vp_ref[0, :, sl], vc_ref[0, :, sl]], axis=0).astype(BF16)
        s = lax.dot_general(q, k, (((1,), (1,)), ((), ())), preferred_element_type=F32) * scale
        s = jnp.where(valid, s, NEG)
        m = jnp.max(s, axis=-1, keepdims=True)
        p = jnp.exp(s - m)
        l = jnp.sum(p, axis=-1, keepdims=True)
        o = jnp.dot(p.astype(BF16), v, preferred_element_type=F32)
        o_ref[0, :, sl] = o / l
        l_ref[0, :, sl] = jnp.broadcast_to(m + jnp.log(l), (SPAN, A_HD))


def _attn_prompt(qkv, bsz, t, g, dil):
    length = t // dil
    nblk = length // SPAN
    per_row = 3 * A_QKV_W // A_WIDTH
    view = qkv.reshape(bsz, length, dil * 3 * A_QKV_W)
    spec = lambda off, prev: pl.BlockSpec(
        (1, SPAN, A_WIDTH),
        (lambda b, rho, n: (b, jnp.maximum(n - 1, 0), rho * per_row + off)) if prev
        else (lambda b, rho, n: (b, n, rho * per_row + off)))
    out_spec = pl.BlockSpec((1, SPAN, A_WIDTH), lambda b, rho, n: (b, n, rho))
    o, l = pl.pallas_call(
        _attn_prompt_kernel,
        grid=(bsz, dil, nblk),
        in_specs=[spec(g, False), spec(N_GROUPS + g, True), spec(N_GROUPS + g, False),
                  spec(2 * N_GROUPS + g, True), spec(2 * N_GROUPS + g, False)],
        out_specs=[out_spec, out_spec],
        out_shape=[jax.ShapeDtypeStruct((bsz, length, dil * A_WIDTH), F32)] * 2,
        compiler_params=_params("parallel", "parallel", "arbitrary"),
        name=f"attn_prompt_g{g}",
    )(view, view, view, view, view)
    return o.reshape(bsz * t, A_WIDTH), l.reshape(bsz * t, A_WIDTH)


def _attn_sample_kernel(qkv_ref, c1_ref, c2_ref, c3_ref, o_ref, l_ref, *, t_new):
    scale = 1.0 / math.sqrt(A_HD)
    row_c = lax.broadcasted_iota(jnp.int32, (SPAN, 1), 0)
    row_n = lax.broadcasted_iota(jnp.int32, (t_new, 1), 0)
    kv_w = 2 * A_WIDTH
    for g, cref in enumerate((c1_ref, c2_ref, c3_ref)):
        k_new = qkv_ref[0, :, A_QKV_W + g * A_WIDTH:A_QKV_W + (g + 1) * A_WIDTH]
        v_new = qkv_ref[0, :, 2 * A_QKV_W + g * A_WIDTH:2 * A_QKV_W + (g + 1) * A_WIDTH]
        for t in range(t_new):
            q = qkv_ref[0, t:t + 1, g * A_WIDTH:(g + 1) * A_WIDTH]
            base = 0 if g == 0 else t * kv_w
            k_c = cref[0, :, base:base + A_WIDTH]
            v_c = cref[0, :, base + A_WIDTH:base + kv_w]
            prod_c = k_c * q
            prod_n = k_new * q
            mask_c = (row_c >= t) if g == 0 else (row_c >= 0)
            mask_n = (row_n <= t) if g == 0 else (row_n == t)
            for h in range(A_HEADS):
                sl = slice(h * A_HD, (h + 1) * A_HD)
                s_c = jnp.sum(prod_c[:, sl], axis=-1, keepdims=True) * scale
                s_n = jnp.sum(prod_n[:, sl], axis=-1, keepdims=True) * scale
                s_c = jnp.where(mask_c, s_c, NEG)
                s_n = jnp.where(mask_n, s_n, NEG)
                m = jnp.maximum(jnp.max(s_c, axis=0, keepdims=True),
                                jnp.max(s_n, axis=0, keepdims=True))
                p_c = jnp.exp(s_c - m)
                p_n = jnp.exp(s_n - m)
                l = jnp.sum(p_c, axis=0, keepdims=True) + jnp.sum(p_n, axis=0, keepdims=True)
                o = (jnp.sum(p_c * v_c[:, sl], axis=0, keepdims=True)
                     + jnp.sum(p_n * v_new[:, sl], axis=0, keepdims=True)) / l
                col = slice(g * A_WIDTH + h * A_HD, g * A_WIDTH + (h + 1) * A_HD)
                o_ref[0, t:t + 1, col] = o
                l_ref[0, t:t + 1, col] = jnp.broadcast_to(m + jnp.log(l), (1, A_HD))


def _attn_sample(qkv, caches, bsz, t_new):
    kv_w = 2 * A_WIDTH
    views = []
    specs = []
    for (win, dil), cache in zip(A_GROUPS, caches):
        assert cache.shape[1] == win and win // dil == SPAN and t_new <= dil or dil == 1
        views.append(cache.reshape(bsz, win // dil, dil * kv_w))
        lanes = min(dil, t_new) * kv_w
        specs.append(pl.BlockSpec((1, SPAN, lanes), lambda b: (b, 0, 0)))
    q_spec = pl.BlockSpec((1, t_new, 3 * A_QKV_W), lambda b: (b, 0, 0))
    out_spec = pl.BlockSpec((1, t_new, A_QKV_W), lambda b: (b, 0, 0))
    o, l = pl.pallas_call(
        functools.partial(_attn_sample_kernel, t_new=t_new),
        grid=(bsz,),
        in_specs=[q_spec] + specs,
        out_specs=[out_spec, out_spec],
        out_shape=[jax.ShapeDtypeStruct((bsz, t_new, A_QKV_W), F32)] * 2,
        compiler_params=_params("parallel"),
        name="attn_sample",
    )(qkv.reshape(bsz, t_new, 3 * A_QKV_W), *views)
    return o.reshape(bsz * t_new, A_QKV_W), l.reshape(bsz * t_new, A_QKV_W)


def _final_kernel(x_ref, o_ref, bon_ref, zr_ref, o1_ref, o2_ref, o3_ref, l1_ref, l2_ref, l3_ref,
                  zgg_ref, seg_ref, lnxg_ref, lnxb_ref, bg_ref, woa_ref, wob_ref, wout_ref,
                  lng_ref, lnb_ref, y_ref):
    seg = seg_ref[...] * (1.0 / R_HD)
    o = o_ref[...]
    mu = jnp.dot(o, seg, precision=HIGHEST, preferred_element_type=F32)
    d = o - mu
    var = jnp.dot(d * d, seg, precision=HIGHEST, preferred_element_type=F32)
    on = d * lax.rsqrt(var + GN_EPS) * lnxg_ref[...] + lnxb_ref[...]
    zr = zr_ref[...]
    y_r = (on + bon_ref[...]) * (zr * jax.nn.sigmoid(zr))

    l1, l2, l3 = l1_ref[...], l2_ref[...], l3_ref[...]
    mx = jnp.maximum(jnp.maximum(l1, l2), l3)
    e1, e2, e3 = jnp.exp(l1 - mx), jnp.exp(l2 - mx), jnp.exp(l3 - mx)
    den = e1 + e2 + e3
    o_a = (e1 / den) * o1_ref[...] + (e2 / den) * o2_ref[...] + (e3 / den) * o3_ref[...]
    z_a = zgg_ref[:, 0:A_WIDTH]
    y_a = o_a * (z_a * jax.nn.sigmoid(z_a))

    g_r = zgg_ref[:, A_WIDTH:A_WIDTH + D_MODEL]
    g_a = zgg_ref[:, A_WIDTH + D_MODEL:A_WIDTH + 2 * D_MODEL]
    gate_r = jax.nn.sigmoid(g_r + bg_ref[:, 0:D_MODEL])
    gate_a = jax.nn.sigmoid(g_a + bg_ref[:, D_MODEL:2 * D_MODEL])
    mix = (gate_r * jnp.dot(y_r.astype(BF16), woa_ref[...], preferred_element_type=F32)
           + gate_a * jnp.dot(y_a.astype(BF16), wob_ref[...], preferred_element_type=F32))
    yy = ALPHA * x_ref[...] + jnp.dot(mix.astype(BF16), wout_ref[...], preferred_element_type=F32)
    mean = jnp.mean(yy, axis=-1, keepdims=True)
    cen = yy - mean
    variance = jnp.mean(cen * cen, axis=-1, keepdims=True)
    y_ref[...] = cen * lax.rsqrt(variance + LN_EPS) * lng_ref[...] + lnb_ref[...]


def _final(x, o, bonus, zr, attn_o, attn_l, zgg, weights, tm):
    m = x.shape[0]
    row = lambda w, j=0: pl.BlockSpec((tm, w), lambda i: (i, j))
    const = lambda shape: pl.BlockSpec(shape, lambda i: (0, 0))
    seg, lnxg, lnxb, bg, woa, wob, wout, lng, lnb = weights
    in_specs = ([row(D_MODEL), row(R_WIDTH), row(R_WIDTH), row(R_WIDTH)]
                + [row(A_WIDTH, j) for _, j in attn_o] + [row(A_WIDTH, j) for _, j in attn_l]
                + [row(A_WIDTH + 2 * D_MODEL), const(seg.shape), const(lnxg.shape), const(lnxb.shape),
                   const(bg.shape), const(woa.shape), const(wob.shape), const(wout.shape),
                   const(lng.shape), const(lnb.shape)])
    return pl.pallas_call(
        _final_kernel,
        grid=(m // tm,),
        in_specs=in_specs,
        out_specs=row(D_MODEL),
        out_shape=jax.ShapeDtypeStruct((m, D_MODEL), F32),
        compiler_params=_params("parallel"),
        name="merge_out_ln",
    )(x, o, bonus, zr, *[a for a, _ in attn_o], *[a for a, _ in attn_l], zgg, *weights)


def _layer(x2, bsz, t, shift_prev, wkv0, caches, wts):
    m = x2.shape[0]
    tm = 512 if m % 512 == 0 else m
    zs = _matmul(x2, wts["w_zs"], tm, SHIFT_COLS)
    zr = _matmul(x2, wts["w_zr"], tm, R_WIDTH)
    qkv = _matmul(x2, wts["w_qkv"], tm, 512)
    zgg = _matmul(x2, wts["w_zgg"], tm, 512)

    if caches is None:
        first = None
        seq_len = t
    else:
        first = jnp.zeros((bsz, t, SHIFT_COLS), F32).at[:, 0].set(shift_prev).reshape(m, SHIFT_COLS)
        seq_len = t
    prep = _rwkv_prep(zs, first, seq_len, wts["prep"], tm)
    r, kp, v, kk, b, lw, bonus = prep

    t_pad = -(-t // CHUNK) * CHUNK
    def seq(a):
        a = a.reshape(bsz, t, R_WIDTH)
        return a if t_pad == t else jnp.pad(a, ((0, 0), (0, t_pad - t), (0, 0)))
    s0 = (jnp.zeros((bsz, R_HEADS // 2, LANES, LANES), F32) if wkv0 is None
          else _state_to_pairs(wkv0))
    o_seq, st = _rwkv_recurrence(seq(r), seq(kp), seq(v), seq(kk), seq(b), seq(lw), s0, 2)
    o_rwkv = o_seq[:, :t].reshape(m, R_WIDTH)
    wkv_new = _pairs_to_state(st)

    if caches is None:
        outs = [_attn_prompt(qkv, bsz, t, g, dil) for g, (_, dil) in enumerate(A_GROUPS)]
        attn_o = [(o, 0) for o, _ in outs]
        attn_l = [(l, 0) for _, l in outs]
    else:
        o_s, l_s = _attn_sample(qkv, caches, bsz, t)
        attn_o = [(o_s, g) for g in range(N_GROUPS)]
        attn_l = [(l_s, g) for g in range(N_GROUPS)]

    tmf = 256 if m % 256 == 0 else m
    y = _final(x2, o_rwkv, bonus, zr, attn_o, attn_l, zgg, wts["final"], tmf)

    qkv3 = qkv.reshape(bsz, t, 3, N_GROUPS, A_HEADS, A_HD)
    new_kv = []
    for g, (win, _) in enumerate(A_GROUPS):
        keep = min(win, t) if caches is None else t
        new_kv.append(jnp.stack([qkv3[:, t - keep:, 1, g], qkv3[:, t - keep:, 2, g]], axis=2))
    shift_new = zs.reshape(bsz, t, SHIFT_COLS)[:, -1]
    return y.reshape(bsz, t, D_MODEL), new_kv, wkv_new, shift_new


def kernel(x_prompt, x_sample, cache_kv_g1, cache_kv_g2, cache_kv_g3, state_rwkv_wkv, state_rwkv_shift,
           w_in, b_gate, mu_shift, w0, w_w2, a0, w_a2, k_k, k_a, r_k, lnx_g, lnx_b,
           w_oa, w_ob, w_out, ln_g, ln_b):
    assert w_in.shape[0] == DEPTH
    bp, tp, _ = x_prompt.shape
    bs, ts, _ = x_sample.shape
    xp = x_prompt.reshape(bp * tp, D_MODEL)
    xs = x_sample.reshape(bs * ts, D_MODEL)
    head = lax.broadcasted_iota(jnp.int32, (R_WIDTH, R_WIDTH), 0) // R_HD
    seg = (head == head.T).astype(F32)
    row = lambda a: a.reshape(1, -1)
    acc = [[] for _ in range(12)]
    for l in range(DEPTH):
        wb = w_in[l].astype(BF16)
        c0 = SHIFT_COLS
        c1 = c0 + R_WIDTH
        c2 = c1 + 3 * A_QKV_W
        zero = jnp.zeros((LORA, R_WIDTH), F32)
        w_lora = jnp.concatenate([jnp.concatenate([w_w2[l], zero], axis=1),
                                  jnp.concatenate([zero, w_a2[l]], axis=1)], axis=0)
        wts = {
            "w_zs": wb[:, :c0], "w_zr": wb[:, c0:c1], "w_qkv": wb[:, c1:c2], "w_zgg": wb[:, c2:],
            "prep": (row(mu_shift[l]), row(w0[l]), row(a0[l]), w_lora, row(k_k[l]), row(k_a[l]),
                     row(r_k[l]), seg),
            "final": (seg, row(lnx_g[l]), row(lnx_b[l]), row(b_gate[l]), w_oa[l].astype(BF16),
                      w_ob[l].astype(BF16), w_out[l].astype(BF16), row(ln_g[l]), row(ln_b[l])),
        }
        yp, kv_p, s_p, sh_p = _layer(xp, bp, tp, None, None, None, wts)
        ys, kv_s, s_s, sh_s = _layer(xs, bs, ts, state_rwkv_shift[l], state_rwkv_wkv[l],
                                     (cache_kv_g1[l], cache_kv_g2[l], cache_kv_g3[l]), wts)
        xp = yp.reshape(bp * tp, D_MODEL)
        xs = ys.reshape(bs * ts, D_MODEL)
        for idx, val in zip(range(2, 12), (kv_p[0], kv_s[0], kv_p[1], kv_s[1], kv_p[2], kv_s[2],
                                           s_p, s_s, sh_p, sh_s)):
            acc[idx].append(val)
    outs = [xp.reshape(bp, tp, D_MODEL), xs.reshape(bs, ts, D_MODEL)]
    outs += [jnp.stack(a) for a in acc[2:]]
    return tuple(outs)
```

```python
import functools
import math

import jax
import jax.numpy as jnp
from jax import lax
from jax.experimental import pallas as pl
from jax.experimental.pallas import tpu as pltpu

F32 = jnp.float32
BF16 = jnp.bfloat16
HIGHEST = lax.Precision.HIGHEST

D_MODEL = 1024
R_HEADS = 8
R_HD = 64
R_WIDTH = R_HEADS * R_HD
LORA = 64
SHIFT_COLS = 3 * R_WIDTH + 2 * LORA
GN_EPS = 64e-5
A_GROUPS = ((128, 1), (512, 4), (2048, 16))
N_GROUPS = 3
A_HEADS = 4
A_HD = 128
A_WIDTH = A_HEADS * A_HD
A_QKV_W = N_GROUPS * A_WIDTH
SPAN = 128
DEPTH = 1
ALPHA = (2 * DEPTH) ** 0.25
LN_EPS = 1e-5
NEG = -1e30

LANES = 128
CHUNK = 64
VMEM_LIMIT = 48 * 1024 * 1024


def _params(*sem):
    return pltpu.CompilerParams(dimension_semantics=sem, vmem_limit_bytes=VMEM_LIMIT)


def _mm_kernel(x_ref, w_ref, o_ref):
    o_ref[...] = jnp.dot(x_ref[...].astype(BF16), w_ref[...],
                         preferred_element_type=F32).astype(o_ref.dtype)


def _matmul(x, w, tm, tn):
    m, k = x.shape
    n = w.shape[1]
    return pl.pallas_call(
        _mm_kernel,
        grid=(m // tm, n // tn),
        in_specs=[pl.BlockSpec((tm, k), lambda i, j: (i, 0)),
                  pl.BlockSpec((k, tn), lambda i, j: (0, j))],
        out_specs=pl.BlockSpec((tm, tn), lambda i, j: (i, j)),
        out_shape=jax.ShapeDtypeStruct((m, n), F32),
        compiler_params=_params("parallel", "parallel"),
        name="in_proj",
    )(x, w)


def _prep_math(zs, prev, mu, w0, a0, wl, k_k, k_a, r_k, seg, outs):
    r_o, kp_o, v_o, kk_o, b_o, lw_o, bon_o = outs
    zsm = zs + mu * (prev - zs)
    r = zsm[:, 0:R_WIDTH]
    k = zsm[:, R_WIDTH:2 * R_WIDTH]
    v = zsm[:, 2 * R_WIDTH:3 * R_WIDTH]
    slab = zsm[:, 3 * R_WIDTH:SHIFT_COLS]
    lane = lax.broadcasted_iota(jnp.int32, slab.shape, 1)
    slab = jnp.where(lane < LORA, jnp.tanh(slab), slab)
    lin = jnp.dot(slab, wl, precision=HIGHEST, preferred_element_type=F32)
    wlin = w0 + lin[:, 0:R_WIDTH]
    alin = a0 + lin[:, R_WIDTH:2 * R_WIDTH]
    nw = -wlin
    softplus = jnp.maximum(nw, 0.0) + jnp.log1p(jnp.exp(-jnp.abs(nw)))
    w = -softplus - 0.5
    lw = -jnp.exp(w)
    a = jax.nn.sigmoid(alin)
    kkr = k * k_k
    ss = jnp.dot(kkr * kkr, seg, precision=HIGHEST, preferred_element_type=F32)
    kk = kkr / jnp.maximum(jnp.sqrt(ss), 1e-12)
    kp = k * (1.0 + (a - 1.0) * k_a)
    rks = jnp.dot(r * kp * r_k, seg, precision=HIGHEST, preferred_element_type=F32)
    r_o[...] = r
    kp_o[...] = kp
    v_o[...] = v
    kk_o[...] = kk
    b_o[...] = kk * a
    lw_o[...] = lw
    bon_o[...] = rks * v


def _prep_prompt_kernel(zs_ref, p8_ref, mu_ref, w0_ref, a0_ref, wl_ref, kk_ref, ka_ref, rk_ref,
                        seg_ref, *outs, tiles_per_seq):
    i = pl.program_id(0)
    zs = zs_ref[...]
    rolled = pltpu.roll(zs, 1, 0)
    carry = jnp.where(i % tiles_per_seq == 0, 0.0, p8_ref[7:8, :])
    row = lax.broadcasted_iota(jnp.int32, zs.shape, 0)
    prev = jnp.where(row == 0, carry, rolled)
    _prep_math(zs, prev, mu_ref[...], w0_ref[...], a0_ref[...], wl_ref[...], kk_ref[...],
               ka_ref[...], rk_ref[...], seg_ref[...], outs)


def _prep_sample_kernel(zs_ref, first_ref, mu_ref, w0_ref, a0_ref, wl_ref, kk_ref, ka_ref, rk_ref,
                        seg_ref, *outs, seq_len):
    zs = zs_ref[...]
    rolled = pltpu.roll(zs, 1, 0)
    row = lax.broadcasted_iota(jnp.int32, zs.shape, 0)
    prev = jnp.where(row % seq_len == 0, first_ref[...], rolled)
    _prep_math(zs, prev, mu_ref[...], w0_ref[...], a0_ref[...], wl_ref[...], kk_ref[...],
               ka_ref[...], rk_ref[...], seg_ref[...], outs)


def _rwkv_prep(zs, first, seq_len, weights, tm):
    m = zs.shape[0]
    const = lambda shape: pl.BlockSpec(shape, lambda i: (0, 0))
    w_specs = [const((1, SHIFT_COLS)), const((1, R_WIDTH)), const((1, R_WIDTH)),
               const((2 * LORA, 2 * R_WIDTH)), const((1, R_WIDTH)), const((1, R_WIDTH)),
               const((1, R_WIDTH)), const((R_WIDTH, R_WIDTH))]
    row_spec = pl.BlockSpec((tm, SHIFT_COLS), lambda i: (i, 0))
    if first is None:
        kern = functools.partial(_prep_prompt_kernel, tiles_per_seq=seq_len // tm)
        second = zs
        second_spec = pl.BlockSpec((8, SHIFT_COLS), lambda i: (jnp.maximum(i * (tm // 8) - 1, 0), 0))
    else:
        kern = functools.partial(_prep_sample_kernel, seq_len=seq_len)
        second = first
        second_spec = row_spec
    out_spec = pl.BlockSpec((tm, R_WIDTH), lambda i: (i, 0))
    return pl.pallas_call(
        kern,
        grid=(m // tm,),
        in_specs=[row_spec, second_spec] + w_specs,
        out_specs=[out_spec] * 7,
        out_shape=[jax.ShapeDtypeStruct((m, R_WIDTH), F32)] * 7,
        compiler_params=_params("parallel"),
        name="rwkv_prep",
    )(zs, second, *weights)


def _rwkv_chunk_kernel(r_ref, kp_ref, v_ref, kk_ref, b_ref, lw_ref, s0_ref, o_ref, st_ref, *, nb):
    n = pl.program_id(1)

    @pl.when(n == 0)
    def _():
        st_ref[...] = s0_ref[...]

    c = CHUNK
    c2 = 2 * c
    lane = lax.broadcasted_iota(jnp.int32, (1, LANES), 1)
    m_lo = (lane < R_HD).astype(F32)
    m_hi = 1.0 - m_lo
    row_c = lax.broadcasted_iota(jnp.int32, (c, c), 0)
    col_c = lax.broadcasted_iota(jnp.int32, (c, c), 1)
    tri = (row_c >= col_c).astype(F32)
    row2 = lax.broadcasted_iota(jnp.int32, (c2, c2), 0)
    col2 = lax.broadcasted_iota(jnp.int32, (c2, c2), 1)
    strict = row2 > col2
    incl = row2 >= col2
    eye = (row2 == col2).astype(F32)

    def stack(x):
        return jnp.concatenate([x * m_lo, x * m_hi], axis=0)

    def mm(a, b):
        return jnp.dot(a.astype(BF16), b.astype(BF16), preferred_element_type=F32)

    chains = [(bi, slice(p * LANES, (p + 1) * LANES), p) for bi in range(nb) for p in range(R_HEADS // 2)]
    load = lambda ref: [ref[bi, :, sl] for bi, sl, _ in chains]
    r, kp, v, kk, b, lw = (load(ref) for ref in (r_ref, kp_ref, v_ref, kk_ref, b_ref, lw_ref))
    st = [st_ref[bi, p] for bi, _, p in chains]
    each = lambda f, *xs: [f(*a) for a in zip(*xs)]

    g = each(lambda x: jnp.dot(tri, x, precision=HIGHEST, preferred_element_type=F32), lw)
    g_last = each(lambda x: x[c - 1:c, :], g)
    e_g = each(jnp.exp, g)
    e_ng = each(lambda x: jnp.exp(-x), g)
    e_gm = each(lambda x, y: jnp.exp(x - y), g, lw)
    e_cg = each(lambda x, y: jnp.exp(y - x), g, g_last)
    ad_s = each(lambda x, y: stack(-(x * y)), kk, e_gm)
    rd_s = each(lambda x, y: stack(x * y), r, e_g)
    bd_s = each(lambda x, y: stack(x * y), b, e_ng)
    kd_s = each(lambda x, y: stack(x * y), kp, e_ng)
    be_s = each(lambda x, y: stack(x * y), b, e_cg)
    ke_s = each(lambda x, y: stack(x * y), kp, e_cg)
    v_s = each(stack, v)

    nt = lambda x, y: lax.dot_general(x.astype(BF16), y.astype(BF16), (((1,), (1,)), ((), ())),
                                      preferred_element_type=F32)
    z = each(lambda a1, a2, b1, b2: nt(jnp.concatenate([a1, a2], axis=0),
                                       jnp.concatenate([b1, b2], axis=0)), ad_s, rd_s, bd_s, kd_s)
    n_b = each(lambda x: jnp.where(strict, x[0:c2, 0:c2], 0.0), z)
    n_k = each(lambda x: jnp.where(strict, x[0:c2, c2:2 * c2], 0.0), z)
    m_b = each(lambda x: jnp.where(incl, x[c2:2 * c2, 0:c2], 0.0), z)
    m_k = each(lambda x: jnp.where(incl, x[c2:2 * c2, c2:2 * c2], 0.0), z)

    t_inv = each(lambda x: eye + x, n_b)
    q = n_b
    for _ in range(int(math.log2(c)) - 1):
        q = each(lambda x: mm(x, x), q)
        t_inv = each(lambda x, y: x + mm(x, y), t_inv, q)

    y = each(lambda a1, s, nk, vs: mm(a1, s) + mm(nk, vs), ad_s, st, n_k, v_s)
    u = each(mm, t_inv, y)
    o_bd = each(lambda rd, s, mb, uu, mk, vs: mm(rd, s) + mm(mb, uu) + mm(mk, vs),
                rd_s, st, m_b, u, m_k, v_s)
    for (bi, sl, _), o in zip(chains, o_bd):
        o_ref[bi, :, sl] = o[0:c] + o[c:c2]

    pc = each(lambda x: jnp.transpose(jnp.broadcast_to(jnp.exp(x), (LANES, LANES))), g_last)
    upd = each(lambda be, ke, uu, vs: mm(jnp.concatenate([jnp.transpose(be), jnp.transpose(ke)], axis=1),
                                         jnp.concatenate([uu, vs], axis=0)), be_s, ke_s, u, v_s)
    for (bi, _, p), dec, s, d in zip(chains, pc, st, upd):
        st_ref[bi, p] = dec * s + d


def _rwkv_recurrence(r, kp, v, kk, b, lw, s0, nb):
    bsz, t, _ = r.shape
    seq_spec = pl.BlockSpec((nb, CHUNK, R_WIDTH), lambda i, n: (i, n, 0))
    st_spec = pl.BlockSpec((nb, R_HEADS // 2, LANES, LANES), lambda i, n: (i, 0, 0, 0))
    return pl.pallas_call(
        functools.partial(_rwkv_chunk_kernel, nb=nb),
        grid=(bsz // nb, t // CHUNK),
        in_specs=[seq_spec] * 6 + [st_spec],
        out_specs=[seq_spec, st_spec],
        out_shape=[jax.ShapeDtypeStruct((bsz, t, R_WIDTH), F32),
                   jax.ShapeDtypeStruct(s0.shape, F32)],
        compiler_params=_params("parallel", "arbitrary"),
        name="rwkv_chunks",
    )(r, kp, v, kk, b, lw, s0)


def _state_to_pairs(s):
    bsz = s.shape[0]
    st = jnp.swapaxes(s, -1, -2).reshape(bsz, R_HEADS // 2, 2, R_HD, R_HD)
    out = jnp.zeros((bsz, R_HEADS // 2, 2, R_HD, 2, R_HD), F32)
    out = out.at[:, :, 0, :, 0, :].set(st[:, :, 0])
    out = out.at[:, :, 1, :, 1, :].set(st[:, :, 1])
    return out.reshape(bsz, R_HEADS // 2, LANES, LANES)


def _pairs_to_state(st):
    bsz = st.shape[0]
    s6 = st.reshape(bsz, R_HEADS // 2, 2, R_HD, 2, R_HD)
    s = jnp.stack([s6[:, :, 0, :, 0, :], s6[:, :, 1, :, 1, :]], axis=2)
    return jnp.swapaxes(s.reshape(bsz, R_HEADS, R_HD, R_HD), -1, -2)


def _attn_prompt_kernel(q_ref, kp_ref, kc_ref, vp_ref, vc_ref, o_ref, l_ref, *, dil, nsub):
    n = pl.program_id(1)
    i = lax.broadcasted_iota(jnp.int32, (SPAN, 2 * SPAN), 0)
    j = lax.broadcasted_iota(jnp.int32, (SPAN, 2 * SPAN), 1)
    dist = SPAN + i - j
    band = (dist >= 0) & (dist <= SPAN)
    scale = 1.0 / math.sqrt(A_HD)

    def rows(run, rho):
        start = run * SPAN * dil + rho
        return pl.ds(start, SPAN, stride=dil) if dil > 1 else pl.ds(start, SPAN)

    for run in range(nsub):
        for rho in range(dil):
            cur = rows(run, rho)
            if run == 0:
                k_prev, v_prev = kp_ref[rows(0, rho), :], vp_ref[rows(0, rho), :]
                valid = band & ((n > 0) | (j >= SPAN))
            else:
                k_prev, v_prev = kc_ref[rows(run - 1, rho), :], vc_ref[rows(run - 1, rho), :]
                valid = band
            q = q_ref[cur, :].astype(BF16)
            k = jnp.concatenate([k_prev, kc_ref[cur, :]], axis=0).astype(BF16)
            v = jnp.concatenate([v_prev, vc_ref[cur, :]], axis=0).astype(BF16)
            s = lax.dot_general(q, k, (((1,), (1,)), ((), ())), preferred_element_type=F32) * scale
            s = jnp.where(valid, s, NEG)
            m = jnp.max(s, axis=-1, keepdims=True)
            p = jnp.exp(s - m)
            l = jnp.sum(p, axis=-1, keepdims=True)
            o = jnp.dot(p.astype(BF16), v, preferred_element_type=F32)
            o_ref[cur, :] = o / l
            l_ref[cur, :] = jnp.broadcast_to(m + jnp.log(l), (SPAN, A_HD))


def _attn_prompt(qkv, bsz, t, g, dil, nsub):
    prev_rows = SPAN * dil
    blk_rows = nsub * prev_rows
    nblk = t // blk_rows
    col = lambda which, h: (which * N_GROUPS + g) * A_HEADS + h
    cur = lambda which: pl.BlockSpec((blk_rows, A_HD), lambda b, n, h: (b * nblk + n, col(which, h)))
    prev = lambda which: pl.BlockSpec(
        (prev_rows, A_HD), lambda b, n, h: (jnp.maximum((b * nblk + n) * nsub - 1, 0), col(which, h)))
    out_spec = pl.BlockSpec((blk_rows, A_HD), lambda b, n, h: (b * nblk + n, h))
    return pl.pallas_call(
        functools.partial(_attn_prompt_kernel, dil=dil, nsub=nsub),
        grid=(bsz, nblk, A_HEADS),
        in_specs=[cur(0), prev(1), cur(1), prev(2), cur(2)],
        out_specs=[out_spec, out_spec],
        out_shape=[jax.ShapeDtypeStruct((bsz * t, A_WIDTH), F32)] * 2,
        compiler_params=_params("parallel", "arbitrary", "arbitrary"),
        name=f"attn_prompt_g{g}",
    )(qkv, qkv, qkv, qkv, qkv)


def _attn_sample_kernel(q_ref, kvn_ref, c1_ref, c2_ref, c3_ref, o_ref, l_ref, *, t_new):
    scale = 1.0 / math.sqrt(A_HD)
    kvh_n = 2 * A_HEADS
    n_q = t_new * A_HEADS
    n_c = SPAN * kvh_n
    n_l = n_c + LANES
    row = lax.broadcasted_iota(jnp.int32, (n_q, n_l), 0)
    lane = lax.broadcasted_iota(jnp.int32, (n_q, n_l), 1)
    t_r, h_r = row // A_HEADS, row % A_HEADS
    in_buf = lane < n_c
    j_l = lane // kvh_n
    t_l = (lane - n_c) // kvh_n
    head_ok = (lane % kvh_n) == h_r
    pad = jnp.zeros((LANES - t_new * kvh_n, A_HD), F32)
    for g, cref in enumerate((c1_ref, c2_ref, c3_ref)):
        q = q_ref[0, g].astype(BF16)
        kv_new = jnp.concatenate([kvn_ref[0, g], pad], axis=0)
        slabs = [None] if g == 0 else list(range(t_new))
        kv2, s = [], []
        for t in slabs:
            buf = cref[0, :, 0 if t is None else t].reshape(n_c, A_HD)
            kv2.append(jnp.concatenate([buf, kv_new], axis=0).astype(BF16))
            if t is None:
                valid = head_ok & ((in_buf & (j_l >= t_r)) | (~in_buf & (t_l <= t_r)))
            else:
                valid = head_ok & (t_r == t) & (in_buf | (t_l == t))
            st = lax.dot_general(q, kv2[-1], (((1,), (1,)), ((), ())), preferred_element_type=F32)
            s.append(jnp.where(valid, st * scale, NEG))
        m = functools.reduce(jnp.maximum, [jnp.max(x, axis=-1, keepdims=True) for x in s])
        p = [jnp.exp(x - m) for x in s]
        l = functools.reduce(jnp.add, [jnp.sum(x, axis=-1, keepdims=True) for x in p])
        o = functools.reduce(jnp.add, [
            jnp.dot(pltpu.roll(x, A_HEADS, 1).astype(BF16), y, preferred_element_type=F32)
            for x, y in zip(p, kv2)])
        o_ref[0, g] = o / l
        l_ref[0, g] = jnp.broadcast_to(m + jnp.log(l), (n_q, A_HD))


def _attn_sample(qkv, caches, bsz, t_new):
    kvh_n = 2 * A_HEADS
    x6 = qkv.reshape(bsz, t_new, 3, N_GROUPS, A_HEADS, A_HD)
    q_in = x6[:, :, 0].transpose(0, 2, 1, 3, 4).reshape(bsz, N_GROUPS, t_new * A_HEADS, A_HD)
    kv_new = x6[:, :, 1:3].transpose(0, 3, 1, 2, 4, 5).reshape(bsz, N_GROUPS, t_new * kvh_n, A_HD)
    views = []
    specs = []
    for (win, dil), cache in zip(A_GROUPS, caches):
        assert cache.shape[1] == win and win // dil == SPAN and (dil == 1 or t_new <= dil)
        views.append(cache.reshape(bsz, SPAN, dil, kvh_n, A_HD))
        specs.append(pl.BlockSpec((1, SPAN, min(dil, t_new), kvh_n, A_HD), lambda b: (b, 0, 0, 0, 0)))
    q_spec = pl.BlockSpec((1, N_GROUPS, t_new * A_HEADS, A_HD), lambda b: (b, 0, 0, 0))
    kvn_spec = pl.BlockSpec((1, N_GROUPS, t_new * kvh_n, A_HD), lambda b: (b, 0, 0, 0))
    o, l = pl.pallas_call(
        functools.partial(_attn_sample_kernel, t_new=t_new),
        grid=(bsz,),
        in_specs=[q_spec, kvn_spec] + specs,
        out_specs=[q_spec, q_spec],
        out_shape=[jax.ShapeDtypeStruct((bsz, N_GROUPS, t_new * A_HEADS, A_HD), F32)] * 2,
        compiler_params=_params("parallel"),
        name="attn_sample",
    )(q_in, kv_new, *views)
    back = lambda a: a.reshape(bsz, N_GROUPS, t_new, A_HEADS, A_HD).transpose(0, 2, 1, 3, 4).reshape(
        bsz * t_new, A_QKV_W)
    return back(o), back(l)


def _final_kernel(x_ref, o_ref, bon_ref, zr_ref, o1_ref, o2_ref, o3_ref, l1_ref, l2_ref, l3_ref,
                  zgg_ref, seg_ref, lnxg_ref, lnxb_ref, bg_ref, woa_ref, wob_ref, wout_ref,
                  lng_ref, lnb_ref, y_ref):
    seg = seg_ref[...] * (1.0 / R_HD)
    o = o_ref[...]
    mu = jnp.dot(o, seg, precision=HIGHEST, preferred_element_type=F32)
    d = o - mu
    var = jnp.dot(d * d, seg, precision=HIGHEST, preferred_element_type=F32)
    on = d * lax.rsqrt(var + GN_EPS) * lnxg_ref[...] + lnxb_ref[...]
    zr = zr_ref[...]
    y_r = (on + bon_ref[...]) * (zr * jax.nn.sigmoid(zr))

    l1, l2, l3 = l1_ref[...], l2_ref[...], l3_ref[...]
    mx = jnp.maximum(jnp.maximum(l1, l2), l3)
    e1, e2, e3 = jnp.exp(l1 - mx), jnp.exp(l2 - mx), jnp.exp(l3 - mx)
    den = e1 + e2 + e3
    o_a = (e1 / den) * o1_ref[...] + (e2 / den) * o2_ref[...] + (e3 / den) * o3_ref[...]
    z_a = zgg_ref[:, 0:A_WIDTH]
    y_a = o_a * (z_a * jax.nn.sigmoid(z_a))

    g_r = zgg_ref[:, A_WIDTH:A_WIDTH + D_MODEL]
    g_a = zgg_ref[:, A_WIDTH + D_MODEL:A_WIDTH + 2 * D_MODEL]
    gate_r = jax.nn.sigmoid(g_r + bg_ref[:, 0:D_MODEL])
    gate_a = jax.nn.sigmoid(g_a + bg_ref[:, D_MODEL:2 * D_MODEL])
    mix = (gate_r * jnp.dot(y_r.astype(BF16), woa_ref[...], preferred_element_type=F32)
           + gate_a * jnp.dot(y_a.astype(BF16), wob_ref[...], preferred_element_type=F32))
    yy = ALPHA * x_ref[...] + jnp.dot(mix.astype(BF16), wout_ref[...], preferred_element_type=F32)
    mean = jnp.mean(yy, axis=-1, keepdims=True)
    cen = yy - mean
    variance = jnp.mean(cen * cen, axis=-1, keepdims=True)
    y_ref[...] = cen * lax.rsqrt(variance + LN_EPS) * lng_ref[...] + lnb_ref[...]


def _final(x, o, bonus, zr, attn_o, attn_l, zgg, weights, tm):
    m = x.shape[0]
    row = lambda w, j=0: pl.BlockSpec((tm, w), lambda i: (i, j))
    const = lambda shape: pl.BlockSpec(shape, lambda i: (0, 0))
    seg, lnxg, lnxb, bg, woa, wob, wout, lng, lnb = weights
    in_specs = ([row(D_MODEL), row(R_WIDTH), row(R_WIDTH), row(R_WIDTH)]
                + [row(A_WIDTH, j) for _, j in attn_o] + [row(A_WIDTH, j) for _, j in attn_l]
                + [row(A_WIDTH + 2 * D_MODEL), const(seg.shape), const(lnxg.shape), const(lnxb.shape),
                   const(bg.shape), const(woa.shape), const(wob.shape), const(wout.shape),
                   const(lng.shape), const(lnb.shape)])
    return pl.pallas_call(
        _final_kernel,
        grid=(m // tm,),
        in_specs=in_specs,
        out_specs=row(D_MODEL),
        out_shape=jax.ShapeDtypeStruct((m, D_MODEL), F32),
        compiler_params=_params("parallel"),
        name="merge_out_ln",
    )(x, o, bonus, zr, *[a for a, _ in attn_o], *[a for a, _ in attn_l], zgg, *weights)


def _layer(x2, bsz, t, shift_prev, wkv0, caches, wts):
    m = x2.shape[0]
    tm = 512 if m % 512 == 0 else m
    zs = _matmul(x2, wts["w_zs"], tm, SHIFT_COLS)
    zr = _matmul(x2, wts["w_zr"], tm, R_WIDTH)
    qkv = _matmul(x2, wts["w_qkv"], tm, 512)
    zgg = _matmul(x2, wts["w_zgg"], tm, 512)

    if caches is None:
        first = None
        seq_len = t
    else:
        first = jnp.zeros((bsz, t, SHIFT_COLS), F32).at[:, 0].set(shift_prev).reshape(m, SHIFT_COLS)
        seq_len = t
    prep = _rwkv_prep(zs, first, seq_len, wts["prep"], tm)
    r, kp, v, kk, b, lw, bonus = prep

    t_pad = -(-t // CHUNK) * CHUNK
    def seq(a):
        a = a.reshape(bsz, t, R_WIDTH)
        return a if t_pad == t else jnp.pad(a, ((0, 0), (0, t_pad - t), (0, 0)))
    s0 = (jnp.zeros((bsz, R_HEADS // 2, LANES, LANES), F32) if wkv0 is None
          else _state_to_pairs(wkv0))
    o_seq, st = _rwkv_recurrence(seq(r), seq(kp), seq(v), seq(kk), seq(b), seq(lw), s0, 2)
    o_rwkv = o_seq[:, :t].reshape(m, R_WIDTH)
    wkv_new = _pairs_to_state(st)

    if caches is None:
        outs = [_attn_prompt(qkv, bsz, t, g, dil, max(1, 512 // (SPAN * dil)))
                for g, (_, dil) in enumerate(A_GROUPS)]
        attn_o = [(o, 0) for o, _ in outs]
        attn_l = [(l, 0) for _, l in outs]
    else:
        o_s, l_s = _attn_sample(qkv, caches, bsz, t)
        attn_o = [(o_s, g) for g in range(N_GROUPS)]
        attn_l = [(l_s, g) for g in range(N_GROUPS)]

    tmf = 256 if m % 256 == 0 else m
    y = _final(x2, o_rwkv, bonus, zr, attn_o, attn_l, zgg, wts["final"], tmf)

    qkv3 = qkv.reshape(bsz, t, 3 * A_QKV_W)
    new_kv = []
    for g, (win, _) in enumerate(A_GROUPS):
        keep = min(win, t) if caches is None else t
        part = lambda which: qkv3[:, t - keep:, (which * N_GROUPS + g) * A_WIDTH:
                                  (which * N_GROUPS + g + 1) * A_WIDTH].reshape(bsz, keep, A_HEADS, A_HD)
        new_kv.append(jnp.stack([part(1), part(2)], axis=2))
    shift_new = zs.reshape(bsz, t, SHIFT_COLS)[:, -1]
    return y.reshape(bsz, t, D_MODEL), new_kv, wkv_new, shift_new


def kernel(x_prompt, x_sample, cache_kv_g1, cache_kv_g2, cache_kv_g3, state_rwkv_wkv, state_rwkv_shift,
           w_in, b_gate, mu_shift, w0, w_w2, a0, w_a2, k_k, k_a, r_k, lnx_g, lnx_b,
           w_oa, w_ob, w_out, ln_g, ln_b):
    assert w_in.shape[0] == DEPTH
    bp, tp, _ = x_prompt.shape
    bs, ts, _ = x_sample.shape
    xp = x_prompt.reshape(bp * tp, D_MODEL)
    xs = x_sample.reshape(bs * ts, D_MODEL)
    head = lax.broadcasted_iota(jnp.int32, (R_WIDTH, R_WIDTH), 0) // R_HD
    seg = (head == head.T).astype(F32)
    row = lambda a: a.reshape(1, -1)
    acc = [[] for _ in range(12)]
    for l in range(DEPTH):
        wb = w_in[l].astype(BF16)
        c0 = SHIFT_COLS
        c1 = c0 + R_WIDTH
        c2 = c1 + 3 * A_QKV_W
        zero = jnp.zeros((LORA, R_WIDTH), F32)
        w_lora = jnp.concatenate([jnp.concatenate([w_w2[l], zero], axis=1),
                                  jnp.concatenate([zero, w_a2[l]], axis=1)], axis=0)
        wts = {
            "w_zs": wb[:, :c0], "w_zr": wb[:, c0:c1], "w_qkv": wb[:, c1:c2], "w_zgg": wb[:, c2:],
            "prep": (row(mu_shift[l]), row(w0[l]), row(a0[l]), w_lora, row(k_k[l]), row(k_a[l]),
                     row(r_k[l]), seg),
            "final": (seg, row(lnx_g[l]), row(lnx_b[l]), row(b_gate[l]), w_oa[l].astype(BF16),
                      w_ob[l].astype(BF16), w_out[l].astype(BF16), row(ln_g[l]), row(ln_b[l])),
        }
        yp, kv_p, s_p, sh_p = _layer(xp, bp, tp, None, None, None, wts)
        ys, kv_s, s_s, sh_s = _layer(xs, bs, ts, state_rwkv_shift[l], state_rwkv_wkv[l],
                                     (cache_kv_g1[l], cache_kv_g2[l], cache_kv_g3[l]), wts)
        xp = yp.reshape(bp * tp, D_MODEL)
        xs = ys.reshape(bs * ts, D_MODEL)
        for idx, val in zip(range(2, 12), (kv_p[0], kv_s[0], kv_p[1], kv_s[1], kv_p[2], kv_s[2],
                                           s_p, s_s, sh_p, sh_s)):
            acc[idx].append(val)
    outs = [xp.reshape(bp, tp, D_MODEL), xs.reshape(bs, ts, D_MODEL)]
    outs += [jnp.stack(a) for a in acc[2:]]
    return tuple(outs)
```

```python
import functools
import math

import jax
import jax.numpy as jnp
from jax import lax
from jax.experimental import pallas as pl
from jax.experimental.pallas import tpu as pltpu

F32 = jnp.float32
BF16 = jnp.bfloat16

D_MODEL = 1024
R_HEADS = 8
R_HD = 64
R_WIDTH = R_HEADS * R_HD
LORA = 64
SHIFT_COLS = 3 * R_WIDTH + 2 * LORA
GN_EPS = 64e-5
A_GROUPS = ((128, 1), (512, 4), (2048, 16))
N_GROUPS = 3
A_HEADS = 4
A_HD = 128
A_WIDTH = A_HEADS * A_HD
A_QKV_W = N_GROUPS * A_WIDTH
SPAN = 128
DEPTH = 1
ALPHA = (2 * DEPTH) ** 0.25
LN_EPS = 1e-5
NEG = -1e30

LANES = 128
CHUNK = 64
N_PAIRS = R_HEADS // 2
VMEM_LIMIT = 48 * 1024 * 1024

PROJ_TN = 512
COL_ZS = 0
COL_ZR = 2048
COL_ZGG = COL_ZR + R_WIDTH
ZGG_W = A_WIDTH + 2 * D_MODEL
COL_QKV = COL_ZGG + ZGG_W
N_PROJ = COL_QKV + 3 * A_QKV_W
assert COL_ZR % R_WIDTH == 0 and COL_ZGG % ZGG_W == 0 and COL_QKV % LANES == 0 and N_PROJ % PROJ_TN == 0


def _params(*sem):
    return pltpu.CompilerParams(dimension_semantics=sem, vmem_limit_bytes=VMEM_LIMIT)


def _split2(x):
    hi = x.astype(BF16)
    return hi, (x - hi.astype(F32)).astype(BF16)


def _seg_sum(x, seg2):
    cols = []
    for c in range(x.shape[1] // LANES):
        hi, lo = _split2(x[:, c * LANES:(c + 1) * LANES])
        cols.append(jnp.dot(jnp.concatenate([hi, lo], axis=1), seg2, preferred_element_type=F32))
    return jnp.concatenate(cols, axis=1)


def _mm_kernel(x_ref, w_ref, o_ref):
    o_ref[...] = jnp.dot(x_ref[...].astype(BF16), w_ref[...], preferred_element_type=F32)


def _in_proj(x, w, tm):
    m, k = x.shape
    n = w.shape[1]
    return pl.pallas_call(
        _mm_kernel,
        grid=(m // tm, n // PROJ_TN),
        in_specs=[pl.BlockSpec((tm, k), lambda i, j: (i, 0)),
                  pl.BlockSpec((k, PROJ_TN), lambda i, j: (0, j))],
        out_specs=pl.BlockSpec((tm, PROJ_TN), lambda i, j: (i, j)),
        out_shape=jax.ShapeDtypeStruct((m, n), F32),
        compiler_params=_params("parallel", "arbitrary"),
        name="in_proj",
    )(x, w)


def _prep_math(zs, prev, mu, w0, a0, wl_a, wl_b, k_k, k_a, r_k, seg2, outs):
    r_o, kp_o, v_o, kk_o, b_o, lw_o, bon_o = outs
    zsm = zs + mu * (prev - zs)
    r = zsm[:, 0:R_WIDTH]
    k = zsm[:, R_WIDTH:2 * R_WIDTH]
    v = zsm[:, 2 * R_WIDTH:3 * R_WIDTH]
    slab = zsm[:, 3 * R_WIDTH:SHIFT_COLS]
    lane = lax.broadcasted_iota(jnp.int32, slab.shape, 1)
    slab = jnp.where(lane < LORA, jnp.tanh(slab), slab)
    s_hi, s_lo = _split2(slab)
    lin = (jnp.dot(jnp.concatenate([s_hi, s_lo], axis=1), wl_a, preferred_element_type=F32)
           + jnp.dot(s_hi, wl_b, preferred_element_type=F32))
    wlin = w0 + lin[:, 0:R_WIDTH]
    alin = a0 + lin[:, R_WIDTH:2 * R_WIDTH]
    nw = -wlin
    softplus = jnp.maximum(nw, 0.0) + jnp.log1p(jnp.exp(-jnp.abs(nw)))
    w = -softplus - 0.5
    lw = -jnp.exp(w)
    a = jax.nn.sigmoid(alin)
    kkr = k * k_k
    kk = kkr / jnp.maximum(jnp.sqrt(_seg_sum(kkr * kkr, seg2)), 1e-12)
    kp = k * (1.0 + (a - 1.0) * k_a)
    r_o[...] = r
    kp_o[...] = kp
    v_o[...] = v
    kk_o[...] = kk
    b_o[...] = kk * a
    lw_o[...] = lw
    bon_o[...] = _seg_sum(r * kp * r_k, seg2) * v


def _prep_prompt_kernel(zs_ref, p8_ref, mu_ref, w0_ref, a0_ref, wla_ref, wlb_ref, kk_ref, ka_ref,
                        rk_ref, seg_ref, *outs, tiles_per_seq):
    i = pl.program_id(0)
    zs = zs_ref[...]
    rolled = pltpu.roll(zs, 1, 0)
    carry = jnp.where(i % tiles_per_seq == 0, 0.0, p8_ref[7:8, :])
    row = lax.broadcasted_iota(jnp.int32, zs.shape, 0)
    prev = jnp.where(row == 0, carry, rolled)
    _prep_math(zs, prev, mu_ref[...], w0_ref[...], a0_ref[...], wla_ref[...], wlb_ref[...],
               kk_ref[...], ka_ref[...], rk_ref[...], seg_ref[...], outs)


def _prep_sample_kernel(zs_ref, first_ref, mu_ref, w0_ref, a0_ref, wla_ref, wlb_ref, kk_ref, ka_ref,
                        rk_ref, seg_ref, *outs, seq_len):
    zs = zs_ref[...]
    rolled = pltpu.roll(zs, 1, 0)
    row = lax.broadcasted_iota(jnp.int32, zs.shape, 0)
    prev = jnp.where(row % seq_len == 0, first_ref[...], rolled)
    _prep_math(zs, prev, mu_ref[...], w0_ref[...], a0_ref[...], wla_ref[...], wlb_ref[...],
               kk_ref[...], ka_ref[...], rk_ref[...], seg_ref[...], outs)


def _rwkv_prep(h, first, seq_len, weights, tm):
    m = h.shape[0]
    const = lambda a: pl.BlockSpec(a.shape, lambda i: (0, 0))
    row_spec = pl.BlockSpec((tm, SHIFT_COLS), lambda i: (i, COL_ZS))
    if first is None:
        kern = functools.partial(_prep_prompt_kernel, tiles_per_seq=seq_len // tm)
        second = h
        second_spec = pl.BlockSpec((8, SHIFT_COLS), lambda i: (jnp.maximum(i * (tm // 8) - 1, 0), COL_ZS))
    else:
        kern = functools.partial(_prep_sample_kernel, seq_len=seq_len)
        second = first
        second_spec = pl.BlockSpec((tm, SHIFT_COLS), lambda i: (i, 0))
    out_spec = pl.BlockSpec((tm, R_WIDTH), lambda i: (i, 0))
    return pl.pallas_call(
        kern,
        grid=(m // tm,),
        in_specs=[row_spec, second_spec] + [const(a) for a in weights],
        out_specs=[out_spec] * 7,
        out_shape=[jax.ShapeDtypeStruct((m, R_WIDTH), F32)] * 7,
        compiler_params=_params("parallel"),
        name="rwkv_prep",
    )(h, second, *weights)


def _rwkv_chunk_kernel(r_ref, kp_ref, v_ref, kk_ref, b_ref, lw_ref, s0_ref, o_ref, st_ref, *, nb):
    n = pl.program_id(1)

    @pl.when(n == 0)
    def _():
        st_ref[...] = s0_ref[...]

    c = CHUNK
    c2 = 2 * c
    lane = lax.broadcasted_iota(jnp.int32, (1, LANES), 1)
    m_lo = (lane < R_HD).astype(F32)
    m_hi = 1.0 - m_lo
    row_c = lax.broadcasted_iota(jnp.int32, (c, 4 * c), 0)
    col_c = lax.broadcasted_iota(jnp.int32, (c, 4 * c), 1)
    tri3 = ((row_c >= col_c % c) & (col_c < 3 * c)).astype(BF16)
    row2 = lax.broadcasted_iota(jnp.int32, (c2, c2), 0)
    col2 = lax.broadcasted_iota(jnp.int32, (c2, c2), 1)
    strict = row2 > col2
    incl = row2 >= col2

    def stack(x):
        return jnp.concatenate([x * m_lo, x * m_hi], axis=0)

    def mm(a, b):
        return jnp.dot(a.astype(BF16), b.astype(BF16), preferred_element_type=F32)

    def nt(a, b):
        return lax.dot_general(a.astype(BF16), b.astype(BF16), (((1,), (1,)), ((), ())),
                               preferred_element_type=F32)

    each = lambda f, *xs: [f(*a) for a in zip(*xs)]

    def cumsum(lw_all):
        hi = lw_all.astype(BF16)
        r1 = lw_all - hi.astype(F32)
        mid = r1.astype(BF16)
        lo = (r1 - mid.astype(F32)).astype(BF16)
        return jnp.dot(tri3, jnp.concatenate([hi, mid, lo, jnp.zeros_like(lo)], axis=0),
                       preferred_element_type=F32)

    g_all = [cumsum(lw_ref[bi]) for bi in range(nb)]

    chains = [(bi, slice(p * LANES, (p + 1) * LANES), p) for bi in range(nb) for p in range(N_PAIRS)]
    load = lambda ref: [ref[bi, :, sl] for bi, sl, _ in chains]
    r, kp, v, kk, b, lw = (load(ref) for ref in (r_ref, kp_ref, v_ref, kk_ref, b_ref, lw_ref))
    st = [st_ref[bi, p] for bi, _, p in chains]
    g = [g_all[bi][:, sl] for bi, sl, _ in chains]
    g_last = each(lambda x: x[c - 1:c, :], g)
    e_g = each(jnp.exp, g)
    e_ng = each(lambda x: jnp.exp(-x), g)
    e_gm = each(lambda x, y: jnp.exp(x - y), g, lw)
    e_cg = each(lambda x, y: jnp.exp(y - x), g, g_last)
    ad_s = each(lambda x, y: stack(-(x * y)), kk, e_gm)
    rd_s = each(lambda x, y: stack(x * y), r, e_g)
    bd_s = each(lambda x, y: stack(x * y), b, e_ng)
    kd_s = each(lambda x, y: stack(x * y), kp, e_ng)
    be_s = each(lambda x, y: stack(x * y), b, e_cg)
    ke_s = each(lambda x, y: stack(x * y), kp, e_cg)
    v_s = each(stack, v)

    z = each(lambda a1, a2, b1, b2: nt(jnp.concatenate([a1, a2], axis=0),
                                       jnp.concatenate([b1, b2], axis=0)), ad_s, rd_s, bd_s, kd_s)
    n_b = each(lambda x: jnp.where(strict, x[0:c2, 0:c2], 0.0), z)
    n_k = each(lambda x: jnp.where(strict, x[0:c2, c2:2 * c2], 0.0), z)
    m_b = each(lambda x: jnp.where(incl, x[c2:2 * c2, 0:c2], 0.0), z)
    m_k = each(lambda x: jnp.where(incl, x[c2:2 * c2, c2:2 * c2], 0.0), z)

    yo = each(lambda a1, nk, rd, mk, s, vs: mm(
        jnp.concatenate([jnp.concatenate([a1, nk], axis=1), jnp.concatenate([rd, mk], axis=1)], axis=0),
        jnp.concatenate([s, vs], axis=0)), ad_s, n_k, rd_s, m_k, st, v_s)
    y = each(lambda x: x[0:c2], yo)
    o_part = each(lambda x: x[c2:2 * c2], yo)

    q = n_b
    steps = int(math.log2(c))
    for i in range(steps):
        if i < steps - 1:
            res = each(lambda x, yy: mm(x, jnp.concatenate([x, yy], axis=1)), q, y)
            q = each(lambda x: x[:, 0:c2], res)
            y = each(lambda yy, x: yy + x[:, c2:2 * c2], y, res)
        else:
            y = each(lambda yy, x: yy + mm(x, yy), y, q)
    u = y

    o_bd = each(lambda op, mb, uu: op + mm(mb, uu), o_part, m_b, u)
    for (bi, sl, _), o in zip(chains, o_bd):
        o_ref[bi, :, sl] = o[0:c] + o[c:c2]

    pc = each(lambda x: jnp.transpose(jnp.broadcast_to(jnp.exp(x), (LANES, LANES))), g_last)
    upd = each(lambda be, ke, uu, vs: mm(jnp.concatenate([jnp.transpose(be), jnp.transpose(ke)], axis=1),
                                         jnp.concatenate([uu, vs], axis=0)), be_s, ke_s, u, v_s)
    for (bi, _, p), dec, s, d in zip(chains, pc, st, upd):
        st_ref[bi, p] = dec * s + d


def _rwkv_recurrence(r, kp, v, kk, b, lw, s0, nb):
    bsz, t, _ = r.shape
    seq_spec = pl.BlockSpec((nb, CHUNK, R_WIDTH), lambda i, n: (i, n, 0))
    st_spec = pl.BlockSpec((nb, N_PAIRS, LANES, LANES), lambda i, n: (i, 0, 0, 0))
    return pl.pallas_call(
        functools.partial(_rwkv_chunk_kernel, nb=nb),
        grid=(bsz // nb, t // CHUNK),
        in_specs=[seq_spec] * 6 + [st_spec],
        out_specs=[seq_spec, st_spec],
        out_shape=[jax.ShapeDtypeStruct((bsz, t, R_WIDTH), F32),
                   jax.ShapeDtypeStruct(s0.shape, F32)],
        compiler_params=_params("parallel", "arbitrary"),
        name="rwkv_chunks",
    )(r, kp, v, kk, b, lw, s0)


def _state_to_pairs(s):
    bsz = s.shape[0]
    st = jnp.swapaxes(s, -1, -2).reshape(bsz, N_PAIRS, 2, R_HD, R_HD)
    out = jnp.zeros((bsz, N_PAIRS, 2, R_HD, 2, R_HD), F32)
    out = out.at[:, :, 0, :, 0, :].set(st[:, :, 0])
    out = out.at[:, :, 1, :, 1, :].set(st[:, :, 1])
    return out.reshape(bsz, N_PAIRS, LANES, LANES)


def _pairs_to_state(st):
    bsz = st.shape[0]
    s6 = st.reshape(bsz, N_PAIRS, 2, R_HD, 2, R_HD)
    s = jnp.stack([s6[:, :, 0, :, 0, :], s6[:, :, 1, :, 1, :]], axis=2)
    return jnp.swapaxes(s.reshape(bsz, R_HEADS, R_HD, R_HD), -1, -2)


def _attn_prompt_kernel(q_ref, kp_ref, kc_ref, vp_ref, vc_ref, o_ref, l_ref, *, dil, nsub):
    n = pl.program_id(1)
    i = lax.broadcasted_iota(jnp.int32, (SPAN, 2 * SPAN), 0)
    j = lax.broadcasted_iota(jnp.int32, (SPAN, 2 * SPAN), 1)
    dist = SPAN + i - j
    band = (dist >= 0) & (dist <= SPAN)
    band_first = band & ((n > 0) | (j >= SPAN))
    scale = 1.0 / math.sqrt(A_HD)

    def rows(run, rho):
        start = run * SPAN * dil + rho
        return pl.ds(start, SPAN, stride=dil) if dil > 1 else pl.ds(start, SPAN)

    probs = [(run, rho) for run in range(nsub) for rho in range(dil)]
    cat = lambda a, b: jnp.concatenate([a, b], axis=0).astype(BF16)
    q = [q_ref[rows(run, rho), :].astype(BF16) for run, rho in probs]
    k = [cat(kp_ref[rows(0, rho), :] if run == 0 else kc_ref[rows(run - 1, rho), :],
             kc_ref[rows(run, rho), :]) for run, rho in probs]
    v = [cat(vp_ref[rows(0, rho), :] if run == 0 else vc_ref[rows(run - 1, rho), :],
             vc_ref[rows(run, rho), :]) for run, rho in probs]
    s = [lax.dot_general(a, b, (((1,), (1,)), ((), ())), preferred_element_type=F32) * scale
         for a, b in zip(q, k)]
    s = [jnp.where(band_first if run == 0 else band, x, NEG) for x, (run, _) in zip(s, probs)]
    m = [jnp.max(x, axis=-1, keepdims=True) for x in s]
    p = [jnp.exp(x - y) for x, y in zip(s, m)]
    l = [jnp.sum(x, axis=-1, keepdims=True) for x in p]
    o = [jnp.dot(x.astype(BF16), y, preferred_element_type=F32) for x, y in zip(p, v)]
    for (run, rho), oo, ll, mm in zip(probs, o, l, m):
        o_ref[rows(run, rho), :] = oo / ll
        l_ref[rows(run, rho), :] = jnp.broadcast_to(mm + jnp.log(ll), (SPAN, A_HD))


def _attn_prompt(h, bsz, t, g, dil, nsub):
    prev_rows = SPAN * dil
    blk_rows = nsub * prev_rows
    nblk = t // blk_rows
    col = lambda which, hd: COL_QKV // A_HD + (which * N_GROUPS + g) * A_HEADS + hd
    cur = lambda which: pl.BlockSpec((blk_rows, A_HD), lambda b, n, hd: (b * nblk + n, col(which, hd)))
    prev = lambda which: pl.BlockSpec(
        (prev_rows, A_HD), lambda b, n, hd: (jnp.maximum((b * nblk + n) * nsub - 1, 0), col(which, hd)))
    out_spec = pl.BlockSpec((blk_rows, A_HD), lambda b, n, hd: (b * nblk + n, hd))
    return pl.pallas_call(
        functools.partial(_attn_prompt_kernel, dil=dil, nsub=nsub),
        grid=(bsz, nblk, A_HEADS),
        in_specs=[cur(0), prev(1), cur(1), prev(2), cur(2)],
        out_specs=[out_spec, out_spec],
        out_shape=[jax.ShapeDtypeStruct((bsz * t, A_WIDTH), F32)] * 2,
        compiler_params=_params("parallel", "arbitrary", "arbitrary"),
        name=f"attn_prompt_g{g}",
    )(h, h, h, h, h)


def _attn_sample_kernel(q_ref, kvn_ref, c1_ref, c2_ref, c3_ref, o_ref, l_ref, *, t_new):
    scale = 1.0 / math.sqrt(A_HD)
    kvh_n = 2 * A_HEADS
    n_q = t_new * A_HEADS
    n_c = SPAN * kvh_n
    n_l = n_c + LANES
    row = lax.broadcasted_iota(jnp.int32, (n_q, n_l), 0)
    lane = lax.broadcasted_iota(jnp.int32, (n_q, n_l), 1)
    t_r, h_r = row // A_HEADS, row % A_HEADS
    in_buf = lane < n_c
    j_l = lane // kvh_n
    t_l = (lane - n_c) // kvh_n
    head_ok = (lane % kvh_n) == h_r
    pad = jnp.zeros((LANES - t_new * kvh_n, A_HD), F32)
    for g, cref in enumerate((c1_ref, c2_ref, c3_ref)):
        q = q_ref[0, g].astype(BF16)
        kv_new = jnp.concatenate([kvn_ref[0, g], pad], axis=0)
        slabs = [None] if g == 0 else list(range(t_new))
        kv2, s = [], []
        for t in slabs:
            buf = cref[0, :, 0 if t is None else t].reshape(n_c, A_HD)
            kv2.append(jnp.concatenate([buf, kv_new], axis=0).astype(BF16))
            if t is None:
                valid = head_ok & ((in_buf & (j_l >= t_r)) | (~in_buf & (t_l <= t_r)))
            else:
                valid = head_ok & (t_r == t) & (in_buf | (t_l == t))
            st = lax.dot_general(q, kv2[-1], (((1,), (1,)), ((), ())), preferred_element_type=F32)
            s.append(jnp.where(valid, st * scale, NEG))
        m = functools.reduce(jnp.maximum, [jnp.max(x, axis=-1, keepdims=True) for x in s])
        p = [jnp.exp(x - m) for x in s]
        l = functools.reduce(jnp.add, [jnp.sum(x, axis=-1, keepdims=True) for x in p])
        o = functools.reduce(jnp.add, [
            jnp.dot(pltpu.roll(x, A_HEADS, 1).astype(BF16), y, preferred_element_type=F32)
            for x, y in zip(p, kv2)])
        o_ref[0, g] = o / l
        l_ref[0, g] = jnp.broadcast_to(m + jnp.log(l), (n_q, A_HD))


def _attn_sample(qkv, caches, bsz, t_new):
    kvh_n = 2 * A_HEADS
    x6 = qkv.reshape(bsz, t_new, 3, N_GROUPS, A_HEADS, A_HD)
    q_in = x6[:, :, 0].transpose(0, 2, 1, 3, 4).reshape(bsz, N_GROUPS, t_new * A_HEADS, A_HD)
    kv_new = x6[:, :, 1:3].transpose(0, 3, 1, 2, 4, 5).reshape(bsz, N_GROUPS, t_new * kvh_n, A_HD)
    views = []
    specs = []
    for (win, dil), cache in zip(A_GROUPS, caches):
        assert cache.shape[1] == win and win // dil == SPAN and (dil == 1 or t_new <= dil)
        views.append(cache.reshape(bsz, SPAN, dil, kvh_n, A_HD))
        specs.append(pl.BlockSpec((1, SPAN, min(dil, t_new), kvh_n, A_HD), lambda b: (b, 0, 0, 0, 0)))
    q_spec = pl.BlockSpec((1, N_GROUPS, t_new * A_HEADS, A_HD), lambda b: (b, 0, 0, 0))
    kvn_spec = pl.BlockSpec((1, N_GROUPS, t_new * kvh_n, A_HD), lambda b: (b, 0, 0, 0))
    o, l = pl.pallas_call(
        functools.partial(_attn_sample_kernel, t_new=t_new),
        grid=(bsz,),
        in_specs=[q_spec, kvn_spec] + specs,
        out_specs=[q_spec, q_spec],
        out_shape=[jax.ShapeDtypeStruct((bsz, N_GROUPS, t_new * A_HEADS, A_HD), F32)] * 2,
        compiler_params=_params("parallel"),
        name="attn_sample",
    )(q_in, kv_new, *views)
    back = lambda a: a.reshape(bsz, N_GROUPS, t_new, A_HEADS, A_HD).transpose(0, 2, 1, 3, 4).reshape(
        bsz * t_new, A_QKV_W)
    return back(o), back(l)


def _final_kernel(x_ref, o_ref, bon_ref, zr_ref, o1_ref, o2_ref, o3_ref, l1_ref, l2_ref, l3_ref,
                  zgg_ref, seg_ref, lnxg_ref, lnxb_ref, bg_ref, woa_ref, wob_ref, wout_ref,
                  lng_ref, lnb_ref, y_ref):
    seg2 = seg_ref[...]
    o = o_ref[...]
    mu = _seg_sum(o, seg2) * (1.0 / R_HD)
    d = o - mu
    var = _seg_sum(d * d, seg2) * (1.0 / R_HD)
    on = d * lax.rsqrt(var + GN_EPS) * lnxg_ref[...] + lnxb_ref[...]
    zr = zr_ref[...]
    y_r = (on + bon_ref[...]) * (zr * jax.nn.sigmoid(zr))

    l1, l2, l3 = l1_ref[...], l2_ref[...], l3_ref[...]
    mx = jnp.maximum(jnp.maximum(l1, l2), l3)
    e1, e2, e3 = jnp.exp(l1 - mx), jnp.exp(l2 - mx), jnp.exp(l3 - mx)
    den = e1 + e2 + e3
    o_a = (e1 / den) * o1_ref[...] + (e2 / den) * o2_ref[...] + (e3 / den) * o3_ref[...]
    z_a = zgg_ref[:, 0:A_WIDTH]
    y_a = o_a * (z_a * jax.nn.sigmoid(z_a))

    g_r = zgg_ref[:, A_WIDTH:A_WIDTH + D_MODEL]
    g_a = zgg_ref[:, A_WIDTH + D_MODEL:A_WIDTH + 2 * D_MODEL]
    gate_r = jax.nn.sigmoid(g_r + bg_ref[:, 0:D_MODEL])
    gate_a = jax.nn.sigmoid(g_a + bg_ref[:, D_MODEL:2 * D_MODEL])
    mix = (gate_r * jnp.dot(y_r.astype(BF16), woa_ref[...], preferred_element_type=F32)
           + gate_a * jnp.dot(y_a.astype(BF16), wob_ref[...], preferred_element_type=F32))
    yy = ALPHA * x_ref[...] + jnp.dot(mix.astype(BF16), wout_ref[...], preferred_element_type=F32)
    mean = jnp.mean(yy, axis=-1, keepdims=True)
    cen = yy - mean
    variance = jnp.mean(cen * cen, axis=-1, keepdims=True)
    y_ref[...] = cen * lax.rsqrt(variance + LN_EPS) * lng_ref[...] + lnb_ref[...]


def _final(x, o, bonus, h, attn_o, attn_l, weights, tm):
    m = x.shape[0]
    row = lambda w, j=0: pl.BlockSpec((tm, w), lambda i: (i, j))
    const = lambda a: pl.BlockSpec(a.shape, lambda i: (0, 0))
    in_specs = ([row(D_MODEL), row(R_WIDTH), row(R_WIDTH), row(R_WIDTH, COL_ZR // R_WIDTH)]
                + [row(A_WIDTH, j) for _, j in attn_o] + [row(A_WIDTH, j) for _, j in attn_l]
                + [row(ZGG_W, COL_ZGG // ZGG_W)] + [const(a) for a in weights])
    return pl.pallas_call(
        _final_kernel,
        grid=(m // tm,),
        in_specs=in_specs,
        out_specs=row(D_MODEL),
        out_shape=jax.ShapeDtypeStruct((m, D_MODEL), F32),
        compiler_params=_params("parallel"),
        name="merge_out_ln",
    )(x, o, bonus, h, *[a for a, _ in attn_o], *[a for a, _ in attn_l], h, *weights)


def _layer(x2, bsz, t, shift_prev, wkv0, caches, wts):
    m = x2.shape[0]
    h = _in_proj(x2, wts["w_proj"], 1024 if m % 1024 == 0 else m)

    tm = 512 if m % 512 == 0 else m
    if caches is None:
        first = None
    else:
        first = jnp.zeros((bsz, t, SHIFT_COLS), F32).at[:, 0].set(shift_prev).reshape(m, SHIFT_COLS)
    r, kp, v, kk, b, lw, bonus = _rwkv_prep(h, first, t, wts["prep"], tm)

    t_pad = -(-t // CHUNK) * CHUNK
    def seq(a):
        a = a.reshape(bsz, t, R_WIDTH)
        return a if t_pad == t else jnp.pad(a, ((0, 0), (0, t_pad - t), (0, 0)))
    s0 = (jnp.zeros((bsz, N_PAIRS, LANES, LANES), F32) if wkv0 is None else _state_to_pairs(wkv0))
    o_seq, st = _rwkv_recurrence(seq(r), seq(kp), seq(v), seq(kk), seq(b), seq(lw), s0, 2)
    o_rwkv = o_seq[:, :t].reshape(m, R_WIDTH)
    wkv_new = _pairs_to_state(st)

    def rows_cols(r0, c0, width):
        if t % 8 == 0:
            return h.reshape(bsz, t, N_PROJ)[:, r0:, c0:c0 + width]
        return h[:, c0:c0 + width].reshape(bsz, t, width)[:, r0:]

    if caches is None:
        outs = [_attn_prompt(h, bsz, t, g, dil, max(1, 1024 // (SPAN * dil)))
                for g, (_, dil) in enumerate(A_GROUPS)]
        attn_o = [(o, 0) for o, _ in outs]
        attn_l = [(l, 0) for _, l in outs]
    else:
        o_s, l_s = _attn_sample(h[:, COL_QKV:], caches, bsz, t)
        attn_o = [(o_s, g) for g in range(N_GROUPS)]
        attn_l = [(l_s, g) for g in range(N_GROUPS)]

    tmf = 256 if m % 256 == 0 else m
    y = _final(x2, o_rwkv, bonus, h, attn_o, attn_l, wts["final"], tmf)

    new_kv = []
    for g, (win, _) in enumerate(A_GROUPS):
        keep = min(win, t) if caches is None else t
        def part(which):
            c0 = COL_QKV + (which * N_GROUPS + g) * A_WIDTH
            return rows_cols(t - keep, c0, A_WIDTH).reshape(bsz, keep, A_HEADS, A_HD)
        new_kv.append(jnp.stack([part(1), part(2)], axis=2))
    shift_new = rows_cols(t - 1, COL_ZS, SHIFT_COLS)[:, 0]
    return y.reshape(bsz, t, D_MODEL), new_kv, wkv_new, shift_new


def kernel(x_prompt, x_sample, cache_kv_g1, cache_kv_g2, cache_kv_g3, state_rwkv_wkv, state_rwkv_shift,
           w_in, b_gate, mu_shift, w0, w_w2, a0, w_a2, k_k, k_a, r_k, lnx_g, lnx_b,
           w_oa, w_ob, w_out, ln_g, ln_b):
    assert w_in.shape[0] == DEPTH
    bp, tp, _ = x_prompt.shape
    bs, ts, _ = x_sample.shape
    xp = x_prompt.reshape(bp * tp, D_MODEL)
    xs = x_sample.reshape(bs * ts, D_MODEL)
    head = lax.broadcasted_iota(jnp.int32, (LANES, LANES), 0) // R_HD
    seg = (head == head.T).astype(BF16)
    seg2 = jnp.concatenate([seg, seg], axis=0)
    row = lambda a: a.reshape(1, -1)
    acc = [[] for _ in range(12)]
    for l in range(DEPTH):
        wb = w_in[l].astype(BF16)
        c0 = SHIFT_COLS
        c1 = c0 + R_WIDTH
        c2 = c1 + 3 * A_QKV_W
        w_proj = jnp.concatenate([wb[:, :c0], jnp.zeros((D_MODEL, COL_ZR - c0), BF16), wb[:, c0:c1],
                                  wb[:, c2:], wb[:, c1:c2]], axis=1)
        zero = jnp.zeros((LORA, R_WIDTH), F32)
        w_lora = jnp.concatenate([jnp.concatenate([w_w2[l], zero], axis=1),
                                  jnp.concatenate([zero, w_a2[l]], axis=1)], axis=0)
        wl_hi = w_lora.astype(BF16)
        wl_lo = (w_lora - wl_hi.astype(F32)).astype(BF16)
        wts = {
            "w_proj": w_proj,
            "prep": (row(mu_shift[l]), row(w0[l]), row(a0[l]), jnp.concatenate([wl_hi, wl_hi], axis=0),
                     wl_lo, row(k_k[l]), row(k_a[l]), row(r_k[l]), seg2),
            "final": (seg2, row(lnx_g[l]), row(lnx_b[l]), row(b_gate[l]), w_oa[l].astype(BF16),
                      w_ob[l].astype(BF16), w_out[l].astype(BF16), row(ln_g[l]), row(ln_b[l])),
        }
        yp, kv_p, s_p, sh_p = _layer(xp, bp, tp, None, None, None, wts)
        ys, kv_s, s_s, sh_s = _layer(xs, bs, ts, state_rwkv_shift[l], state_rwkv_wkv[l],
                                     (cache_kv_g1[l], cache_kv_g2[l], cache_kv_g3[l]), wts)
        xp = yp.reshape(bp * tp, D_MODEL)
        xs = ys.reshape(bs * ts, D_MODEL)
        for idx, val in zip(range(2, 12), (kv_p[0], kv_s[0], kv_p[1], kv_s[1], kv_p[2], kv_s[2],
                                           s_p, s_s, sh_p, sh_s)):
            acc[idx].append(val)
    outs = [xp.reshape(bp, tp, D_MODEL), xs.reshape(bs, ts, D_MODEL)]
    outs += [jnp.stack(a) for a in acc[2:]]
    return tuple(outs)
```

```python
import functools
import math

import jax
import jax.numpy as jnp
from jax import lax
from jax.experimental import pallas as pl
from jax.experimental.pallas import tpu as pltpu

F32 = jnp.float32
BF16 = jnp.bfloat16

D_MODEL = 1024
R_HEADS = 8
R_HD = 64
R_WIDTH = R_HEADS * R_HD
LORA = 64
SHIFT_COLS = 3 * R_WIDTH + 2 * LORA
GN_EPS = 64e-5
A_GROUPS = ((128, 1), (512, 4), (2048, 16))
N_GROUPS = 3
A_HEADS = 4
A_HD = 128
A_WIDTH = A_HEADS * A_HD
A_QKV_W = N_GROUPS * A_WIDTH
SPAN = 128
DEPTH = 1
ALPHA = (2 * DEPTH) ** 0.25
LN_EPS = 1e-5
NEG = -1e30

LANES = 128
CHUNK = 64
N_PAIRS = R_HEADS // 2
VMEM_LIMIT = 48 * 1024 * 1024

PROJ_TN = 512
COL_ZS = 0
COL_ZR = 2048
COL_ZGG = COL_ZR + R_WIDTH
ZGG_W = A_WIDTH + 2 * D_MODEL
COL_QKV = COL_ZGG + ZGG_W
N_PROJ = COL_QKV + 3 * A_QKV_W
assert COL_ZR % R_WIDTH == 0 and COL_ZGG % ZGG_W == 0 and COL_QKV % LANES == 0 and N_PROJ % PROJ_TN == 0


def _params(*sem):
    return pltpu.CompilerParams(dimension_semantics=sem, vmem_limit_bytes=VMEM_LIMIT)


def _split2(x):
    hi = x.astype(BF16)
    return hi, (x - hi.astype(F32)).astype(BF16)


def _seg_sum(x, seg2):
    cols = []
    for c in range(x.shape[1] // LANES):
        hi, lo = _split2(x[:, c * LANES:(c + 1) * LANES])
        cols.append(jnp.dot(jnp.concatenate([hi, lo], axis=1), seg2, preferred_element_type=F32))
    return jnp.concatenate(cols, axis=1)


def _mm_kernel(x_ref, w_ref, o_ref):
    o_ref[...] = jnp.dot(x_ref[...].astype(BF16), w_ref[...], preferred_element_type=F32)


def _in_proj(x, w, tm):
    m, k = x.shape
    n = w.shape[1]
    return pl.pallas_call(
        _mm_kernel,
        grid=(m // tm, n // PROJ_TN),
        in_specs=[pl.BlockSpec((tm, k), lambda i, j: (i, 0)),
                  pl.BlockSpec((k, PROJ_TN), lambda i, j: (0, j))],
        out_specs=pl.BlockSpec((tm, PROJ_TN), lambda i, j: (i, j)),
        out_shape=jax.ShapeDtypeStruct((m, n), F32),
        compiler_params=_params("parallel", "arbitrary"),
        name="in_proj",
    )(x, w)


def _prep_math(zs, prev, mu, w0, a0, wl_a, wl_b, k_k, k_a, r_k, seg2, outs):
    r_o, kp_o, v_o, kk_o, b_o, lw_o, bon_o = outs
    zsm = zs + mu * (prev - zs)
    r = zsm[:, 0:R_WIDTH]
    k = zsm[:, R_WIDTH:2 * R_WIDTH]
    v = zsm[:, 2 * R_WIDTH:3 * R_WIDTH]
    slab = zsm[:, 3 * R_WIDTH:SHIFT_COLS]
    lane = lax.broadcasted_iota(jnp.int32, slab.shape, 1)
    slab = jnp.where(lane < LORA, jnp.tanh(slab), slab)
    s_hi, s_lo = _split2(slab)
    lin = (jnp.dot(jnp.concatenate([s_hi, s_lo], axis=1), wl_a, preferred_element_type=F32)
           + jnp.dot(s_hi, wl_b, preferred_element_type=F32))
    wlin = w0 + lin[:, 0:R_WIDTH]
    alin = a0 + lin[:, R_WIDTH:2 * R_WIDTH]
    nw = -wlin
    softplus = jnp.maximum(nw, 0.0) + jnp.log1p(jnp.exp(-jnp.abs(nw)))
    w = -softplus - 0.5
    lw = -jnp.exp(w)
    a = jax.nn.sigmoid(alin)
    kkr = k * k_k
    kk = kkr / jnp.maximum(jnp.sqrt(_seg_sum(kkr * kkr, seg2)), 1e-12)
    kp = k * (1.0 + (a - 1.0) * k_a)
    r_o[...] = r
    kp_o[...] = kp
    v_o[...] = v
    kk_o[...] = kk
    b_o[...] = kk * a
    lw_o[...] = lw
    bon_o[...] = _seg_sum(r * kp * r_k, seg2) * v


def _prep_prompt_kernel(zs_ref, p8_ref, mu_ref, w0_ref, a0_ref, wla_ref, wlb_ref, kk_ref, ka_ref,
                        rk_ref, seg_ref, *outs, tiles_per_seq):
    i = pl.program_id(0)
    zs = zs_ref[...]
    rolled = pltpu.roll(zs, 1, 0)
    carry = jnp.where(i % tiles_per_seq == 0, 0.0, p8_ref[7:8, :])
    row = lax.broadcasted_iota(jnp.int32, zs.shape, 0)
    prev = jnp.where(row == 0, carry, rolled)
    _prep_math(zs, prev, mu_ref[...], w0_ref[...], a0_ref[...], wla_ref[...], wlb_ref[...],
               kk_ref[...], ka_ref[...], rk_ref[...], seg_ref[...], outs)


def _prep_sample_kernel(zs_ref, first_ref, mu_ref, w0_ref, a0_ref, wla_ref, wlb_ref, kk_ref, ka_ref,
                        rk_ref, seg_ref, *outs, seq_len):
    zs = zs_ref[...]
    rolled = pltpu.roll(zs, 1, 0)
    row = lax.broadcasted_iota(jnp.int32, zs.shape, 0)
    prev = jnp.where(row % seq_len == 0, first_ref[...], rolled)
    _prep_math(zs, prev, mu_ref[...], w0_ref[...], a0_ref[...], wla_ref[...], wlb_ref[...],
               kk_ref[...], ka_ref[...], rk_ref[...], seg_ref[...], outs)


def _rwkv_prep(h, first, seq_len, weights, tm):
    m = h.shape[0]
    const = lambda a: pl.BlockSpec(a.shape, lambda i: (0, 0))
    row_spec = pl.BlockSpec((tm, SHIFT_COLS), lambda i: (i, COL_ZS))
    if first is None:
        kern = functools.partial(_prep_prompt_kernel, tiles_per_seq=seq_len // tm)
        second = h
        second_spec = pl.BlockSpec((8, SHIFT_COLS), lambda i: (jnp.maximum(i * (tm // 8) - 1, 0), COL_ZS))
    else:
        kern = functools.partial(_prep_sample_kernel, seq_len=seq_len)
        second = first
        second_spec = pl.BlockSpec((tm, SHIFT_COLS), lambda i: (i, 0))
    out_spec = pl.BlockSpec((tm, R_WIDTH), lambda i: (i, 0))
    return pl.pallas_call(
        kern,
        grid=(m // tm,),
        in_specs=[row_spec, second_spec] + [const(a) for a in weights],
        out_specs=[out_spec] * 7,
        out_shape=[jax.ShapeDtypeStruct((m, R_WIDTH), F32)] * 7,
        compiler_params=_params("parallel"),
        name="rwkv_prep",
    )(h, second, *weights)


def _rwkv_chunk_kernel(r_ref, kp_ref, v_ref, kk_ref, b_ref, lw_ref, s0_ref, o_ref, st_ref, *, nb):
    n = pl.program_id(1)

    @pl.when(n == 0)
    def _():
        st_ref[...] = s0_ref[...]

    c = CHUNK
    c2 = 2 * c
    lane = lax.broadcasted_iota(jnp.int32, (1, LANES), 1)
    m_lo = (lane < R_HD).astype(F32)
    m_hi = 1.0 - m_lo
    row_c = lax.broadcasted_iota(jnp.int32, (c, 4 * c), 0)
    col_c = lax.broadcasted_iota(jnp.int32, (c, 4 * c), 1)
    tri3 = ((row_c >= col_c % c) & (col_c < 3 * c)).astype(BF16)
    row2 = lax.broadcasted_iota(jnp.int32, (c2, c2), 0)
    col2 = lax.broadcasted_iota(jnp.int32, (c2, c2), 1)
    strict = row2 > col2
    incl = row2 >= col2

    def stack(x):
        return jnp.concatenate([x * m_lo, x * m_hi], axis=0)

    def mm(a, b):
        return jnp.dot(a.astype(BF16), b.astype(BF16), preferred_element_type=F32)

    def nt(a, b):
        return lax.dot_general(a.astype(BF16), b.astype(BF16), (((1,), (1,)), ((), ())),
                               preferred_element_type=F32)

    each = lambda f, *xs: [f(*a) for a in zip(*xs)]

    def cumsum(lw_all):
        hi = lw_all.astype(BF16)
        r1 = lw_all - hi.astype(F32)
        mid = r1.astype(BF16)
        lo = (r1 - mid.astype(F32)).astype(BF16)
        return jnp.dot(tri3, jnp.concatenate([hi, mid, lo, jnp.zeros_like(lo)], axis=0),
                       preferred_element_type=F32)

    g_all = [cumsum(lw_ref[bi]) for bi in range(nb)]

    chains = [(bi, slice(p * LANES, (p + 1) * LANES), p) for bi in range(nb) for p in range(N_PAIRS)]
    load = lambda ref: [ref[bi, :, sl] for bi, sl, _ in chains]
    r, kp, v, kk, b, lw = (load(ref) for ref in (r_ref, kp_ref, v_ref, kk_ref, b_ref, lw_ref))
    st = [st_ref[bi, p] for bi, _, p in chains]
    g = [g_all[bi][:, sl] for bi, sl, _ in chains]
    g_last = each(lambda x: x[c - 1:c, :], g)
    e_g = each(jnp.exp, g)
    e_ng = each(lambda x: jnp.exp(-x), g)
    e_gm = each(lambda x, y: jnp.exp(x - y), g, lw)
    e_cg = each(lambda x, y: jnp.exp(y - x), g, g_last)
    ad_s = each(lambda x, y: stack(-(x * y)), kk, e_gm)
    rd_s = each(lambda x, y: stack(x * y), r, e_g)
    bd_s = each(lambda x, y: stack(x * y), b, e_ng)
    kd_s = each(lambda x, y: stack(x * y), kp, e_ng)
    be_s = each(lambda x, y: stack(x * y), b, e_cg)
    ke_s = each(lambda x, y: stack(x * y), kp, e_cg)
    v_s = each(stack, v)

    z = each(lambda a1, a2, b1, b2: nt(jnp.concatenate([a1, a2], axis=0),
                                       jnp.concatenate([b1, b2], axis=0)), ad_s, rd_s, bd_s, kd_s)
    n_b = each(lambda x: jnp.where(strict, x[0:c2, 0:c2], 0.0), z)
    n_k = each(lambda x: jnp.where(strict, x[0:c2, c2:2 * c2], 0.0), z)
    m_b = each(lambda x: jnp.where(incl, x[c2:2 * c2, 0:c2], 0.0), z)
    m_k = each(lambda x: jnp.where(incl, x[c2:2 * c2, c2:2 * c2], 0.0), z)

    yo = each(lambda a1, nk, rd, mk, s, vs: mm(
        jnp.concatenate([jnp.concatenate([a1, nk], axis=1), jnp.concatenate([rd, mk], axis=1)], axis=0),
        jnp.concatenate([s, vs], axis=0)), ad_s, n_k, rd_s, m_k, st, v_s)
    y = each(lambda x: x[0:c2], yo)
    o_part = each(lambda x: x[c2:2 * c2], yo)

    q = n_b
    steps = int(math.log2(c))
    for i in range(steps):
        if i < steps - 1:
            res = each(lambda x, yy: mm(x, jnp.concatenate([x, yy], axis=1)), q, y)
            q = each(lambda x: x[:, 0:c2], res)
            y = each(lambda yy, x: yy + x[:, c2:2 * c2], y, res)
        else:
            y = each(lambda yy, x: yy + mm(x, yy), y, q)
    u = y

    o_bd = each(lambda op, mb, uu: op + mm(mb, uu), o_part, m_b, u)
    for (bi, sl, _), o in zip(chains, o_bd):
        o_ref[bi, :, sl] = o[0:c] + o[c:c2]

    pc = each(lambda x: jnp.transpose(jnp.broadcast_to(jnp.exp(x), (LANES, LANES))), g_last)
    upd = each(lambda be, ke, uu, vs: mm(jnp.concatenate([jnp.transpose(be), jnp.transpose(ke)], axis=1),
                                         jnp.concatenate([uu, vs], axis=0)), be_s, ke_s, u, v_s)
    for (bi, _, p), dec, s, d in zip(chains, pc, st, upd):
        st_ref[bi, p] = dec * s + d


def _rwkv_recurrence(r, kp, v, kk, b, lw, s0, nb):
    bsz, t, _ = r.shape
    seq_spec = pl.BlockSpec((nb, CHUNK, R_WIDTH), lambda i, n: (i, n, 0))
    st_spec = pl.BlockSpec((nb, N_PAIRS, LANES, LANES), lambda i, n: (i, 0, 0, 0))
    return pl.pallas_call(
        functools.partial(_rwkv_chunk_kernel, nb=nb),
        grid=(bsz // nb, t // CHUNK),
        in_specs=[seq_spec] * 6 + [st_spec],
        out_specs=[seq_spec, st_spec],
        out_shape=[jax.ShapeDtypeStruct((bsz, t, R_WIDTH), F32),
                   jax.ShapeDtypeStruct(s0.shape, F32)],
        compiler_params=_params("parallel", "arbitrary"),
        name="rwkv_chunks",
    )(r, kp, v, kk, b, lw, s0)


def _pairs_to_state(st):
    bsz = st.shape[0]
    s6 = st.reshape(bsz, N_PAIRS, 2, R_HD, 2, R_HD)
    s = jnp.stack([s6[:, :, 0, :, 0, :], s6[:, :, 1, :, 1, :]], axis=2)
    return jnp.swapaxes(s.reshape(bsz, R_HEADS, R_HD, R_HD), -1, -2)


SHORT_ROWS = 64


def _rwkv_short_kernel(r_ref, kp_ref, v_ref, kk_ref, b_ref, lw_ref, s_ref, o_ref, sn_ref, *, t_len):
    rows = SHORT_ROWS
    ng = rows // t_len
    r2 = 2 * rows
    assert 2 * t_len == 8 and r2 == LANES
    lane = lax.broadcasted_iota(jnp.int32, (1, LANES), 1)
    m_lo = (lane < R_HD).astype(F32)
    m_hi = 1.0 - m_lo
    lane_blk = lane // t_len
    row_c = lax.broadcasted_iota(jnp.int32, (rows, 4 * rows), 0)
    col_c = lax.broadcasted_iota(jnp.int32, (rows, 4 * rows), 1)
    j_c = col_c % rows
    same_c = ((row_c // t_len) == (j_c // t_len)) & (col_c < 3 * rows)
    cum_lhs = jnp.concatenate([(same_c & (row_c >= j_c)).astype(BF16), same_c.astype(BF16)], axis=0)
    row2 = lax.broadcasted_iota(jnp.int32, (r2, r2), 0)
    col2 = lax.broadcasted_iota(jnp.int32, (r2, r2), 1)
    blk = (row2 // t_len) == (col2 // t_len)
    strict = blk & (row2 > col2)
    incl = blk & (row2 >= col2)
    first_seq = lax.broadcasted_iota(jnp.int32, (8, LANES), 0) < t_len
    zero_half = jnp.zeros((R_HD, R_HD), F32)

    def stack(x):
        return jnp.concatenate([x * m_lo, x * m_hi], axis=0)

    def half(x):
        return jnp.concatenate([x * m_lo, pltpu.roll(x, R_HD, 1) * m_lo], axis=0)

    def mm(a, b):
        return jnp.dot(a.astype(BF16), b.astype(BF16), preferred_element_type=F32)

    def nt(a, b):
        return lax.dot_general(a.astype(BF16), b.astype(BF16), (((1,), (1,)), ((), ())),
                               preferred_element_type=F32)

    each = lambda f, *xs: [f(*a) for a in zip(*xs)]

    lw_all = lw_ref[...]
    hi = lw_all.astype(BF16)
    r1 = lw_all - hi.astype(F32)
    mid = r1.astype(BF16)
    lo = (r1 - mid.astype(F32)).astype(BF16)
    g_both = jnp.dot(cum_lhs, jnp.concatenate([hi, mid, lo, jnp.zeros_like(lo)], axis=0),
                     preferred_element_type=F32)

    pairs = list(range(N_PAIRS))
    sls = [slice(p * LANES, (p + 1) * LANES) for p in pairs]
    load = lambda ref: [ref[:, sl] for sl in sls]
    r, kp, v, kk, b, lw = (load(ref) for ref in (r_ref, kp_ref, v_ref, kk_ref, b_ref, lw_ref))
    g = [g_both[0:rows, sl] for sl in sls]
    g_tot = [g_both[rows:r2, sl] for sl in sls]
    e_g = each(jnp.exp, g)
    e_ng = each(lambda x: jnp.exp(-x), g)
    e_gm = each(lambda x, y: jnp.exp(x - y), g, lw)
    e_cg = each(lambda x, y: jnp.exp(y - x), g, g_tot)
    ad = each(lambda x, y: -(x * y), kk, e_gm)
    rd = each(lambda x, y: x * y, r, e_g)
    bd = each(lambda x, y: x * y, b, e_ng)
    kd = each(lambda x, y: x * y, kp, e_ng)
    be_h = each(lambda x, y: half(x * y), b, e_cg)
    ke_h = each(lambda x, y: half(x * y), kp, e_cg)
    v_h = each(half, v)
    ad_h = each(half, ad)
    rd_h = each(half, rd)
    dec_h = each(lambda x: half(jnp.exp(x)), g_tot)

    z = each(lambda a1, a2, b1, b2: nt(jnp.concatenate([stack(a1), stack(a2)], axis=0),
                                       jnp.concatenate([stack(b1), stack(b2)], axis=0)), ad, rd, bd, kd)
    n_b = each(lambda x: jnp.where(strict, x[0:r2, 0:r2], 0.0), z)
    n_k = each(lambda x: jnp.where(strict, x[0:r2, r2:2 * r2], 0.0), z)
    m_b = each(lambda x: jnp.where(incl, x[r2:2 * r2, 0:r2], 0.0), z)
    m_k = each(lambda x: jnp.where(incl, x[r2:2 * r2, r2:2 * r2], 0.0), z)

    y0, o0 = [], []
    for p in pairs:
        y_tiles, o_tiles = [], []
        for hh in range(2):
            for i in range(ng // 2):
                rs = slice(hh * rows + 8 * i, hh * rows + 8 * i + 8)
                lhs = jnp.concatenate([ad_h[p][rs], rd_h[p][rs]], axis=0)[:, 0:R_HD]
                res = [nt(lhs, jnp.concatenate([s_ref[2 * i + w, 2 * p + hh], zero_half], axis=0))
                       for w in range(2)]
                y_tiles.append(jnp.where(first_seq, res[0][0:8], res[1][0:8]))
                o_tiles.append(jnp.where(first_seq, res[0][8:16], res[1][8:16]))
        y0.append(jnp.concatenate(y_tiles, axis=0))
        o0.append(jnp.concatenate(o_tiles, axis=0))

    y = each(lambda a, nk, vh: a + mm(nk, vh), y0, n_k, v_h)
    q = n_b
    steps = int(math.log2(t_len))
    for i in range(steps):
        if i < steps - 1:
            res = each(lambda x, yy: mm(x, jnp.concatenate([x, yy], axis=1)), q, y)
            q = each(lambda x: x[:, 0:r2], res)
            y = each(lambda yy, x: yy + x[:, r2:2 * r2], y, res)
        else:
            y = each(lambda yy, x: yy + mm(x, yy), y, q)
    u = y

    o_h = each(lambda a, mb, uu, mk, vh: a + mm(mb, uu) + mm(mk, vh), o0, m_b, u, m_k, v_h)
    for sl, o in zip(sls, o_h):
        o_ref[:, sl] = o[0:rows] + pltpu.roll(o[rows:r2], R_HD, 1)

    for p in pairs:
        ut = jnp.transpose(u[p])[0:R_HD]
        vt = jnp.transpose(v_h[p])[0:R_HD]
        rhs = jnp.concatenate([be_h[p], ke_h[p]], axis=0)
        keys = [(hh, s) for hh in range(2) for s in range(ng)]
        sel = [(lane_blk == hh * ng + s).astype(F32) for hh, s in keys]
        lhs = jnp.concatenate([jnp.concatenate([ut * m, vt * m], axis=1) for m in sel], axis=0)
        upd = mm(lhs, rhs)
        for idx, (hh, s) in enumerate(keys):
            row0 = hh * rows + s * t_len
            dec = dec_h[p][row0:row0 + 1, 0:R_HD]
            sn_ref[s, 2 * p + hh] = (s_ref[s, 2 * p + hh] * dec
                                     + upd[idx * R_HD:(idx + 1) * R_HD, 0:R_HD])


def _rwkv_short(r, kp, v, kk, b, lw, state, t_len):
    m = r.shape[0]
    ng = SHORT_ROWS // t_len
    seq_spec = pl.BlockSpec((SHORT_ROWS, R_WIDTH), lambda i: (i, 0))
    st_spec = pl.BlockSpec((ng, R_HEADS, R_HD, R_HD), lambda i: (i, 0, 0, 0))
    return pl.pallas_call(
        functools.partial(_rwkv_short_kernel, t_len=t_len),
        grid=(m // SHORT_ROWS,),
        in_specs=[seq_spec] * 6 + [st_spec],
        out_specs=[seq_spec, st_spec],
        out_shape=[jax.ShapeDtypeStruct((m, R_WIDTH), F32), jax.ShapeDtypeStruct(state.shape, F32)],
        compiler_params=_params("parallel"),
        name="rwkv_short",
    )(r, kp, v, kk, b, lw, state)


ATTN_ROWS = SPAN * max(d for _, d in A_GROUPS)
ATTN_BATCH = 8


def _attn_prompt_kernel(*refs):
    ins = refs[:5 * N_GROUPS]
    o_ref = refs[5 * N_GROUPS]
    o_sc = refs[5 * N_GROUPS + 1:5 * N_GROUPS + 1 + N_GROUPS]
    l_sc = refs[5 * N_GROUPS + 1 + N_GROUPS:]
    n = pl.program_id(1)
    i = lax.broadcasted_iota(jnp.int32, (SPAN, 2 * SPAN), 0)
    j = lax.broadcasted_iota(jnp.int32, (SPAN, 2 * SPAN), 1)
    dist = SPAN + i - j
    band = (dist >= 0) & (dist <= SPAN)
    band_first = band & ((n > 0) | (j >= SPAN))
    scale = 1.0 / math.sqrt(A_HD)
    cat = lambda a, b: jnp.concatenate([a, b], axis=0).astype(BF16)

    for g, (_, dil) in enumerate(A_GROUPS):
        q_ref, kp_ref, kc_ref, vp_ref, vc_ref = ins[5 * g:5 * g + 5]

        def rows(run, rho):
            start = run * SPAN * dil + rho
            return pl.ds(start, SPAN, stride=dil) if dil > 1 else pl.ds(start, SPAN)

        all_probs = [(run, rho) for run in range(ATTN_ROWS // (SPAN * dil)) for rho in range(dil)]
        for b0 in range(0, len(all_probs), ATTN_BATCH):
            probs = all_probs[b0:b0 + ATTN_BATCH]
            q = [q_ref[rows(run, rho), :].astype(BF16) for run, rho in probs]
            k = [cat(kp_ref[rows(0, rho), :] if run == 0 else kc_ref[rows(run - 1, rho), :],
                     kc_ref[rows(run, rho), :]) for run, rho in probs]
            v = [cat(vp_ref[rows(0, rho), :] if run == 0 else vc_ref[rows(run - 1, rho), :],
                     vc_ref[rows(run, rho), :]) for run, rho in probs]
            s = [lax.dot_general(a, b, (((1,), (1,)), ((), ())), preferred_element_type=F32) * scale
                 for a, b in zip(q, k)]
            s = [jnp.where(band_first if run == 0 else band, x, NEG) for x, (run, _) in zip(s, probs)]
            m = [jnp.max(x, axis=-1, keepdims=True) for x in s]
            p = [jnp.exp(x - y) for x, y in zip(s, m)]
            l = [jnp.sum(x, axis=-1, keepdims=True) for x in p]
            o = [jnp.dot(x.astype(BF16), y, preferred_element_type=F32) for x, y in zip(p, v)]
            for (run, rho), oo, ll, mx in zip(probs, o, l, m):
                o_sc[g][rows(run, rho), :] = oo / ll
                l_sc[g][rows(run, rho), :] = jnp.broadcast_to(mx + jnp.log(ll), (SPAN, A_HD))

    lse = [ref[...] for ref in l_sc]
    top = functools.reduce(jnp.maximum, lse)
    e = [jnp.exp(x - top) for x in lse]
    den = functools.reduce(jnp.add, e)
    o_ref[...] = functools.reduce(jnp.add, [(w / den) * ref[...] for w, ref in zip(e, o_sc)])


def _attn_prompt(h, bsz, t):
    nblk = t // ATTN_ROWS
    in_specs = []
    for g, (_, dil) in enumerate(A_GROUPS):
        prev_rows = SPAN * dil
        per_blk = ATTN_ROWS // prev_rows
        col = lambda which, hd, g=g: COL_QKV // A_HD + (which * N_GROUPS + g) * A_HEADS + hd
        cur = lambda which, col=col: pl.BlockSpec(
            (ATTN_ROWS, A_HD), lambda b, n, hd: (b * nblk + n, col(which, hd)))
        prev = lambda which, col=col, per_blk=per_blk, prev_rows=prev_rows: pl.BlockSpec(
            (prev_rows, A_HD),
            lambda b, n, hd: (jnp.maximum((b * nblk + n) * per_blk - 1, 0), col(which, hd)))
        in_specs += [cur(0), prev(1), cur(1), prev(2), cur(2)]
    return pl.pallas_call(
        _attn_prompt_kernel,
        grid=(bsz, nblk, A_HEADS),
        in_specs=in_specs,
        out_specs=pl.BlockSpec((ATTN_ROWS, A_HD), lambda b, n, hd: (b * nblk + n, hd)),
        out_shape=jax.ShapeDtypeStruct((bsz * t, A_WIDTH), F32),
        scratch_shapes=[pltpu.VMEM((ATTN_ROWS, A_HD), F32)] * (2 * N_GROUPS),
        compiler_params=_params("parallel", "arbitrary", "arbitrary"),
        name="attn_prompt",
    )(*([h] * (5 * N_GROUPS)))


def _attn_sample_kernel(q_ref, kvn_ref, c1_ref, c2_ref, c3_ref, o_ref, *, t_new):
    scale = 1.0 / math.sqrt(A_HD)
    kvh_n = 2 * A_HEADS
    n_q = t_new * A_HEADS
    n_c = SPAN * kvh_n
    n_l = n_c + LANES
    row = lax.broadcasted_iota(jnp.int32, (n_q, n_l), 0)
    lane = lax.broadcasted_iota(jnp.int32, (n_q, n_l), 1)
    t_r, h_r = row // A_HEADS, row % A_HEADS
    in_buf = lane < n_c
    j_l = lane // kvh_n
    t_l = (lane - n_c) // kvh_n
    head_ok = (lane % kvh_n) == h_r
    pad = jnp.zeros((LANES - t_new * kvh_n, A_HD), F32)
    outs, lses = [], []
    for g, cref in enumerate((c1_ref, c2_ref, c3_ref)):
        q = q_ref[0, g].astype(BF16)
        kv_new = jnp.concatenate([kvn_ref[0, g], pad], axis=0)
        slabs = [None] if g == 0 else list(range(t_new))
        kv2, s = [], []
        for t in slabs:
            buf = cref[0, :, 0 if t is None else t].reshape(n_c, A_HD)
            kv2.append(jnp.concatenate([buf, kv_new], axis=0).astype(BF16))
            if t is None:
                valid = head_ok & ((in_buf & (j_l >= t_r)) | (~in_buf & (t_l <= t_r)))
            else:
                valid = head_ok & (t_r == t) & (in_buf | (t_l == t))
            st = lax.dot_general(q, kv2[-1], (((1,), (1,)), ((), ())), preferred_element_type=F32)
            s.append(jnp.where(valid, st * scale, NEG))
        m = functools.reduce(jnp.maximum, [jnp.max(x, axis=-1, keepdims=True) for x in s])
        p = [jnp.exp(x - m) for x in s]
        l = functools.reduce(jnp.add, [jnp.sum(x, axis=-1, keepdims=True) for x in p])
        o = functools.reduce(jnp.add, [
            jnp.dot(pltpu.roll(x, A_HEADS, 1).astype(BF16), y, preferred_element_type=F32)
            for x, y in zip(p, kv2)])
        outs.append(o / l)
        lses.append(m + jnp.log(l))
    top = functools.reduce(jnp.maximum, lses)
    e = [jnp.exp(x - top) for x in lses]
    den = functools.reduce(jnp.add, e)
    o_ref[0] = functools.reduce(jnp.add, [(w / den) * x for w, x in zip(e, outs)])


def _attn_sample(qkv, caches, bsz, t_new):
    kvh_n = 2 * A_HEADS
    x6 = qkv.reshape(bsz, t_new, 3, N_GROUPS, A_HEADS, A_HD)
    q_in = x6[:, :, 0].transpose(0, 2, 1, 3, 4).reshape(bsz, N_GROUPS, t_new * A_HEADS, A_HD)
    kv_new = x6[:, :, 1:3].transpose(0, 3, 1, 2, 4, 5).reshape(bsz, N_GROUPS, t_new * kvh_n, A_HD)
    views = []
    specs = []
    for (win, dil), cache in zip(A_GROUPS, caches):
        assert cache.shape[1] == win and win // dil == SPAN and (dil == 1 or t_new <= dil)
        views.append(cache.reshape(bsz, SPAN, dil, kvh_n, A_HD))
        specs.append(pl.BlockSpec((1, SPAN, min(dil, t_new), kvh_n, A_HD), lambda b: (b, 0, 0, 0, 0)))
    q_spec = pl.BlockSpec((1, N_GROUPS, t_new * A_HEADS, A_HD), lambda b: (b, 0, 0, 0))
    kvn_spec = pl.BlockSpec((1, N_GROUPS, t_new * kvh_n, A_HD), lambda b: (b, 0, 0, 0))
    o = pl.pallas_call(
        functools.partial(_attn_sample_kernel, t_new=t_new),
        grid=(bsz,),
        in_specs=[q_spec, kvn_spec] + specs,
        out_specs=pl.BlockSpec((1, t_new * A_HEADS, A_HD), lambda b: (b, 0, 0)),
        out_shape=jax.ShapeDtypeStruct((bsz, t_new * A_HEADS, A_HD), F32),
        compiler_params=_params("parallel"),
        name="attn_sample",
    )(q_in, kv_new, *views)
    return o.reshape(bsz * t_new, A_WIDTH)


def _final_kernel(x_ref, o_ref, bon_ref, zr_ref, oa_ref, zgg_ref, seg_ref, lnxg_ref, lnxb_ref, bg_ref,
                  woa_ref, wob_ref, wout_ref, lng_ref, lnb_ref, y_ref):
    seg2 = seg_ref[...]
    o = o_ref[...]
    mu = _seg_sum(o, seg2) * (1.0 / R_HD)
    d = o - mu
    var = _seg_sum(d * d, seg2) * (1.0 / R_HD)
    on = d * lax.rsqrt(var + GN_EPS) * lnxg_ref[...] + lnxb_ref[...]
    zr = zr_ref[...]
    y_r = (on + bon_ref[...]) * (zr * jax.nn.sigmoid(zr))

    z_a = zgg_ref[:, 0:A_WIDTH]
    y_a = oa_ref[...] * (z_a * jax.nn.sigmoid(z_a))

    g_r = zgg_ref[:, A_WIDTH:A_WIDTH + D_MODEL]
    g_a = zgg_ref[:, A_WIDTH + D_MODEL:A_WIDTH + 2 * D_MODEL]
    gate_r = jax.nn.sigmoid(g_r + bg_ref[:, 0:D_MODEL])
    gate_a = jax.nn.sigmoid(g_a + bg_ref[:, D_MODEL:2 * D_MODEL])
    mix = (gate_r * jnp.dot(y_r.astype(BF16), woa_ref[...], preferred_element_type=F32)
           + gate_a * jnp.dot(y_a.astype(BF16), wob_ref[...], preferred_element_type=F32))
    yy = ALPHA * x_ref[...] + jnp.dot(mix.astype(BF16), wout_ref[...], preferred_element_type=F32)
    mean = jnp.mean(yy, axis=-1, keepdims=True)
    cen = yy - mean
    variance = jnp.mean(cen * cen, axis=-1, keepdims=True)
    y_ref[...] = cen * lax.rsqrt(variance + LN_EPS) * lng_ref[...] + lnb_ref[...]


def _final(x, o, bonus, h, o_attn, weights, tm):
    m = x.shape[0]
    row = lambda w, j=0: pl.BlockSpec((tm, w), lambda i: (i, j))
    const = lambda a: pl.BlockSpec(a.shape, lambda i: (0, 0))
    in_specs = ([row(D_MODEL), row(R_WIDTH), row(R_WIDTH), row(R_WIDTH, COL_ZR // R_WIDTH), row(A_WIDTH),
                 row(ZGG_W, COL_ZGG // ZGG_W)] + [const(a) for a in weights])
    return pl.pallas_call(
        _final_kernel,
        grid=(m // tm,),
        in_specs=in_specs,
        out_specs=row(D_MODEL),
        out_shape=jax.ShapeDtypeStruct((m, D_MODEL), F32),
        compiler_params=_params("parallel"),
        name="merge_out_ln",
    )(x, o, bonus, h, o_attn, h, *weights)


def _layer(x2, bsz, t, shift_prev, wkv0, caches, wts):
    m = x2.shape[0]
    h = _in_proj(x2, wts["w_proj"], 2048 if m % 2048 == 0 else m)

    tm = 512 if m % 512 == 0 else m
    if caches is None:
        first = None
    else:
        first = jnp.zeros((bsz, t, SHIFT_COLS), F32).at[:, 0].set(shift_prev).reshape(m, SHIFT_COLS)
    r, kp, v, kk, b, lw, bonus = _rwkv_prep(h, first, t, wts["prep"], tm)

    if t % CHUNK == 0 and wkv0 is None:
        seq = lambda a: a.reshape(bsz, t, R_WIDTH)
        s0 = jnp.zeros((bsz, N_PAIRS, LANES, LANES), F32)
        o_seq, st = _rwkv_recurrence(seq(r), seq(kp), seq(v), seq(kk), seq(b), seq(lw), s0, 2)
        o_rwkv = o_seq.reshape(m, R_WIDTH)
        wkv_new = _pairs_to_state(st)
    else:
        o_rwkv, wkv_new = _rwkv_short(r, kp, v, kk, b, lw, wkv0, t)

    def rows_cols(r0, c0, width):
        if t % 8 == 0:
            return h.reshape(bsz, t, N_PROJ)[:, r0:, c0:c0 + width]
        return h[:, c0:c0 + width].reshape(bsz, t, width)[:, r0:]

    if caches is None:
        o_attn = _attn_prompt(h, bsz, t)
    else:
        o_attn = _attn_sample(h[:, COL_QKV:], caches, bsz, t)

    tmf = 256 if m % 256 == 0 else m
    y = _final(x2, o_rwkv, bonus, h, o_attn, wts["final"], tmf)

    new_kv = []
    for g, (win, _) in enumerate(A_GROUPS):
        keep = min(win, t) if caches is None else t
        def part(which):
            c0 = COL_QKV + (which * N_GROUPS + g) * A_WIDTH
            return rows_cols(t - keep, c0, A_WIDTH).reshape(bsz, keep, A_HEADS, A_HD)
        new_kv.append(jnp.stack([part(1), part(2)], axis=2))
    shift_new = rows_cols(t - 1, COL_ZS, SHIFT_COLS)[:, 0]
    return y.reshape(bsz, t, D_MODEL), new_kv, wkv_new, shift_new


def kernel(x_prompt, x_sample, cache_kv_g1, cache_kv_g2, cache_kv_g3, state_rwkv_wkv, state_rwkv_shift,
           w_in, b_gate, mu_shift, w0, w_w2, a0, w_a2, k_k, k_a, r_k, lnx_g, lnx_b,
           w_oa, w_ob, w_out, ln_g, ln_b):
    assert w_in.shape[0] == DEPTH
    bp, tp, _ = x_prompt.shape
    bs, ts, _ = x_sample.shape
    xp = x_prompt.reshape(bp * tp, D_MODEL)
    xs = x_sample.reshape(bs * ts, D_MODEL)
    head = lax.broadcasted_iota(jnp.int32, (LANES, LANES), 0) // R_HD
    seg = (head == head.T).astype(BF16)
    seg2 = jnp.concatenate([seg, seg], axis=0)
    row = lambda a: a.reshape(1, -1)
    acc = [[] for _ in range(12)]
    for l in range(DEPTH):
        wb = w_in[l].astype(BF16)
        c0 = SHIFT_COLS
        c1 = c0 + R_WIDTH
        c2 = c1 + 3 * A_QKV_W
        w_proj = jnp.concatenate([wb[:, :c0], jnp.zeros((D_MODEL, COL_ZR - c0), BF16), wb[:, c0:c1],
                                  wb[:, c2:], wb[:, c1:c2]], axis=1)
        zero = jnp.zeros((LORA, R_WIDTH), F32)
        w_lora = jnp.concatenate([jnp.concatenate([w_w2[l], zero], axis=1),
                                  jnp.concatenate([zero, w_a2[l]], axis=1)], axis=0)
        wl_hi = w_lora.astype(BF16)
        wl_lo = (w_lora - wl_hi.astype(F32)).astype(BF16)
        wts = {
            "w_proj": w_proj,
            "prep": (row(mu_shift[l]), row(w0[l]), row(a0[l]), jnp.concatenate([wl_hi, wl_hi], axis=0),
                     wl_lo, row(k_k[l]), row(k_a[l]), row(r_k[l]), seg2),
            "final": (seg2, row(lnx_g[l]), row(lnx_b[l]), row(b_gate[l]), w_oa[l].astype(BF16),
                      w_ob[l].astype(BF16), w_out[l].astype(BF16), row(ln_g[l]), row(ln_b[l])),
        }
        yp, kv_p, s_p, sh_p = _layer(xp, bp, tp, None, None, None, wts)
        ys, kv_s, s_s, sh_s = _layer(xs, bs, ts, state_rwkv_shift[l], state_rwkv_wkv[l],
                                     (cache_kv_g1[l], cache_kv_g2[l], cache_kv_g3[l]), wts)
        xp = yp.reshape(bp * tp, D_MODEL)
        xs = ys.reshape(bs * ts, D_MODEL)
        for idx, val in zip(range(2, 12), (kv_p[0], kv_s[0], kv_p[1], kv_s[1], kv_p[2], kv_s[2],
                                           s_p, s_s, sh_p, sh_s)):
            acc[idx].append(val)
    outs = [xp.reshape(bp, tp, D_MODEL), xs.reshape(bs, ts, D_MODEL)]
    outs += [jnp.stack(a) for a in acc[2:]]
    return tuple(outs)
```

```python
import functools
import math

import jax
import jax.numpy as jnp
from jax import lax
from jax.experimental import pallas as pl
from jax.experimental.pallas import tpu as pltpu

F32 = jnp.float32
BF16 = jnp.bfloat16

D_MODEL = 1024
R_HEADS = 8
R_HD = 64
R_WIDTH = R_HEADS * R_HD
LORA = 64
SHIFT_COLS = 3 * R_WIDTH + 2 * LORA
GN_EPS = 64e-5
A_GROUPS = ((128, 1), (512, 4), (2048, 16))
N_GROUPS = 3
A_HEADS = 4
A_HD = 128
A_WIDTH = A_HEADS * A_HD
A_QKV_W = N_GROUPS * A_WIDTH
SPAN = 128
DEPTH = 1
ALPHA = (2 * DEPTH) ** 0.25
LN_EPS = 1e-5
NEG = -1e30

LANES = 128
CHUNK = 64
N_PAIRS = R_HEADS // 2
VMEM_LIMIT = 48 * 1024 * 1024

PROJ_TN = 512
COL_ZS = 0
COL_ZR = 2048
COL_ZGG = COL_ZR + R_WIDTH
ZGG_W = A_WIDTH + 2 * D_MODEL
COL_QKV = COL_ZGG + ZGG_W
N_PROJ = COL_QKV + 3 * A_QKV_W
assert COL_ZR % R_WIDTH == 0 and COL_ZGG % ZGG_W == 0 and COL_QKV % LANES == 0 and N_PROJ % PROJ_TN == 0


def _params(*sem):
    return pltpu.CompilerParams(dimension_semantics=sem, vmem_limit_bytes=VMEM_LIMIT)


def _split2(x):
    hi = x.astype(BF16)
    return hi, (x - hi.astype(F32)).astype(BF16)


def _seg_sum(x, seg2):
    cols = []
    for c in range(x.shape[1] // LANES):
        hi, lo = _split2(x[:, c * LANES:(c + 1) * LANES])
        cols.append(jnp.dot(jnp.concatenate([hi, lo], axis=1), seg2, preferred_element_type=F32))
    return jnp.concatenate(cols, axis=1)


def _mm_kernel(x_ref, w_ref, o_ref):
    o_ref[...] = jnp.dot(x_ref[...].astype(BF16), w_ref[...], preferred_element_type=F32)


def _in_proj(x, w, tm):
    m, k = x.shape
    n = w.shape[1]
    return pl.pallas_call(
        _mm_kernel,
        grid=(m // tm, n // PROJ_TN),
        in_specs=[pl.BlockSpec((tm, k), lambda i, j: (i, 0)),
                  pl.BlockSpec((k, PROJ_TN), lambda i, j: (0, j))],
        out_specs=pl.BlockSpec((tm, PROJ_TN), lambda i, j: (i, j)),
        out_shape=jax.ShapeDtypeStruct((m, n), F32),
        compiler_params=_params("parallel", "arbitrary"),
        name="in_proj",
    )(x, w)


def _prep_math(zs, prev, mu, w0, a0, wl_a, wl_b, k_k, k_a, r_k, seg2, outs):
    r_o, kp_o, v_o, kk_o, b_o, lw_o, bon_o = outs
    zsm = zs + mu * (prev - zs)
    r = zsm[:, 0:R_WIDTH]
    k = zsm[:, R_WIDTH:2 * R_WIDTH]
    v = zsm[:, 2 * R_WIDTH:3 * R_WIDTH]
    slab = zsm[:, 3 * R_WIDTH:SHIFT_COLS]
    lane = lax.broadcasted_iota(jnp.int32, slab.shape, 1)
    slab = jnp.where(lane < LORA, jnp.tanh(slab), slab)
    s_hi, s_lo = _split2(slab)
    lin = (jnp.dot(jnp.concatenate([s_hi, s_lo], axis=1), wl_a, preferred_element_type=F32)
           + jnp.dot(s_hi, wl_b, preferred_element_type=F32))
    wlin = w0 + lin[:, 0:R_WIDTH]
    alin = a0 + lin[:, R_WIDTH:2 * R_WIDTH]
    nw = -wlin
    softplus = jnp.maximum(nw, 0.0) + jnp.log1p(jnp.exp(-jnp.abs(nw)))
    w = -softplus - 0.5
    lw = -jnp.exp(w)
    a = jax.nn.sigmoid(alin)
    kkr = k * k_k
    kk = kkr / jnp.maximum(jnp.sqrt(_seg_sum(kkr * kkr, seg2)), 1e-12)
    kp = k * (1.0 + (a - 1.0) * k_a)
    r_o[...] = r.astype(r_o.dtype)
    kp_o[...] = kp.astype(kp_o.dtype)
    v_o[...] = v.astype(v_o.dtype)
    kk_o[...] = kk.astype(kk_o.dtype)
    b_o[...] = (kk * a).astype(b_o.dtype)
    lw_o[...] = lw
    bon_o[...] = (_seg_sum(r * kp * r_k, seg2) * v).astype(bon_o.dtype)


def _prep_prompt_kernel(zs_ref, p8_ref, mu_ref, w0_ref, a0_ref, wla_ref, wlb_ref, kk_ref, ka_ref,
                        rk_ref, seg_ref, *outs, tiles_per_seq):
    i = pl.program_id(0)
    zs = zs_ref[...]
    rolled = pltpu.roll(zs, 1, 0)
    carry = jnp.where(i % tiles_per_seq == 0, 0.0, p8_ref[7:8, :])
    row = lax.broadcasted_iota(jnp.int32, zs.shape, 0)
    prev = jnp.where(row == 0, carry, rolled)
    _prep_math(zs, prev, mu_ref[...], w0_ref[...], a0_ref[...], wla_ref[...], wlb_ref[...],
               kk_ref[...], ka_ref[...], rk_ref[...], seg_ref[...], outs)


def _prep_sample_kernel(zs_ref, first_ref, mu_ref, w0_ref, a0_ref, wla_ref, wlb_ref, kk_ref, ka_ref,
                        rk_ref, seg_ref, *outs, seq_len):
    zs = zs_ref[...]
    rolled = pltpu.roll(zs, 1, 0)
    row = lax.broadcasted_iota(jnp.int32, zs.shape, 0)
    prev = jnp.where(row % seq_len == 0, first_ref[...], rolled)
    _prep_math(zs, prev, mu_ref[...], w0_ref[...], a0_ref[...], wla_ref[...], wlb_ref[...],
               kk_ref[...], ka_ref[...], rk_ref[...], seg_ref[...], outs)


def _rwkv_prep(h, first, seq_len, weights, tm):
    m = h.shape[0]
    const = lambda a: pl.BlockSpec(a.shape, lambda i: (0, 0))
    row_spec = pl.BlockSpec((tm, SHIFT_COLS), lambda i: (i, COL_ZS))
    if first is None:
        kern = functools.partial(_prep_prompt_kernel, tiles_per_seq=seq_len // tm)
        second = h
        second_spec = pl.BlockSpec((8, SHIFT_COLS), lambda i: (jnp.maximum(i * (tm // 8) - 1, 0), COL_ZS))
    else:
        kern = functools.partial(_prep_sample_kernel, seq_len=seq_len)
        second = first
        second_spec = pl.BlockSpec((tm, SHIFT_COLS), lambda i: (i, 0))
    out_spec = pl.BlockSpec((tm, R_WIDTH), lambda i: (i, 0))
    return pl.pallas_call(
        kern,
        grid=(m // tm,),
        in_specs=[row_spec, second_spec] + [const(a) for a in weights],
        out_specs=[out_spec] * 7,
        out_shape=[jax.ShapeDtypeStruct((m, R_WIDTH), dt) for dt in (BF16,) * 5 + (F32, BF16)],
        compiler_params=_params("parallel"),
        name="rwkv_prep",
    )(h, second, *weights)


def _rwkv_chunk_kernel(r_ref, kp_ref, v_ref, kk_ref, b_ref, lw_ref, s0_ref, o_ref, st_ref, *, nb):
    n = pl.program_id(1)

    @pl.when(n == 0)
    def _():
        st_ref[...] = s0_ref[...]

    c = CHUNK
    c2 = 2 * c
    lane = lax.broadcasted_iota(jnp.int32, (1, LANES), 1)
    m_lo = (lane < R_HD).astype(F32)
    m_hi = 1.0 - m_lo
    row_c = lax.broadcasted_iota(jnp.int32, (c, 4 * c), 0)
    col_c = lax.broadcasted_iota(jnp.int32, (c, 4 * c), 1)
    tri3 = ((row_c >= col_c % c) & (col_c < 3 * c)).astype(BF16)
    row2 = lax.broadcasted_iota(jnp.int32, (c2, c2), 0)
    col2 = lax.broadcasted_iota(jnp.int32, (c2, c2), 1)
    strict = row2 > col2
    incl = row2 >= col2

    def stack(x):
        return jnp.concatenate([x * m_lo, x * m_hi], axis=0)

    def mm(a, b):
        return jnp.dot(a.astype(BF16), b.astype(BF16), preferred_element_type=F32)

    def nt(a, b):
        return lax.dot_general(a.astype(BF16), b.astype(BF16), (((1,), (1,)), ((), ())),
                               preferred_element_type=F32)

    each = lambda f, *xs: [f(*a) for a in zip(*xs)]

    def cumsum(lw_all):
        hi = lw_all.astype(BF16)
        r1 = lw_all - hi.astype(F32)
        mid = r1.astype(BF16)
        lo = (r1 - mid.astype(F32)).astype(BF16)
        return jnp.dot(tri3, jnp.concatenate([hi, mid, lo, jnp.zeros_like(lo)], axis=0),
                       preferred_element_type=F32)

    g_all = [cumsum(lw_ref[bi]) for bi in range(nb)]

    chains = [(bi, slice(p * LANES, (p + 1) * LANES), p) for bi in range(nb) for p in range(N_PAIRS)]
    load = lambda ref: [ref[bi, :, sl].astype(F32) for bi, sl, _ in chains]
    r, kp, v, kk, b, lw = (load(ref) for ref in (r_ref, kp_ref, v_ref, kk_ref, b_ref, lw_ref))
    st = [st_ref[bi, p] for bi, _, p in chains]
    g = [g_all[bi][:, sl] for bi, sl, _ in chains]
    g_last = each(lambda x: x[c - 1:c, :], g)
    e_g = each(jnp.exp, g)
    e_ng = each(lambda x: jnp.exp(-x), g)
    e_gm = each(lambda x, y: jnp.exp(x - y), g, lw)
    e_cg = each(lambda x, y: jnp.exp(y - x), g, g_last)
    ad_s = each(lambda x, y: stack(-(x * y)), kk, e_gm)
    rd_s = each(lambda x, y: stack(x * y), r, e_g)
    bd_s = each(lambda x, y: stack(x * y), b, e_ng)
    kd_s = each(lambda x, y: stack(x * y), kp, e_ng)
    be_s = each(lambda x, y: stack(x * y), b, e_cg)
    ke_s = each(lambda x, y: stack(x * y), kp, e_cg)
    v_s = each(stack, v)

    z = each(lambda a1, a2, b1, b2: nt(jnp.concatenate([a1, a2], axis=0),
                                       jnp.concatenate([b1, b2], axis=0)), ad_s, rd_s, bd_s, kd_s)
    n_b = each(lambda x: jnp.where(strict, x[0:c2, 0:c2], 0.0), z)
    n_k = each(lambda x: jnp.where(strict, x[0:c2, c2:2 * c2], 0.0), z)
    m_b = each(lambda x: jnp.where(incl, x[c2:2 * c2, 0:c2], 0.0), z)
    m_k = each(lambda x: jnp.where(incl, x[c2:2 * c2, c2:2 * c2], 0.0), z)

    yo = each(lambda a1, nk, rd, mk, s, vs: mm(
        jnp.concatenate([jnp.concatenate([a1, nk], axis=1), jnp.concatenate([rd, mk], axis=1)], axis=0),
        jnp.concatenate([s, vs], axis=0)), ad_s, n_k, rd_s, m_k, st, v_s)
    y = each(lambda x: x[0:c2], yo)
    o_part = each(lambda x: x[c2:2 * c2], yo)

    q = n_b
    steps = int(math.log2(c))
    for i in range(steps):
        if i < steps - 1:
            res = each(lambda x, yy: mm(x, jnp.concatenate([x, yy], axis=1)), q, y)
            q = each(lambda x: x[:, 0:c2], res)
            y = each(lambda yy, x: yy + x[:, c2:2 * c2], y, res)
        else:
            y = each(lambda yy, x: yy + mm(x, yy), y, q)
    u = y

    o_bd = each(lambda op, mb, uu: op + mm(mb, uu), o_part, m_b, u)
    for (bi, sl, _), o in zip(chains, o_bd):
        o_ref[bi, :, sl] = o[0:c] + o[c:c2]

    pc = each(lambda x: jnp.transpose(jnp.broadcast_to(jnp.exp(x), (LANES, LANES))), g_last)
    upd = each(lambda be, ke, uu, vs: mm(jnp.concatenate([jnp.transpose(be), jnp.transpose(ke)], axis=1),
                                         jnp.concatenate([uu, vs], axis=0)), be_s, ke_s, u, v_s)
    for (bi, _, p), dec, s, d in zip(chains, pc, st, upd):
        st_ref[bi, p] = dec * s + d


def _rwkv_recurrence(r, kp, v, kk, b, lw, s0, nb):
    bsz, t, _ = r.shape
    seq_spec = pl.BlockSpec((nb, CHUNK, R_WIDTH), lambda i, n: (i, n, 0))
    st_spec = pl.BlockSpec((nb, N_PAIRS, LANES, LANES), lambda i, n: (i, 0, 0, 0))
    return pl.pallas_call(
        functools.partial(_rwkv_chunk_kernel, nb=nb),
        grid=(bsz // nb, t // CHUNK),
        in_specs=[seq_spec] * 6 + [st_spec],
        out_specs=[seq_spec, st_spec],
        out_shape=[jax.ShapeDtypeStruct((bsz, t, R_WIDTH), F32),
                   jax.ShapeDtypeStruct(s0.shape, F32)],
        compiler_params=_params("parallel", "arbitrary"),
        name="rwkv_chunks",
    )(r, kp, v, kk, b, lw, s0)


def _pairs_to_state(st):
    bsz = st.shape[0]
    s6 = st.reshape(bsz, N_PAIRS, 2, R_HD, 2, R_HD)
    s = jnp.stack([s6[:, :, 0, :, 0, :], s6[:, :, 1, :, 1, :]], axis=2)
    return jnp.swapaxes(s.reshape(bsz, R_HEADS, R_HD, R_HD), -1, -2)


SHORT_ROWS = 64


def _rwkv_short_kernel(r_ref, kp_ref, v_ref, kk_ref, b_ref, lw_ref, s_ref, o_ref, sn_ref, *, t_len):
    rows = SHORT_ROWS
    ng = rows // t_len
    r2 = 2 * rows
    assert 2 * t_len == 8 and r2 == LANES
    lane = lax.broadcasted_iota(jnp.int32, (1, LANES), 1)
    m_lo = (lane < R_HD).astype(F32)
    m_hi = 1.0 - m_lo
    lane_blk = lane // t_len
    row_c = lax.broadcasted_iota(jnp.int32, (rows, 4 * rows), 0)
    col_c = lax.broadcasted_iota(jnp.int32, (rows, 4 * rows), 1)
    j_c = col_c % rows
    same_c = ((row_c // t_len) == (j_c // t_len)) & (col_c < 3 * rows)
    cum_lhs = jnp.concatenate([(same_c & (row_c >= j_c)).astype(BF16), same_c.astype(BF16)], axis=0)
    row2 = lax.broadcasted_iota(jnp.int32, (r2, r2), 0)
    col2 = lax.broadcasted_iota(jnp.int32, (r2, r2), 1)
    blk = (row2 // t_len) == (col2 // t_len)
    strict = blk & (row2 > col2)
    incl = blk & (row2 >= col2)
    first_seq = lax.broadcasted_iota(jnp.int32, (8, LANES), 0) < t_len
    zero_half = jnp.zeros((R_HD, R_HD), F32)

    def stack(x):
        return jnp.concatenate([x * m_lo, x * m_hi], axis=0)

    def half(x):
        return jnp.concatenate([x * m_lo, pltpu.roll(x, R_HD, 1) * m_lo], axis=0)

    def mm(a, b):
        return jnp.dot(a.astype(BF16), b.astype(BF16), preferred_element_type=F32)

    def nt(a, b):
        return lax.dot_general(a.astype(BF16), b.astype(BF16), (((1,), (1,)), ((), ())),
                               preferred_element_type=F32)

    each = lambda f, *xs: [f(*a) for a in zip(*xs)]

    lw_all = lw_ref[...]
    hi = lw_all.astype(BF16)
    r1 = lw_all - hi.astype(F32)
    mid = r1.astype(BF16)
    lo = (r1 - mid.astype(F32)).astype(BF16)
    g_both = jnp.dot(cum_lhs, jnp.concatenate([hi, mid, lo, jnp.zeros_like(lo)], axis=0),
                     preferred_element_type=F32)

    pairs = list(range(N_PAIRS))
    sls = [slice(p * LANES, (p + 1) * LANES) for p in pairs]
    load = lambda ref: [ref[:, sl].astype(F32) for sl in sls]
    r, kp, v, kk, b, lw = (load(ref) for ref in (r_ref, kp_ref, v_ref, kk_ref, b_ref, lw_ref))
    g = [g_both[0:rows, sl] for sl in sls]
    g_tot = [g_both[rows:r2, sl] for sl in sls]
    e_g = each(jnp.exp, g)
    e_ng = each(lambda x: jnp.exp(-x), g)
    e_gm = each(lambda x, y: jnp.exp(x - y), g, lw)
    e_cg = each(lambda x, y: jnp.exp(y - x), g, g_tot)
    ad = each(lambda x, y: -(x * y), kk, e_gm)
    rd = each(lambda x, y: x * y, r, e_g)
    bd = each(lambda x, y: x * y, b, e_ng)
    kd = each(lambda x, y: x * y, kp, e_ng)
    be_h = each(lambda x, y: half(x * y), b, e_cg)
    ke_h = each(lambda x, y: half(x * y), kp, e_cg)
    v_h = each(half, v)
    ad_h = each(half, ad)
    rd_h = each(half, rd)
    dec_h = each(lambda x: half(jnp.exp(x)), g_tot)

    z = each(lambda a1, a2, b1, b2: nt(jnp.concatenate([stack(a1), stack(a2)], axis=0),
                                       jnp.concatenate([stack(b1), stack(b2)], axis=0)), ad, rd, bd, kd)
    n_b = each(lambda x: jnp.where(strict, x[0:r2, 0:r2], 0.0), z)
    n_k = each(lambda x: jnp.where(strict, x[0:r2, r2:2 * r2], 0.0), z)
    m_b = each(lambda x: jnp.where(incl, x[r2:2 * r2, 0:r2], 0.0), z)
    m_k = each(lambda x: jnp.where(incl, x[r2:2 * r2, r2:2 * r2], 0.0), z)

    y0, o0 = [], []
    for p in pairs:
        y_tiles, o_tiles = [], []
        for hh in range(2):
            for i in range(ng // 2):
                rs = slice(hh * rows + 8 * i, hh * rows + 8 * i + 8)
                lhs = jnp.concatenate([ad_h[p][rs], rd_h[p][rs]], axis=0)[:, 0:R_HD]
                res = [nt(lhs, jnp.concatenate([s_ref[2 * i + w, 2 * p + hh], zero_half], axis=0))
                       for w in range(2)]
                y_tiles.append(jnp.where(first_seq, res[0][0:8], res[1][0:8]))
                o_tiles.append(jnp.where(first_seq, res[0][8:16], res[1][8:16]))
        y0.append(jnp.concatenate(y_tiles, axis=0))
        o0.append(jnp.concatenate(o_tiles, axis=0))

    y = each(lambda a, nk, vh: a + mm(nk, vh), y0, n_k, v_h)
    q = n_b
    steps = int(math.log2(t_len))
    for i in range(steps):
        if i < steps - 1:
            res = each(lambda x, yy: mm(x, jnp.concatenate([x, yy], axis=1)), q, y)
            q = each(lambda x: x[:, 0:r2], res)
            y = each(lambda yy, x: yy + x[:, r2:2 * r2], y, res)
        else:
            y = each(lambda yy, x: yy + mm(x, yy), y, q)
    u = y

    o_h = each(lambda a, mb, uu, mk, vh: a + mm(mb, uu) + mm(mk, vh), o0, m_b, u, m_k, v_h)
    for sl, o in zip(sls, o_h):
        o_ref[:, sl] = o[0:rows] + pltpu.roll(o[rows:r2], R_HD, 1)

    for p in pairs:
        ut = jnp.transpose(u[p])[0:R_HD]
        vt = jnp.transpose(v_h[p])[0:R_HD]
        rhs = jnp.concatenate([be_h[p], ke_h[p]], axis=0)
        keys = [(hh, s) for hh in range(2) for s in range(ng)]
        sel = [(lane_blk == hh * ng + s).astype(F32) for hh, s in keys]
        lhs = jnp.concatenate([jnp.concatenate([ut * m, vt * m], axis=1) for m in sel], axis=0)
        upd = mm(lhs, rhs)
        for idx, (hh, s) in enumerate(keys):
            row0 = hh * rows + s * t_len
            dec = dec_h[p][row0:row0 + 1, 0:R_HD]
            sn_ref[s, 2 * p + hh] = (s_ref[s, 2 * p + hh] * dec
                                     + upd[idx * R_HD:(idx + 1) * R_HD, 0:R_HD])


def _rwkv_short(r, kp, v, kk, b, lw, state, t_len):
    m = r.shape[0]
    ng = SHORT_ROWS // t_len
    seq_spec = pl.BlockSpec((SHORT_ROWS, R_WIDTH), lambda i: (i, 0))
    st_spec = pl.BlockSpec((ng, R_HEADS, R_HD, R_HD), lambda i: (i, 0, 0, 0))
    return pl.pallas_call(
        functools.partial(_rwkv_short_kernel, t_len=t_len),
        grid=(m // SHORT_ROWS,),
        in_specs=[seq_spec] * 6 + [st_spec],
        out_specs=[seq_spec, st_spec],
        out_shape=[jax.ShapeDtypeStruct((m, R_WIDTH), F32), jax.ShapeDtypeStruct(state.shape, F32)],
        compiler_params=_params("parallel"),
        name="rwkv_short",
    )(r, kp, v, kk, b, lw, state)


ATTN_ROWS = SPAN * max(d for _, d in A_GROUPS)
ATTN_BATCH = 8


def _attn_prompt_kernel(*refs):
    ins = refs[:5 * N_GROUPS]
    o_ref = refs[5 * N_GROUPS]
    o_sc = refs[5 * N_GROUPS + 1:5 * N_GROUPS + 1 + N_GROUPS]
    l_sc = refs[5 * N_GROUPS + 1 + N_GROUPS:]
    n = pl.program_id(1)
    i = lax.broadcasted_iota(jnp.int32, (SPAN, 2 * SPAN), 0)
    j = lax.broadcasted_iota(jnp.int32, (SPAN, 2 * SPAN), 1)
    dist = SPAN + i - j
    band = (dist >= 0) & (dist <= SPAN)
    band_first = band & ((n > 0) | (j >= SPAN))
    scale = 1.0 / math.sqrt(A_HD)
    cat = lambda a, b: jnp.concatenate([a, b], axis=0).astype(BF16)

    for g, (_, dil) in enumerate(A_GROUPS):
        q_ref, kp_ref, kc_ref, vp_ref, vc_ref = ins[5 * g:5 * g + 5]

        def rows(run, rho):
            start = run * SPAN * dil + rho
            return pl.ds(start, SPAN, stride=dil) if dil > 1 else pl.ds(start, SPAN)

        all_probs = [(run, rho) for run in range(ATTN_ROWS // (SPAN * dil)) for rho in range(dil)]
        for b0 in range(0, len(all_probs), ATTN_BATCH):
            probs = all_probs[b0:b0 + ATTN_BATCH]
            q = [q_ref[rows(run, rho), :].astype(BF16) for run, rho in probs]
            k = [cat(kp_ref[rows(0, rho), :] if run == 0 else kc_ref[rows(run - 1, rho), :],
                     kc_ref[rows(run, rho), :]) for run, rho in probs]
            v = [cat(vp_ref[rows(0, rho), :] if run == 0 else vc_ref[rows(run - 1, rho), :],
                     vc_ref[rows(run, rho), :]) for run, rho in probs]
            s = [lax.dot_general(a, b, (((1,), (1,)), ((), ())), preferred_element_type=F32) * scale
                 for a, b in zip(q, k)]
            s = [jnp.where(band_first if run == 0 else band, x, NEG) for x, (run, _) in zip(s, probs)]
            m = [jnp.max(x, axis=-1, keepdims=True) for x in s]
            p = [jnp.exp(x - y) for x, y in zip(s, m)]
            l = [jnp.sum(x, axis=-1, keepdims=True) for x in p]
            o = [jnp.dot(x.astype(BF16), y, preferred_element_type=F32) for x, y in zip(p, v)]
            for (run, rho), oo, ll, mx in zip(probs, o, l, m):
                o_sc[g][rows(run, rho), :] = oo / ll
                l_sc[g][rows(run, rho), :] = jnp.broadcast_to(mx + jnp.log(ll), (SPAN, A_HD))

    lse = [ref[...] for ref in l_sc]
    top = functools.reduce(jnp.maximum, lse)
    e = [jnp.exp(x - top) for x in lse]
    den = functools.reduce(jnp.add, e)
    o_ref[...] = functools.reduce(jnp.add, [(w / den) * ref[...] for w, ref in zip(e, o_sc)])


def _attn_prompt(h, bsz, t):
    nblk = t // ATTN_ROWS
    in_specs = []
    for g, (_, dil) in enumerate(A_GROUPS):
        prev_rows = SPAN * dil
        per_blk = ATTN_ROWS // prev_rows
        col = lambda which, hd, g=g: COL_QKV // A_HD + (which * N_GROUPS + g) * A_HEADS + hd
        cur = lambda which, col=col: pl.BlockSpec(
            (ATTN_ROWS, A_HD), lambda b, n, hd: (b * nblk + n, col(which, hd)))
        prev = lambda which, col=col, per_blk=per_blk, prev_rows=prev_rows: pl.BlockSpec(
            (prev_rows, A_HD),
            lambda b, n, hd: (jnp.maximum((b * nblk + n) * per_blk - 1, 0), col(which, hd)))
        in_specs += [cur(0), prev(1), cur(1), prev(2), cur(2)]
    return pl.pallas_call(
        _attn_prompt_kernel,
        grid=(bsz, nblk, A_HEADS),
        in_specs=in_specs,
        out_specs=pl.BlockSpec((ATTN_ROWS, A_HD), lambda b, n, hd: (b * nblk + n, hd)),
        out_shape=jax.ShapeDtypeStruct((bsz * t, A_WIDTH), F32),
        scratch_shapes=[pltpu.VMEM((ATTN_ROWS, A_HD), F32)] * (2 * N_GROUPS),
        compiler_params=_params("parallel", "arbitrary", "arbitrary"),
        name="attn_prompt",
    )(*([h] * (5 * N_GROUPS)))


SAMPLE_ROWS = 8


def _attn_sample_kernel(*refs, t_new):
    q_refs, k_refs, v_refs = refs[0:3], refs[3:6], refs[6:9]
    c_refs, o_ref = refs[9:12], refs[12]
    scale = 1.0 / math.sqrt(A_HD)
    kvh_n = 2 * A_HEADS
    n_seq = SAMPLE_ROWS // t_new
    n_q = A_HEADS * SAMPLE_ROWS
    n_c = SPAN * kvh_n
    n_l = n_c + LANES
    row = lax.broadcasted_iota(jnp.int32, (n_q, n_l), 0)
    lane = lax.broadcasted_iota(jnp.int32, (n_q, n_l), 1)
    h_r, seq_r, t_r = row // SAMPLE_ROWS, (row % SAMPLE_ROWS) // t_new, row % t_new
    in_buf = lane < n_c
    j_l = lane // kvh_n
    new = lane - n_c
    t_l = new % t_new
    own = (in_buf & ((lane % kvh_n) == h_r)) | (
        ~in_buf & ((new // SAMPLE_ROWS) == h_r) & (((new % SAMPLE_ROWS) // t_new) == seq_r))
    pad = jnp.zeros((LANES - kvh_n * SAMPLE_ROWS, A_HD), F32)
    heads = lambda ref: [ref[:, hd * A_HD:(hd + 1) * A_HD] for hd in range(A_HEADS)]
    outs, lses = [], []
    for g in range(N_GROUPS):
        q = jnp.concatenate(heads(q_refs[g]), axis=0).astype(BF16)
        kv_new = jnp.concatenate(heads(k_refs[g]) + heads(v_refs[g]) + [pad], axis=0)
        kv2, s = [], []
        for seq in range(n_seq):
            for t in ([None] if g == 0 else range(t_new)):
                buf = c_refs[g][seq, :, 0 if t is None else t].reshape(n_c, A_HD)
                kv2.append(jnp.concatenate([buf, kv_new], axis=0).astype(BF16))
                if t is None:
                    valid = own & (seq_r == seq) & ((in_buf & (j_l >= t_r)) | (~in_buf & (t_l <= t_r)))
                else:
                    valid = own & (seq_r == seq) & (t_r == t) & (in_buf | (t_l == t))
                st = lax.dot_general(q, kv2[-1], (((1,), (1,)), ((), ())), preferred_element_type=F32)
                s.append(jnp.where(valid, st * scale, NEG))
        m = functools.reduce(jnp.maximum, [jnp.max(x, axis=-1, keepdims=True) for x in s])
        p = [jnp.exp(x - m) for x in s]
        l = functools.reduce(jnp.add, [jnp.sum(x, axis=-1, keepdims=True) for x in p])
        shift = lambda x: jnp.concatenate(
            [pltpu.roll(x[:, 0:n_c], A_HEADS, 1), pltpu.roll(x[:, n_c:n_l], A_HEADS * SAMPLE_ROWS, 1)],
            axis=1)
        o = functools.reduce(jnp.add, [
            jnp.dot(shift(x).astype(BF16), y, preferred_element_type=F32) for x, y in zip(p, kv2)])
        outs.append(o / l)
        lses.append(m + jnp.log(l))
    top = functools.reduce(jnp.maximum, lses)
    e = [jnp.exp(x - top) for x in lses]
    den = functools.reduce(jnp.add, e)
    o = functools.reduce(jnp.add, [(w / den) * x for w, x in zip(e, outs)])
    for hd in range(A_HEADS):
        o_ref[:, hd * A_HD:(hd + 1) * A_HD] = o[hd * SAMPLE_ROWS:(hd + 1) * SAMPLE_ROWS]


def _attn_sample(h, caches, bsz, t_new):
    kvh_n = 2 * A_HEADS
    n_seq = SAMPLE_ROWS // t_new
    assert SAMPLE_ROWS % t_new == 0 and bsz % n_seq == 0 and COL_QKV % A_WIDTH == 0
    views = []
    specs = []
    for (win, dil), cache in zip(A_GROUPS, caches):
        assert cache.shape[1] == win and win // dil == SPAN and (dil == 1 or t_new <= dil)
        views.append(cache.reshape(bsz, SPAN, dil, kvh_n, A_HD))
        specs.append(pl.BlockSpec((n_seq, SPAN, min(dil, t_new), kvh_n, A_HD), lambda i: (i, 0, 0, 0, 0)))
    col = lambda which, g: pl.BlockSpec(
        (SAMPLE_ROWS, A_WIDTH), lambda i: (i, COL_QKV // A_WIDTH + which * N_GROUPS + g))
    qkv_specs = [col(which, g) for which in range(3) for g in range(N_GROUPS)]
    return pl.pallas_call(
        functools.partial(_attn_sample_kernel, t_new=t_new),
        grid=(bsz // n_seq,),
        in_specs=qkv_specs + specs,
        out_specs=pl.BlockSpec((SAMPLE_ROWS, A_WIDTH), lambda i: (i, 0)),
        out_shape=jax.ShapeDtypeStruct((bsz * t_new, A_WIDTH), F32),
        compiler_params=_params("parallel"),
        name="attn_sample",
    )(*([h] * (3 * N_GROUPS)), *views)


def _final_kernel(x_ref, o_ref, bon_ref, zr_ref, oa_ref, zgg_ref, seg_ref, lnxg_ref, lnxb_ref, bg_ref,
                  woa_ref, wob_ref, wout_ref, lng_ref, lnb_ref, y_ref):
    seg2 = seg_ref[...]
    o = o_ref[...]
    mu = _seg_sum(o, seg2) * (1.0 / R_HD)
    d = o - mu
    var = _seg_sum(d * d, seg2) * (1.0 / R_HD)
    on = d * lax.rsqrt(var + GN_EPS) * lnxg_ref[...] + lnxb_ref[...]
    zr = zr_ref[...]
    y_r = (on + bon_ref[...].astype(F32)) * (zr * jax.nn.sigmoid(zr))

    z_a = zgg_ref[:, 0:A_WIDTH]
    y_a = oa_ref[...] * (z_a * jax.nn.sigmoid(z_a))

    g_r = zgg_ref[:, A_WIDTH:A_WIDTH + D_MODEL]
    g_a = zgg_ref[:, A_WIDTH + D_MODEL:A_WIDTH + 2 * D_MODEL]
    gate_r = jax.nn.sigmoid(g_r + bg_ref[:, 0:D_MODEL])
    gate_a = jax.nn.sigmoid(g_a + bg_ref[:, D_MODEL:2 * D_MODEL])
    mix = (gate_r * jnp.dot(y_r.astype(BF16), woa_ref[...], preferred_element_type=F32)
           + gate_a * jnp.dot(y_a.astype(BF16), wob_ref[...], preferred_element_type=F32))
    yy = ALPHA * x_ref[...] + jnp.dot(mix.astype(BF16), wout_ref[...], preferred_element_type=F32)
    mean = jnp.mean(yy, axis=-1, keepdims=True)
    cen = yy - mean
    variance = jnp.mean(cen * cen, axis=-1, keepdims=True)
    y_ref[...] = cen * lax.rsqrt(variance + LN_EPS) * lng_ref[...] + lnb_ref[...]


def _final(x, o, bonus, h, o_attn, weights, tm):
    m = x.shape[0]
    row = lambda w, j=0: pl.BlockSpec((tm, w), lambda i: (i, j))
    const = lambda a: pl.BlockSpec(a.shape, lambda i: (0, 0))
    in_specs = ([row(D_MODEL), row(R_WIDTH), row(R_WIDTH), row(R_WIDTH, COL_ZR // R_WIDTH), row(A_WIDTH),
                 row(ZGG_W, COL_ZGG // ZGG_W)] + [const(a) for a in weights])
    return pl.pallas_call(
        _final_kernel,
        grid=(m // tm,),
        in_specs=in_specs,
        out_specs=row(D_MODEL),
        out_shape=jax.ShapeDtypeStruct((m, D_MODEL), F32),
        compiler_params=_params("parallel"),
        name="merge_out_ln",
    )(x, o, bonus, h, o_attn, h, *weights)


def _layer(x2, bsz, t, shift_prev, wkv0, caches, wts):
    m = x2.shape[0]
    h = _in_proj(x2, wts["w_proj"], 2048 if m % 2048 == 0 else m)

    tm = 512 if m % 512 == 0 else m
    if caches is None:
        first = None
    else:
        first = jnp.zeros((bsz, t, SHIFT_COLS), F32).at[:, 0].set(shift_prev).reshape(m, SHIFT_COLS)
    r, kp, v, kk, b, lw, bonus = _rwkv_prep(h, first, t, wts["prep"], tm)

    if t % CHUNK == 0 and wkv0 is None:
        seq = lambda a: a.reshape(bsz, t, R_WIDTH)
        s0 = jnp.zeros((bsz, N_PAIRS, LANES, LANES), F32)
        o_seq, st = _rwkv_recurrence(seq(r), seq(kp), seq(v), seq(kk), seq(b), seq(lw), s0, 2)
        o_rwkv = o_seq.reshape(m, R_WIDTH)
        wkv_new = _pairs_to_state(st)
    else:
        o_rwkv, wkv_new = _rwkv_short(r, kp, v, kk, b, lw, wkv0, t)

    def rows_cols(r0, c0, width):
        if t % 8 == 0:
            return h.reshape(bsz, t, N_PROJ)[:, r0:, c0:c0 + width]
        return h[:, c0:c0 + width].reshape(bsz, t, width)[:, r0:]

    if caches is None:
        o_attn = _attn_prompt(h, bsz, t)
    else:
        o_attn = _attn_sample(h, caches, bsz, t)

    tmf = 256 if m % 256 == 0 else m
    y = _final(x2, o_rwkv, bonus, h, o_attn, wts["final"], tmf)

    new_kv = []
    for g, (win, _) in enumerate(A_GROUPS):
        keep = min(win, t) if caches is None else t
        def part(which):
            c0 = COL_QKV + (which * N_GROUPS + g) * A_WIDTH
            return rows_cols(t - keep, c0, A_WIDTH).reshape(bsz, keep, A_HEADS, A_HD)
        new_kv.append(jnp.stack([part(1), part(2)], axis=2))
    shift_new = rows_cols(t - 1, COL_ZS, SHIFT_COLS)[:, 0]
    return y.reshape(bsz, t, D_MODEL), new_kv, wkv_new, shift_new


def kernel(x_prompt, x_sample, cache_kv_g1, cache_kv_g2, cache_kv_g3, state_rwkv_wkv, state_rwkv_shift,
           w_in, b_gate, mu_shift, w0, w_w2, a0, w_a2, k_k, k_a, r_k, lnx_g, lnx_b,
           w_oa, w_ob, w_out, ln_g, ln_b):
    assert w_in.shape[0] == DEPTH
    bp, tp, _ = x_prompt.shape
    bs, ts, _ = x_sample.shape
    xp = x_prompt.reshape(bp * tp, D_MODEL)
    xs = x_sample.reshape(bs * ts, D_MODEL)
    head = lax.broadcasted_iota(jnp.int32, (LANES, LANES), 0) // R_HD
    seg = (head == head.T).astype(BF16)
    seg2 = jnp.concatenate([seg, seg], axis=0)
    row = lambda a: a.reshape(1, -1)
    acc = [[] for _ in range(12)]
    for l in range(DEPTH):
        wb = w_in[l].astype(BF16)
        c0 = SHIFT_COLS
        c1 = c0 + R_WIDTH
        c2 = c1 + 3 * A_QKV_W
        w_proj = jnp.concatenate([wb[:, :c0], jnp.zeros((D_MODEL, COL_ZR - c0), BF16), wb[:, c0:c1],
                                  wb[:, c2:], wb[:, c1:c2]], axis=1)
        zero = jnp.zeros((LORA, R_WIDTH), F32)
        w_lora = jnp.concatenate([jnp.concatenate([w_w2[l], zero], axis=1),
                                  jnp.concatenate([zero, w_a2[l]], axis=1)], axis=0)
        wl_hi = w_lora.astype(BF16)
        wl_lo = (w_lora - wl_hi.astype(F32)).astype(BF16)
        wts = {
            "w_proj": w_proj,
            "prep": (row(mu_shift[l]), row(w0[l]), row(a0[l]), jnp.concatenate([wl_hi, wl_hi], axis=0),
                     wl_lo, row(k_k[l]), row(k_a[l]), row(r_k[l]), seg2),
            "final": (seg2, row(lnx_g[l]), row(lnx_b[l]), row(b_gate[l]), w_oa[l].astype(BF16),
                      w_ob[l].astype(BF16), w_out[l].astype(BF16), row(ln_g[l]), row(ln_b[l])),
        }
        yp, kv_p, s_p, sh_p = _layer(xp, bp, tp, None, None, None, wts)
        ys, kv_s, s_s, sh_s = _layer(xs, bs, ts, state_rwkv_shift[l], state_rwkv_wkv[l],
                                     (cache_kv_g1[l], cache_kv_g2[l], cache_kv_g3[l]), wts)
        xp = yp.reshape(bp * tp, D_MODEL)
        xs = ys.reshape(bs * ts, D_MODEL)
        for idx, val in zip(range(2, 12), (kv_p[0], kv_s[0], kv_p[1], kv_s[1], kv_p[2], kv_s[2],
                                           s_p, s_s, sh_p, sh_s)):
            acc[idx].append(val)
    outs = [xp.reshape(bp, tp, D_MODEL), xs.reshape(bs, ts, D_MODEL)]
    outs += [jnp.stack(a) for a in acc[2:]]
    return tuple(outs)
```

```python
import functools
import math

import jax
import jax.numpy as jnp
from jax import lax
from jax.experimental import pallas as pl
from jax.experimental.pallas import tpu as pltpu

F32 = jnp.float32
BF16 = jnp.bfloat16

D_MODEL = 1024
R_HEADS = 8
R_HD = 64
R_WIDTH = R_HEADS * R_HD
LORA = 64
SHIFT_COLS = 3 * R_WIDTH + 2 * LORA
GN_EPS = 64e-5
A_GROUPS = ((128, 1), (512, 4), (2048, 16))
N_GROUPS = 3
A_HEADS = 4
A_HD = 128
A_WIDTH = A_HEADS * A_HD
A_QKV_W = N_GROUPS * A_WIDTH
SPAN = 128
DEPTH = 1
ALPHA = (2 * DEPTH) ** 0.25
LN_EPS = 1e-5
NEG = -1e30

LANES = 128
CHUNK = 64
N_PAIRS = R_HEADS // 2
VMEM_LIMIT = 48 * 1024 * 1024

PROJ_TN = 512
COL_ZS = 0
COL_Q = 2048
COL_KV = COL_Q + A_QKV_W
N_PROJ32 = COL_KV + 2 * A_QKV_W
ZGG_W = A_WIDTH + 2 * D_MODEL
COL_ZGG = 0
COL_ZR = ZGG_W
N_PROJ16 = COL_ZR + R_WIDTH
assert COL_Q % A_WIDTH == 0 and COL_KV % A_WIDTH == 0 and COL_ZR % R_WIDTH == 0
assert N_PROJ32 % PROJ_TN == 0 and N_PROJ16 % PROJ_TN == 0


def _col_q(g):
    return COL_Q + g * A_WIDTH


def _col_k(g):
    return COL_KV + g * 2 * A_WIDTH


def _col_v(g):
    return _col_k(g) + A_WIDTH


def _params(*sem):
    return pltpu.CompilerParams(dimension_semantics=sem, vmem_limit_bytes=VMEM_LIMIT)


def _split2(x):
    hi = x.astype(BF16)
    return hi, (x - hi.astype(F32)).astype(BF16)


def _seg_sum(x, seg2):
    cols = []
    for c in range(x.shape[1] // LANES):
        hi, lo = _split2(x[:, c * LANES:(c + 1) * LANES])
        cols.append(jnp.dot(jnp.concatenate([hi, lo], axis=1), seg2, preferred_element_type=F32))
    return jnp.concatenate(cols, axis=1)


def _mm_kernel(x_ref, w_ref, o32_ref, o16_ref, *, n32):
    j = pl.program_id(1)
    product = lambda: jnp.dot(x_ref[...].astype(BF16), w_ref[...], preferred_element_type=F32)

    @pl.when(j < n32)
    def _():
        o32_ref[...] = product()

    @pl.when(j >= n32)
    def _():
        o16_ref[...] = product().astype(BF16)


def _in_proj(x, w, tm):
    m, k = x.shape
    n32 = N_PROJ32 // PROJ_TN
    n16 = N_PROJ16 // PROJ_TN
    return pl.pallas_call(
        functools.partial(_mm_kernel, n32=n32),
        grid=(m // tm, n32 + n16),
        in_specs=[pl.BlockSpec((tm, k), lambda i, j: (i, 0)),
                  pl.BlockSpec((k, PROJ_TN), lambda i, j: (0, j))],
        out_specs=[pl.BlockSpec((tm, PROJ_TN), lambda i, j: (i, jnp.minimum(j, n32 - 1))),
                   pl.BlockSpec((tm, PROJ_TN), lambda i, j: (i, jnp.maximum(j - n32, 0)))],
        out_shape=[jax.ShapeDtypeStruct((m, N_PROJ32), F32), jax.ShapeDtypeStruct((m, N_PROJ16), BF16)],
        compiler_params=_params("parallel", "arbitrary"),
        name="in_proj",
    )(x, w)


def _prep_math(zs, prev, mu, w0, a0, wl_a, wl_b, k_k, k_a, r_k, seg2, outs):
    r_o, kp_o, v_o, kk_o, b_o, lw_o, bon_o = outs
    zsm = zs + mu * (prev - zs)
    r = zsm[:, 0:R_WIDTH]
    k = zsm[:, R_WIDTH:2 * R_WIDTH]
    v = zsm[:, 2 * R_WIDTH:3 * R_WIDTH]
    slab = zsm[:, 3 * R_WIDTH:SHIFT_COLS]
    lane = lax.broadcasted_iota(jnp.int32, slab.shape, 1)
    slab = jnp.where(lane < LORA, jnp.tanh(slab), slab)
    s_hi, s_lo = _split2(slab)
    lin = (jnp.dot(jnp.concatenate([s_hi, s_lo], axis=1), wl_a, preferred_element_type=F32)
           + jnp.dot(s_hi, wl_b, preferred_element_type=F32))
    wlin = w0 + lin[:, 0:R_WIDTH]
    alin = a0 + lin[:, R_WIDTH:2 * R_WIDTH]
    nw = -wlin
    softplus = jnp.maximum(nw, 0.0) + jnp.log1p(jnp.exp(-jnp.abs(nw)))
    w = -softplus - 0.5
    lw = -jnp.exp(w)
    a = jax.nn.sigmoid(alin)
    kkr = k * k_k
    kk = kkr / jnp.maximum(jnp.sqrt(_seg_sum(kkr * kkr, seg2)), 1e-12)
    kp = k * (1.0 + (a - 1.0) * k_a)
    r_o[...] = r.astype(r_o.dtype)
    kp_o[...] = kp.astype(kp_o.dtype)
    v_o[...] = v.astype(v_o.dtype)
    kk_o[...] = kk.astype(kk_o.dtype)
    b_o[...] = (kk * a).astype(b_o.dtype)
    lw_o[...] = lw
    bon_o[...] = (_seg_sum(r * kp * r_k, seg2) * v).astype(bon_o.dtype)


def _prep_prompt_kernel(zs_ref, p8_ref, mu_ref, w0_ref, a0_ref, wla_ref, wlb_ref, kk_ref, ka_ref,
                        rk_ref, seg_ref, *outs, tiles_per_seq):
    i = pl.program_id(0)
    zs = zs_ref[...]
    rolled = pltpu.roll(zs, 1, 0)
    carry = jnp.where(i % tiles_per_seq == 0, 0.0, p8_ref[7:8, :])
    row = lax.broadcasted_iota(jnp.int32, zs.shape, 0)
    prev = jnp.where(row == 0, carry, rolled)
    _prep_math(zs, prev, mu_ref[...], w0_ref[...], a0_ref[...], wla_ref[...], wlb_ref[...],
               kk_ref[...], ka_ref[...], rk_ref[...], seg_ref[...], outs)


def _prep_sample_kernel(zs_ref, first_ref, mu_ref, w0_ref, a0_ref, wla_ref, wlb_ref, kk_ref, ka_ref,
                        rk_ref, seg_ref, *outs, seq_len):
    zs = zs_ref[...]
    rolled = pltpu.roll(zs, 1, 0)
    row = lax.broadcasted_iota(jnp.int32, zs.shape, 0)
    prev = jnp.where(row % seq_len == 0, first_ref[...], rolled)
    _prep_math(zs, prev, mu_ref[...], w0_ref[...], a0_ref[...], wla_ref[...], wlb_ref[...],
               kk_ref[...], ka_ref[...], rk_ref[...], seg_ref[...], outs)


def _rwkv_prep(h, first, seq_len, weights, tm):
    m = h.shape[0]
    const = lambda a: pl.BlockSpec(a.shape, lambda i: (0, 0))
    row_spec = pl.BlockSpec((tm, SHIFT_COLS), lambda i: (i, COL_ZS))
    if first is None:
        kern = functools.partial(_prep_prompt_kernel, tiles_per_seq=seq_len // tm)
        second = h
        second_spec = pl.BlockSpec((8, SHIFT_COLS), lambda i: (jnp.maximum(i * (tm // 8) - 1, 0), COL_ZS))
    else:
        kern = functools.partial(_prep_sample_kernel, seq_len=seq_len)
        second = first
        second_spec = pl.BlockSpec((tm, SHIFT_COLS), lambda i: (i, 0))
    out_spec = pl.BlockSpec((tm, R_WIDTH), lambda i: (i, 0))
    return pl.pallas_call(
        kern,
        grid=(m // tm,),
        in_specs=[row_spec, second_spec] + [const(a) for a in weights],
        out_specs=[out_spec] * 7,
        out_shape=[jax.ShapeDtypeStruct((m, R_WIDTH), dt) for dt in (BF16,) * 5 + (F32, BF16)],
        compiler_params=_params("parallel"),
        name="rwkv_prep",
    )(h, second, *weights)


CHUNKS_PER_STEP = 2


def _rwkv_chunk_kernel(r_c, kp_c, v_c, kk_c, b_c, lw_c, r_n, kp_n, v_n, kk_n, b_n, lw_n, s0_ref,
                       o_ref, st_ref, yo_sc, ti_sc, mb_sc, ut_sc, vs_sc, dec_sc, *, nb):
    n = pl.program_id(1)
    c = CHUNK
    c2 = 2 * c
    lane = lax.broadcasted_iota(jnp.int32, (1, LANES), 1)
    m_lo = (lane < R_HD).astype(F32)
    m_hi = 1.0 - m_lo
    row_c = lax.broadcasted_iota(jnp.int32, (c, 4 * c), 0)
    col_c = lax.broadcasted_iota(jnp.int32, (c, 4 * c), 1)
    tri3 = ((row_c >= col_c % c) & (col_c < 3 * c)).astype(BF16)
    row2 = lax.broadcasted_iota(jnp.int32, (c2, c2), 0)
    col2 = lax.broadcasted_iota(jnp.int32, (c2, c2), 1)
    strict = row2 > col2
    incl = row2 >= col2
    eye = (row2 == col2).astype(F32)

    def stack(x):
        return jnp.concatenate([x * m_lo, x * m_hi], axis=0)

    def mm(a, b):
        return jnp.dot(a.astype(BF16), b.astype(BF16), preferred_element_type=F32)

    def nt(a, b):
        return lax.dot_general(a.astype(BF16), b.astype(BF16), (((1,), (1,)), ((), ())),
                               preferred_element_type=F32)

    each = lambda f, *xs: [f(*a) for a in zip(*xs)]
    chains = [(bi, slice(p * LANES, (p + 1) * LANES), p) for bi in range(nb) for p in range(N_PAIRS)]

    def cumsum(lw_all):
        hi = lw_all.astype(BF16)
        r1 = lw_all - hi.astype(F32)
        mid = r1.astype(BF16)
        lo = (r1 - mid.astype(F32)).astype(BF16)
        return jnp.dot(tri3, jnp.concatenate([hi, mid, lo, jnp.zeros_like(lo)], axis=0),
                       preferred_element_type=F32)

    def prepare(refs, row0, slot):
        lw_ref = refs[5]
        rs = slice(row0, row0 + c)
        g_all = [cumsum(lw_ref[bi, rs, :]) for bi in range(nb)]
        yield
        load = lambda ref: [ref[bi, rs, sl].astype(F32) for bi, sl, _ in chains]
        r, kp, v, kk, b, lw = (load(ref) for ref in refs)
        g = [g_all[bi][:, sl] for bi, sl, _ in chains]
        g_last = each(lambda x: x[c - 1:c, :], g)
        e_g = each(jnp.exp, g)
        e_ng = each(lambda x: jnp.exp(-x), g)
        e_gm = each(lambda x, y: jnp.exp(x - y), g, lw)
        e_cg = each(lambda x, y: jnp.exp(y - x), g, g_last)
        ad_s = each(lambda x, y: stack(-(x * y)), kk, e_gm)
        rd_s = each(lambda x, y: stack(x * y), r, e_g)
        bd_s = each(lambda x, y: stack(x * y), b, e_ng)
        kd_s = each(lambda x, y: stack(x * y), kp, e_ng)
        be_s = each(lambda x, y: stack(x * y), b, e_cg)
        ke_s = each(lambda x, y: stack(x * y), kp, e_cg)
        for idx in range(len(chains)):
            yo_sc[slot, idx, 0:c2, 0:c2] = ad_s[idx].astype(BF16)
            yo_sc[slot, idx, c2:2 * c2, 0:c2] = rd_s[idx].astype(BF16)
            ut_sc[slot, idx, :, 0:c2] = jnp.transpose(be_s[idx]).astype(BF16)
            ut_sc[slot, idx, :, c2:2 * c2] = jnp.transpose(ke_s[idx]).astype(BF16)
            vs_sc[slot, idx] = stack(v[idx]).astype(BF16)
            dec_sc[slot, idx] = jnp.transpose(jnp.broadcast_to(jnp.exp(g_last[idx]), (LANES, LANES)))
        yield
        z = each(lambda a1, a2, b1, b2: nt(jnp.concatenate([a1, a2], axis=0),
                                           jnp.concatenate([b1, b2], axis=0)), ad_s, rd_s, bd_s, kd_s)
        yield
        for idx, zz in enumerate(z):
            yo_sc[slot, idx, 0:c2, c2:2 * c2] = jnp.where(strict, zz[0:c2, c2:2 * c2], 0.0).astype(BF16)
            yo_sc[slot, idx, c2:2 * c2, c2:2 * c2] = jnp.where(
                incl, zz[c2:2 * c2, c2:2 * c2], 0.0).astype(BF16)
            mb_sc[slot, idx] = jnp.where(incl, zz[c2:2 * c2, 0:c2], 0.0).astype(BF16)
        n_b = each(lambda x: jnp.where(strict, x[0:c2, 0:c2], 0.0), z)
        t_inv = each(lambda x: eye + x, n_b)
        q = each(lambda x: mm(x, x), n_b)
        yield
        steps = int(math.log2(c))
        for i in range(1, steps):
            if i < steps - 1:
                res = each(lambda x, tt: mm(x, jnp.concatenate([x, tt], axis=1)), q, t_inv)
                q = each(lambda x: x[:, 0:c2], res)
                t_inv = each(lambda tt, x: tt + x[:, c2:2 * c2], t_inv, res)
            else:
                t_inv = each(lambda tt, x: tt + mm(x, tt), t_inv, q)
            yield
        for idx, tt in enumerate(t_inv):
            ti_sc[slot, idx] = tt.astype(BF16)

    def advance(slot, row0):
        idxs = list(range(len(chains)))
        st = [st_ref[bi, p] for bi, _, p in chains]
        vs = [vs_sc[slot, i] for i in idxs]
        yo = [jnp.dot(yo_sc[slot, i], jnp.concatenate([s.astype(BF16), w], axis=0),
                      preferred_element_type=F32) for i, s, w in zip(idxs, st, vs)]
        yield
        u = [jnp.dot(ti_sc[slot, i], x[0:c2].astype(BF16), preferred_element_type=F32)
             for i, x in zip(idxs, yo)]
        ub = [x.astype(BF16) for x in u]
        yield
        o_bd = [x[c2:2 * c2] + jnp.dot(mb_sc[slot, i], w, preferred_element_type=F32)
                for i, x, w in zip(idxs, yo, ub)]
        for (bi, sl, _), o in zip(chains, o_bd):
            o_ref[bi, row0:row0 + c, sl] = o[0:c] + o[c:c2]
        yield
        upd = [jnp.dot(ut_sc[slot, i], jnp.concatenate([w, x], axis=0), preferred_element_type=F32)
               for i, w, x in zip(idxs, ub, vs)]
        for i, ((bi, _, p), s, d) in enumerate(zip(chains, st, upd)):
            st_ref[bi, p] = dec_sc[slot, i] * s + d

    def interleave(*stages):
        live = list(stages)
        while live:
            for gen in list(live):
                if next(gen, StopIteration) is StopIteration:
                    live.remove(gen)

    cur = (r_c, kp_c, v_c, kk_c, b_c, lw_c)
    nxt = (r_n, kp_n, v_n, kk_n, b_n, lw_n)

    @pl.when(n == 0)
    def _():
        st_ref[...] = s0_ref[...]
        interleave(prepare(cur, 0, 0))

    for k in range(CHUNKS_PER_STEP):
        following = (prepare(cur, (k + 1) * c, (k + 1) % 2) if k + 1 < CHUNKS_PER_STEP
                     else prepare(nxt, 0, (k + 1) % 2))
        interleave(advance(k % 2, k * c), following)


def _rwkv_recurrence(r, kp, v, kk, b, lw, s0, nb):
    bsz, t, _ = r.shape
    rows = CHUNKS_PER_STEP * CHUNK
    nsteps = t // rows
    seq_spec = pl.BlockSpec((nb, rows, R_WIDTH), lambda i, n: (i, n, 0))
    nxt_spec = pl.BlockSpec((nb, rows, R_WIDTH), lambda i, n: (i, jnp.minimum(n + 1, nsteps - 1), 0))
    st_spec = pl.BlockSpec((nb, N_PAIRS, LANES, LANES), lambda i, n: (i, 0, 0, 0))
    nchain = nb * N_PAIRS
    c2 = 2 * CHUNK
    assert CHUNKS_PER_STEP % 2 == 0
    slots = 2
    scratch = [pltpu.VMEM((slots, nchain, 2 * c2, 2 * c2), BF16),
               pltpu.VMEM((slots, nchain, c2, c2), BF16),
               pltpu.VMEM((slots, nchain, c2, c2), BF16),
               pltpu.VMEM((slots, nchain, c2, 2 * c2), BF16),
               pltpu.VMEM((slots, nchain, c2, c2), BF16),
               pltpu.VMEM((slots, nchain, LANES, LANES), F32)]
    return pl.pallas_call(
        functools.partial(_rwkv_chunk_kernel, nb=nb),
        grid=(bsz // nb, nsteps),
        in_specs=[seq_spec] * 6 + [nxt_spec] * 6 + [st_spec],
        out_specs=[seq_spec, st_spec],
        out_shape=[jax.ShapeDtypeStruct((bsz, t, R_WIDTH), F32),
                   jax.ShapeDtypeStruct(s0.shape, F32)],
        scratch_shapes=scratch,
        compiler_params=_params("parallel", "arbitrary"),
        name="rwkv_chunks",
    )(r, kp, v, kk, b, lw, r, kp, v, kk, b, lw, s0)


def _pairs_to_state(st):
    bsz = st.shape[0]
    s6 = st.reshape(bsz, N_PAIRS, 2, R_HD, 2, R_HD)
    s = jnp.stack([s6[:, :, 0, :, 0, :], s6[:, :, 1, :, 1, :]], axis=2)
    return jnp.swapaxes(s.reshape(bsz, R_HEADS, R_HD, R_HD), -1, -2)


SHORT_ROWS = 64


def _rwkv_short_kernel(r_ref, kp_ref, v_ref, kk_ref, b_ref, lw_ref, s_ref, o_ref, sn_ref, *, t_len):
    rows = SHORT_ROWS
    ng = rows // t_len
    r2 = 2 * rows
    assert 2 * t_len == 8 and r2 == LANES
    lane = lax.broadcasted_iota(jnp.int32, (1, LANES), 1)
    m_lo = (lane < R_HD).astype(F32)
    m_hi = 1.0 - m_lo
    lane_blk = lane // t_len
    row_c = lax.broadcasted_iota(jnp.int32, (rows, 4 * rows), 0)
    col_c = lax.broadcasted_iota(jnp.int32, (rows, 4 * rows), 1)
    j_c = col_c % rows
    same_c = ((row_c // t_len) == (j_c // t_len)) & (col_c < 3 * rows)
    cum_lhs = jnp.concatenate([(same_c & (row_c >= j_c)).astype(BF16), same_c.astype(BF16)], axis=0)
    row2 = lax.broadcasted_iota(jnp.int32, (r2, r2), 0)
    col2 = lax.broadcasted_iota(jnp.int32, (r2, r2), 1)
    blk = (row2 // t_len) == (col2 // t_len)
    strict = blk & (row2 > col2)
    incl = blk & (row2 >= col2)
    first_seq = lax.broadcasted_iota(jnp.int32, (8, LANES), 0) < t_len
    zero_half = jnp.zeros((R_HD, R_HD), F32)

    def stack(x):
        return jnp.concatenate([x * m_lo, x * m_hi], axis=0)

    def half(x):
        return jnp.concatenate([x * m_lo, pltpu.roll(x, R_HD, 1) * m_lo], axis=0)

    def mm(a, b):
        return jnp.dot(a.astype(BF16), b.astype(BF16), preferred_element_type=F32)

    def nt(a, b):
        return lax.dot_general(a.astype(BF16), b.astype(BF16), (((1,), (1,)), ((), ())),
                               preferred_element_type=F32)

    each = lambda f, *xs: [f(*a) for a in zip(*xs)]

    lw_all = lw_ref[...]
    hi = lw_all.astype(BF16)
    r1 = lw_all - hi.astype(F32)
    mid = r1.astype(BF16)
    lo = (r1 - mid.astype(F32)).astype(BF16)
    g_both = jnp.dot(cum_lhs, jnp.concatenate([hi, mid, lo, jnp.zeros_like(lo)], axis=0),
                     preferred_element_type=F32)

    pairs = list(range(N_PAIRS))
    sls = [slice(p * LANES, (p + 1) * LANES) for p in pairs]
    load = lambda ref: [ref[:, sl].astype(F32) for sl in sls]
    r, kp, v, kk, b, lw = (load(ref) for ref in (r_ref, kp_ref, v_ref, kk_ref, b_ref, lw_ref))
    g = [g_both[0:rows, sl] for sl in sls]
    g_tot = [g_both[rows:r2, sl] for sl in sls]
    e_g = each(jnp.exp, g)
    e_ng = each(lambda x: jnp.exp(-x), g)
    e_gm = each(lambda x, y: jnp.exp(x - y), g, lw)
    e_cg = each(lambda x, y: jnp.exp(y - x), g, g_tot)
    ad = each(lambda x, y: -(x * y), kk, e_gm)
    rd = each(lambda x, y: x * y, r, e_g)
    bd = each(lambda x, y: x * y, b, e_ng)
    kd = each(lambda x, y: x * y, kp, e_ng)
    be_h = each(lambda x, y: half(x * y), b, e_cg)
    ke_h = each(lambda x, y: half(x * y), kp, e_cg)
    v_h = each(half, v)
    ad_h = each(half, ad)
    rd_h = each(half, rd)
    dec_h = each(lambda x: half(jnp.exp(x)), g_tot)

    z = each(lambda a1, a2, b1, b2: nt(jnp.concatenate([stack(a1), stack(a2)], axis=0),
                                       jnp.concatenate([stack(b1), stack(b2)], axis=0)), ad, rd, bd, kd)
    n_b = each(lambda x: jnp.where(strict, x[0:r2, 0:r2], 0.0), z)
    n_k = each(lambda x: jnp.where(strict, x[0:r2, r2:2 * r2], 0.0), z)
    m_b = each(lambda x: jnp.where(incl, x[r2:2 * r2, 0:r2], 0.0), z)
    m_k = each(lambda x: jnp.where(incl, x[r2:2 * r2, r2:2 * r2], 0.0), z)

    y0, o0 = [], []
    for p in pairs:
        y_tiles, o_tiles = [], []
        for hh in range(2):
            for i in range(ng // 2):
                rs = slice(hh * rows + 8 * i, hh * rows + 8 * i + 8)
                lhs = jnp.concatenate([ad_h[p][rs], rd_h[p][rs]], axis=0)[:, 0:R_HD]
                res = [nt(lhs, jnp.concatenate([s_ref[2 * i + w, 2 * p + hh], zero_half], axis=0))
                       for w in range(2)]
                y_tiles.append(jnp.where(first_seq, res[0][0:8], res[1][0:8]))
                o_tiles.append(jnp.where(first_seq, res[0][8:16], res[1][8:16]))
        y0.append(jnp.concatenate(y_tiles, axis=0))
        o0.append(jnp.concatenate(o_tiles, axis=0))

    y = each(lambda a, nk, vh: a + mm(nk, vh), y0, n_k, v_h)
    q = n_b
    steps = int(math.log2(t_len))
    for i in range(steps):
        if i < steps - 1:
            res = each(lambda x, yy: mm(x, jnp.concatenate([x, yy], axis=1)), q, y)
            q = each(lambda x: x[:, 0:r2], res)
            y = each(lambda yy, x: yy + x[:, r2:2 * r2], y, res)
        else:
            y = each(lambda yy, x: yy + mm(x, yy), y, q)
    u = y

    o_h = each(lambda a, mb, uu, mk, vh: a + mm(mb, uu) + mm(mk, vh), o0, m_b, u, m_k, v_h)
    for sl, o in zip(sls, o_h):
        o_ref[:, sl] = o[0:rows] + pltpu.roll(o[rows:r2], R_HD, 1)

    for p in pairs:
        ut = jnp.transpose(u[p])[0:R_HD]
        vt = jnp.transpose(v_h[p])[0:R_HD]
        rhs = jnp.concatenate([be_h[p], ke_h[p]], axis=0)
        keys = [(hh, s) for hh in range(2) for s in range(ng)]
        sel = [(lane_blk == hh * ng + s).astype(F32) for hh, s in keys]
        lhs = jnp.concatenate([jnp.concatenate([ut * m, vt * m], axis=1) for m in sel], axis=0)
        upd = mm(lhs, rhs)
        for idx, (hh, s) in enumerate(keys):
            row0 = hh * rows + s * t_len
            dec = dec_h[p][row0:row0 + 1, 0:R_HD]
            sn_ref[s, 2 * p + hh] = (s_ref[s, 2 * p + hh] * dec
                                     + upd[idx * R_HD:(idx + 1) * R_HD, 0:R_HD])


def _rwkv_short(r, kp, v, kk, b, lw, state, t_len):
    m = r.shape[0]
    ng = SHORT_ROWS // t_len
    seq_spec = pl.BlockSpec((SHORT_ROWS, R_WIDTH), lambda i: (i, 0))
    st_spec = pl.BlockSpec((ng, R_HEADS, R_HD, R_HD), lambda i: (i, 0, 0, 0))
    return pl.pallas_call(
        functools.partial(_rwkv_short_kernel, t_len=t_len),
        grid=(m // SHORT_ROWS,),
        in_specs=[seq_spec] * 6 + [st_spec],
        out_specs=[seq_spec, st_spec],
        out_shape=[jax.ShapeDtypeStruct((m, R_WIDTH), F32), jax.ShapeDtypeStruct(state.shape, F32)],
        compiler_params=_params("parallel"),
        name="rwkv_short",
    )(r, kp, v, kk, b, lw, state)


ATTN_ROWS = SPAN * max(d for _, d in A_GROUPS)
ATTN_BATCH = 8


def _attn_prompt_kernel(*refs):
    ins = refs[:5 * N_GROUPS]
    o_ref = refs[5 * N_GROUPS]
    o_sc = refs[5 * N_GROUPS + 1:5 * N_GROUPS + 1 + N_GROUPS]
    l_sc = refs[5 * N_GROUPS + 1 + N_GROUPS:]
    n = pl.program_id(1)
    i = lax.broadcasted_iota(jnp.int32, (SPAN, 2 * SPAN), 0)
    j = lax.broadcasted_iota(jnp.int32, (SPAN, 2 * SPAN), 1)
    dist = SPAN + i - j
    band = (dist >= 0) & (dist <= SPAN)
    band_first = band & ((n > 0) | (j >= SPAN))
    scale = 1.0 / math.sqrt(A_HD)
    cat = lambda a, b: jnp.concatenate([a, b], axis=0).astype(BF16)

    for g, (_, dil) in enumerate(A_GROUPS):
        q_ref, kp_ref, kc_ref, vp_ref, vc_ref = ins[5 * g:5 * g + 5]

        def rows(run, rho):
            start = run * SPAN * dil + rho
            return pl.ds(start, SPAN, stride=dil) if dil > 1 else pl.ds(start, SPAN)

        all_probs = [(run, rho) for run in range(ATTN_ROWS // (SPAN * dil)) for rho in range(dil)]
        for b0 in range(0, len(all_probs), ATTN_BATCH):
            probs = all_probs[b0:b0 + ATTN_BATCH]
            q = [q_ref[rows(run, rho), :].astype(BF16) for run, rho in probs]
            k = [cat(kp_ref[rows(0, rho), :] if run == 0 else kc_ref[rows(run - 1, rho), :],
                     kc_ref[rows(run, rho), :]) for run, rho in probs]
            v = [cat(vp_ref[rows(0, rho), :] if run == 0 else vc_ref[rows(run - 1, rho), :],
                     vc_ref[rows(run, rho), :]) for run, rho in probs]
            s = [lax.dot_general(a, b, (((1,), (1,)), ((), ())), preferred_element_type=F32) * scale
                 for a, b in zip(q, k)]
            s = [jnp.where(band_first if run == 0 else band, x, NEG) for x, (run, _) in zip(s, probs)]
            m = [jnp.max(x, axis=-1, keepdims=True) for x in s]
            p = [jnp.exp(x - y) for x, y in zip(s, m)]
            l = [jnp.sum(x, axis=-1, keepdims=True) for x in p]
            o = [jnp.dot(x.astype(BF16), y, preferred_element_type=F32) for x, y in zip(p, v)]
            for (run, rho), oo, ll, mx in zip(probs, o, l, m):
                o_sc[g][rows(run, rho), :] = oo / ll
                l_sc[g][rows(run, rho), :] = jnp.broadcast_to(mx + jnp.log(ll), (SPAN, A_HD))

    lse = [ref[...] for ref in l_sc]
    top = functools.reduce(jnp.maximum, lse)
    e = [jnp.exp(x - top) for x in lse]
    den = functools.reduce(jnp.add, e)
    o_ref[...] = functools.reduce(jnp.add, [(w / den) * ref[...] for w, ref in zip(e, o_sc)])


def _attn_prompt(h, bsz, t):
    nblk = t // ATTN_ROWS
    in_specs = []
    for g, (_, dil) in enumerate(A_GROUPS):
        prev_rows = SPAN * dil
        per_blk = ATTN_ROWS // prev_rows
        col = lambda which, hd, g=g: (_col_q, _col_k, _col_v)[which](g) // A_HD + hd
        cur = lambda which, col=col: pl.BlockSpec(
            (ATTN_ROWS, A_HD), lambda b, n, hd: (b * nblk + n, col(which, hd)))
        prev = lambda which, col=col, per_blk=per_blk, prev_rows=prev_rows: pl.BlockSpec(
            (prev_rows, A_HD),
            lambda b, n, hd: (jnp.maximum((b * nblk + n) * per_blk - 1, 0), col(which, hd)))
        in_specs += [cur(0), prev(1), cur(1), prev(2), cur(2)]
    return pl.pallas_call(
        _attn_prompt_kernel,
        grid=(bsz, nblk, A_HEADS),
        in_specs=in_specs,
        out_specs=pl.BlockSpec((ATTN_ROWS, A_HD), lambda b, n, hd: (b * nblk + n, hd)),
        out_shape=jax.ShapeDtypeStruct((bsz * t, A_WIDTH), F32),
        scratch_shapes=[pltpu.VMEM((ATTN_ROWS, A_HD), F32)] * (2 * N_GROUPS),
        compiler_params=_params("parallel", "arbitrary", "arbitrary"),
        name="attn_prompt",
    )(*([h] * (5 * N_GROUPS)))


SAMPLE_ROWS = 8


def _attn_sample_kernel(*refs, t_new):
    q_refs, k_refs, v_refs = refs[0:3], refs[3:6], refs[6:9]
    c_refs, o_ref = refs[9:12], refs[12]
    scale = 1.0 / math.sqrt(A_HD)
    kvh_n = 2 * A_HEADS
    n_seq = SAMPLE_ROWS // t_new
    n_q = A_HEADS * SAMPLE_ROWS
    n_c = SPAN * kvh_n
    n_l = n_c + LANES
    row = lax.broadcasted_iota(jnp.int32, (n_q, n_l), 0)
    lane = lax.broadcasted_iota(jnp.int32, (n_q, n_l), 1)
    h_r, seq_r, t_r = row // SAMPLE_ROWS, (row % SAMPLE_ROWS) // t_new, row % t_new
    in_buf = lane < n_c
    j_l = lane // kvh_n
    new = lane - n_c
    t_l = new % t_new
    own = (in_buf & ((lane % kvh_n) == h_r)) | (
        ~in_buf & ((new // SAMPLE_ROWS) == h_r) & (((new % SAMPLE_ROWS) // t_new) == seq_r))
    n_pad = LANES - kvh_n * SAMPLE_ROWS
    pad = [jnp.zeros((n_pad, A_HD), F32)] if n_pad else []
    heads = lambda ref: [ref[:, hd * A_HD:(hd + 1) * A_HD] for hd in range(A_HEADS)]
    outs, lses = [], []
    for g in range(N_GROUPS):
        q = jnp.concatenate(heads(q_refs[g]), axis=0).astype(BF16)
        kv_new = jnp.concatenate(heads(k_refs[g]) + heads(v_refs[g]) + pad, axis=0)
        kv2, s = [], []
        for seq in range(n_seq):
            for t in ([None] if g == 0 else range(t_new)):
                buf = c_refs[g][seq, :, 0 if t is None else t].reshape(n_c, A_HD)
                kv2.append(jnp.concatenate([buf, kv_new], axis=0).astype(BF16))
                if t is None:
                    valid = own & (seq_r == seq) & ((in_buf & (j_l >= t_r)) | (~in_buf & (t_l <= t_r)))
                else:
                    valid = own & (seq_r == seq) & (t_r == t) & (in_buf | (t_l == t))
                st = lax.dot_general(q, kv2[-1], (((1,), (1,)), ((), ())), preferred_element_type=F32)
                s.append(jnp.where(valid, st * scale, NEG))
        m = functools.reduce(jnp.maximum, [jnp.max(x, axis=-1, keepdims=True) for x in s])
        p = [jnp.exp(x - m) for x in s]
        l = functools.reduce(jnp.add, [jnp.sum(x, axis=-1, keepdims=True) for x in p])
        shift = lambda x: jnp.concatenate(
            [pltpu.roll(x[:, 0:n_c], A_HEADS, 1), pltpu.roll(x[:, n_c:n_l], A_HEADS * SAMPLE_ROWS, 1)],
            axis=1)
        o = functools.reduce(jnp.add, [
            jnp.dot(shift(x).astype(BF16), y, preferred_element_type=F32) for x, y in zip(p, kv2)])
        outs.append(o / l)
        lses.append(m + jnp.log(l))
    top = functools.reduce(jnp.maximum, lses)
    e = [jnp.exp(x - top) for x in lses]
    den = functools.reduce(jnp.add, e)
    o = functools.reduce(jnp.add, [(w / den) * x for w, x in zip(e, outs)])
    for hd in range(A_HEADS):
        o_ref[:, hd * A_HD:(hd + 1) * A_HD] = o[hd * SAMPLE_ROWS:(hd + 1) * SAMPLE_ROWS]


def _attn_sample(h, caches, bsz, t_new):
    kvh_n = 2 * A_HEADS
    n_seq = SAMPLE_ROWS // t_new
    assert SAMPLE_ROWS % t_new == 0 and bsz % n_seq == 0
    views = []
    specs = []
    for (win, dil), cache in zip(A_GROUPS, caches):
        assert cache.shape[1] == win and win // dil == SPAN and (dil == 1 or t_new <= dil)
        views.append(cache.reshape(bsz, SPAN, dil, kvh_n, A_HD))
        specs.append(pl.BlockSpec((n_seq, SPAN, min(dil, t_new), kvh_n, A_HD), lambda i: (i, 0, 0, 0, 0)))
    col = lambda which, g: pl.BlockSpec(
        (SAMPLE_ROWS, A_WIDTH), lambda i: (i, (_col_q, _col_k, _col_v)[which](g) // A_WIDTH))
    qkv_specs = [col(which, g) for which in range(3) for g in range(N_GROUPS)]
    return pl.pallas_call(
        functools.partial(_attn_sample_kernel, t_new=t_new),
        grid=(bsz // n_seq,),
        in_specs=qkv_specs + specs,
        out_specs=pl.BlockSpec((SAMPLE_ROWS, A_WIDTH), lambda i: (i, 0)),
        out_shape=jax.ShapeDtypeStruct((bsz * t_new, A_WIDTH), F32),
        compiler_params=_params("parallel"),
        name="attn_sample",
    )(*([h] * (3 * N_GROUPS)), *views)


def _final_kernel(x_ref, o_ref, bon_ref, zr_ref, oa_ref, zgg_ref, seg_ref, lnxg_ref, lnxb_ref, bg_ref,
                  woa_ref, wob_ref, wout_ref, lng_ref, lnb_ref, y_ref):
    seg2 = seg_ref[...]
    o = o_ref[...]
    mu = _seg_sum(o, seg2) * (1.0 / R_HD)
    d = o - mu
    var = _seg_sum(d * d, seg2) * (1.0 / R_HD)
    on = d * lax.rsqrt(var + GN_EPS) * lnxg_ref[...] + lnxb_ref[...]
    zr = zr_ref[...].astype(F32)
    y_r = (on + bon_ref[...].astype(F32)) * (zr * jax.nn.sigmoid(zr))

    z_a = zgg_ref[:, 0:A_WIDTH].astype(F32)
    y_a = oa_ref[...] * (z_a * jax.nn.sigmoid(z_a))

    g_r = zgg_ref[:, A_WIDTH:A_WIDTH + D_MODEL].astype(F32)
    g_a = zgg_ref[:, A_WIDTH + D_MODEL:A_WIDTH + 2 * D_MODEL].astype(F32)
    gate_r = jax.nn.sigmoid(g_r + bg_ref[:, 0:D_MODEL])
    gate_a = jax.nn.sigmoid(g_a + bg_ref[:, D_MODEL:2 * D_MODEL])
    mix = (gate_r * jnp.dot(y_r.astype(BF16), woa_ref[...], preferred_element_type=F32)
           + gate_a * jnp.dot(y_a.astype(BF16), wob_ref[...], preferred_element_type=F32))
    yy = ALPHA * x_ref[...] + jnp.dot(mix.astype(BF16), wout_ref[...], preferred_element_type=F32)
    mean = jnp.mean(yy, axis=-1, keepdims=True)
    cen = yy - mean
    variance = jnp.mean(cen * cen, axis=-1, keepdims=True)
    y_ref[...] = cen * lax.rsqrt(variance + LN_EPS) * lng_ref[...] + lnb_ref[...]


def _final(x, o, bonus, h16, o_attn, weights, tm):
    m = x.shape[0]
    row = lambda w, j=0: pl.BlockSpec((tm, w), lambda i: (i, j))
    const = lambda a: pl.BlockSpec(a.shape, lambda i: (0, 0))
    in_specs = ([row(D_MODEL), row(R_WIDTH), row(R_WIDTH), row(R_WIDTH, COL_ZR // R_WIDTH), row(A_WIDTH),
                 row(ZGG_W, COL_ZGG // ZGG_W)] + [const(a) for a in weights])
    return pl.pallas_call(
        _final_kernel,
        grid=(m // tm,),
        in_specs=in_specs,
        out_specs=row(D_MODEL),
        out_shape=jax.ShapeDtypeStruct((m, D_MODEL), F32),
        compiler_params=_params("parallel"),
        name="merge_out_ln",
    )(x, o, bonus, h16, o_attn, h16, *weights)


def _layer(x2, bsz, t, shift_prev, wkv0, caches, wts):
    m = x2.shape[0]
    h, h16 = _in_proj(x2, wts["w_proj"], 2048 if m % 2048 == 0 else m)

    tm = 512 if m % 512 == 0 else m
    if caches is None:
        first = None
    else:
        first = jnp.zeros((bsz, t, SHIFT_COLS), F32).at[:, 0].set(shift_prev).reshape(m, SHIFT_COLS)
    r, kp, v, kk, b, lw, bonus = _rwkv_prep(h, first, t, wts["prep"], tm)

    if t % CHUNK == 0 and wkv0 is None:
        seq = lambda a: a.reshape(bsz, t, R_WIDTH)
        s0 = jnp.zeros((bsz, N_PAIRS, LANES, LANES), F32)
        o_seq, st = _rwkv_recurrence(seq(r), seq(kp), seq(v), seq(kk), seq(b), seq(lw), s0, 2)
        o_rwkv = o_seq.reshape(m, R_WIDTH)
        wkv_new = _pairs_to_state(st)
    else:
        o_rwkv, wkv_new = _rwkv_short(r, kp, v, kk, b, lw, wkv0, t)

    def rows_cols(r0, c0, width):
        if t % 8 == 0:
            return h.reshape(bsz, t, N_PROJ32)[:, r0:, c0:c0 + width]
        return h[:, c0:c0 + width].reshape(bsz, t, width)[:, r0:]

    if caches is None:
        o_attn = _attn_prompt(h, bsz, t)
    else:
        o_attn = _attn_sample(h, caches, bsz, t)

    tmf = 512 if m % 512 == 0 else m
    y = _final(x2, o_rwkv, bonus, h16, o_attn, wts["final"], tmf)

    new_kv = []
    for g, (win, _) in enumerate(A_GROUPS):
        keep = min(win, t) if caches is None else t
        new_kv.append(rows_cols(t - keep, _col_k(g), 2 * A_WIDTH).reshape(bsz, keep, 2, A_HEADS, A_HD))
    shift_new = rows_cols(t - 1, COL_ZS, SHIFT_COLS)[:, 0]
    return y.reshape(bsz, t, D_MODEL), new_kv, wkv_new, shift_new


def kernel(x_prompt, x_sample, cache_kv_g1, cache_kv_g2, cache_kv_g3, state_rwkv_wkv, state_rwkv_shift,
           w_in, b_gate, mu_shift, w0, w_w2, a0, w_a2, k_k, k_a, r_k, lnx_g, lnx_b,
           w_oa, w_ob, w_out, ln_g, ln_b):
    assert w_in.shape[0] == DEPTH
    bp, tp, _ = x_prompt.shape
    bs, ts, _ = x_sample.shape
    xp = x_prompt.reshape(bp * tp, D_MODEL)
    xs = x_sample.reshape(bs * ts, D_MODEL)
    head = lax.broadcasted_iota(jnp.int32, (LANES, LANES), 0) // R_HD
    seg = (head == head.T).astype(BF16)
    seg2 = jnp.concatenate([seg, seg], axis=0)
    row = lambda a: a.reshape(1, -1)
    acc = [[] for _ in range(12)]
    for l in range(DEPTH):
        wb = w_in[l].astype(BF16)
        c0 = SHIFT_COLS
        c1 = c0 + R_WIDTH
        c2 = c1 + 3 * A_QKV_W
        qkv_col = lambda which, g: wb[:, c1 + (which * N_GROUPS + g) * A_WIDTH:
                                      c1 + (which * N_GROUPS + g + 1) * A_WIDTH]
        w_proj = jnp.concatenate(
            [wb[:, :c0], jnp.zeros((D_MODEL, COL_Q - c0), BF16)] + [qkv_col(0, g) for g in range(N_GROUPS)]
            + [qkv_col(which, g) for g in range(N_GROUPS) for which in (1, 2)]
            + [wb[:, c2:], wb[:, c0:c1]], axis=1)
        zero = jnp.zeros((LORA, R_WIDTH), F32)
        w_lora = jnp.concatenate([jnp.concatenate([w_w2[l], zero], axis=1),
                                  jnp.concatenate([zero, w_a2[l]], axis=1)], axis=0)
        wl_hi = w_lora.astype(BF16)
        wl_lo = (w_lora - wl_hi.astype(F32)).astype(BF16)
        wts = {
            "w_proj": w_proj,
            "prep": (row(mu_shift[l]), row(w0[l]), row(a0[l]), jnp.concatenate([wl_hi, wl_hi], axis=0),
                     wl_lo, row(k_k[l]), row(k_a[l]), row(r_k[l]), seg2),
            "final": (seg2, row(lnx_g[l]), row(lnx_b[l]), row(b_gate[l]), w_oa[l].astype(BF16),
                      w_ob[l].astype(BF16), w_out[l].astype(BF16), row(ln_g[l]), row(ln_b[l])),
        }
        yp, kv_p, s_p, sh_p = _layer(xp, bp, tp, None, None, None, wts)
        ys, kv_s, s_s, sh_s = _layer(xs, bs, ts, state_rwkv_shift[l], state_rwkv_wkv[l],
                                     (cache_kv_g1[l], cache_kv_g2[l], cache_kv_g3[l]), wts)
        xp = yp.reshape(bp * tp, D_MODEL)
        xs = ys.reshape(bs * ts, D_MODEL)
        for idx, val in zip(range(2, 12), (kv_p[0], kv_s[0], kv_p[1], kv_s[1], kv_p[2], kv_s[2],
                                           s_p, s_s, sh_p, sh_s)):
            acc[idx].append(val)
    outs = [xp.reshape(bp, tp, D_MODEL), xs.reshape(bs, ts, D_MODEL)]
    outs += [jnp.stack(a) for a in acc[2:]]
    return tuple(outs)
```

```python
import functools
import math

import jax
import jax.numpy as jnp
from jax import lax
from jax.experimental import pallas as pl
from jax.experimental.pallas import tpu as pltpu

F32 = jnp.float32
BF16 = jnp.bfloat16

D_MODEL = 1024
R_HEADS = 8
R_HD = 64
R_WIDTH = R_HEADS * R_HD
LORA = 64
SHIFT_COLS = 3 * R_WIDTH + 2 * LORA
GN_EPS = 64e-5
A_GROUPS = ((128, 1), (512, 4), (2048, 16))
N_GROUPS = 3
A_HEADS = 4
A_HD = 128
A_WIDTH = A_HEADS * A_HD
A_QKV_W = N_GROUPS * A_WIDTH
SPAN = 128
DEPTH = 1
ALPHA = (2 * DEPTH) ** 0.25
LN_EPS = 1e-5
NEG = -1e30

LANES = 128
CHUNK = 64
N_PAIRS = R_HEADS // 2
VMEM_LIMIT = 48 * 1024 * 1024

PROJ_TN = 512
COL_ZS = 0
COL_Q = 2048
COL_KV = COL_Q + A_QKV_W
N_PROJ32 = COL_KV + 2 * A_QKV_W
ZGG_W = A_WIDTH + 2 * D_MODEL
COL_ZGG = 0
COL_ZR = ZGG_W
N_PROJ16 = COL_ZR + R_WIDTH
assert COL_Q % A_WIDTH == 0 and COL_KV % A_WIDTH == 0 and COL_ZR % R_WIDTH == 0
assert N_PROJ32 % PROJ_TN == 0 and N_PROJ16 % PROJ_TN == 0


def _col_q(g):
    return COL_Q + g * A_WIDTH


def _col_k(g):
    return COL_KV + g * 2 * A_WIDTH


def _col_v(g):
    return _col_k(g) + A_WIDTH


def _params(*sem):
    return pltpu.CompilerParams(dimension_semantics=sem, vmem_limit_bytes=VMEM_LIMIT)


def _split2(x):
    hi = x.astype(BF16)
    return hi, (x - hi.astype(F32)).astype(BF16)


def _seg_sum(x, seg2):
    cols = []
    for c in range(x.shape[1] // LANES):
        hi, lo = _split2(x[:, c * LANES:(c + 1) * LANES])
        cols.append(jnp.dot(jnp.concatenate([hi, lo], axis=1), seg2, preferred_element_type=F32))
    return jnp.concatenate(cols, axis=1)


def _mm_kernel(x_ref, w_ref, o32_ref, o16_ref, *, n32):
    j = pl.program_id(1)
    product = lambda: jnp.dot(x_ref[...].astype(BF16), w_ref[...], preferred_element_type=F32)

    @pl.when(j < n32)
    def _():
        o32_ref[...] = product()

    @pl.when(j >= n32)
    def _():
        o16_ref[...] = product().astype(BF16)


def _in_proj(x, w, tm):
    m, k = x.shape
    n32 = N_PROJ32 // PROJ_TN
    n16 = N_PROJ16 // PROJ_TN
    return pl.pallas_call(
        functools.partial(_mm_kernel, n32=n32),
        grid=(m // tm, n32 + n16),
        in_specs=[pl.BlockSpec((tm, k), lambda i, j: (i, 0)),
                  pl.BlockSpec((k, PROJ_TN), lambda i, j: (0, j))],
        out_specs=[pl.BlockSpec((tm, PROJ_TN), lambda i, j: (i, jnp.minimum(j, n32 - 1))),
                   pl.BlockSpec((tm, PROJ_TN), lambda i, j: (i, jnp.maximum(j - n32, 0)))],
        out_shape=[jax.ShapeDtypeStruct((m, N_PROJ32), F32), jax.ShapeDtypeStruct((m, N_PROJ16), BF16)],
        compiler_params=_params("parallel", "arbitrary"),
        name="in_proj",
    )(x, w)


def _prep_math(zs, prev, mu, w0, a0, wl_a, wl_b, k_k, k_a, r_k, seg2, outs):
    r_o, kp_o, v_o, kk_o, b_o, lw_o, bon_o = outs
    zsm = zs + mu * (prev - zs)
    r = zsm[:, 0:R_WIDTH]
    k = zsm[:, R_WIDTH:2 * R_WIDTH]
    v = zsm[:, 2 * R_WIDTH:3 * R_WIDTH]
    slab = zsm[:, 3 * R_WIDTH:SHIFT_COLS]
    lane = lax.broadcasted_iota(jnp.int32, slab.shape, 1)
    slab = jnp.where(lane < LORA, jnp.tanh(slab), slab)
    s_hi, s_lo = _split2(slab)
    lin = (jnp.dot(jnp.concatenate([s_hi, s_lo], axis=1), wl_a, preferred_element_type=F32)
           + jnp.dot(s_hi, wl_b, preferred_element_type=F32))
    wlin = w0 + lin[:, 0:R_WIDTH]
    alin = a0 + lin[:, R_WIDTH:2 * R_WIDTH]
    nw = -wlin
    softplus = jnp.maximum(nw, 0.0) + jnp.log1p(jnp.exp(-jnp.abs(nw)))
    w = -softplus - 0.5
    lw = -jnp.exp(w)
    a = jax.nn.sigmoid(alin)
    kkr = k * k_k
    kk = kkr / jnp.maximum(jnp.sqrt(_seg_sum(kkr * kkr, seg2)), 1e-12)
    kp = k * (1.0 + (a - 1.0) * k_a)
    r_o[...] = r.astype(r_o.dtype)
    kp_o[...] = kp.astype(kp_o.dtype)
    v_o[...] = v.astype(v_o.dtype)
    kk_o[...] = kk.astype(kk_o.dtype)
    b_o[...] = (kk * a).astype(b_o.dtype)
    lw_o[...] = lw
    bon_o[...] = (_seg_sum(r * kp * r_k, seg2) * v).astype(bon_o.dtype)


def _prep_prompt_kernel(zs_ref, p8_ref, mu_ref, w0_ref, a0_ref, wla_ref, wlb_ref, kk_ref, ka_ref,
                        rk_ref, seg_ref, *outs, tiles_per_seq):
    i = pl.program_id(0)
    zs = zs_ref[...]
    rolled = pltpu.roll(zs, 1, 0)
    carry = jnp.where(i % tiles_per_seq == 0, 0.0, p8_ref[7:8, :])
    row = lax.broadcasted_iota(jnp.int32, zs.shape, 0)
    prev = jnp.where(row == 0, carry, rolled)
    _prep_math(zs, prev, mu_ref[...], w0_ref[...], a0_ref[...], wla_ref[...], wlb_ref[...],
               kk_ref[...], ka_ref[...], rk_ref[...], seg_ref[...], outs)


def _prep_sample_kernel(zs_ref, first_ref, mu_ref, w0_ref, a0_ref, wla_ref, wlb_ref, kk_ref, ka_ref,
                        rk_ref, seg_ref, *outs, seq_len):
    zs = zs_ref[...]
    rolled = pltpu.roll(zs, 1, 0)
    row = lax.broadcasted_iota(jnp.int32, zs.shape, 0)
    prev = jnp.where(row % seq_len == 0, first_ref[...], rolled)
    _prep_math(zs, prev, mu_ref[...], w0_ref[...], a0_ref[...], wla_ref[...], wlb_ref[...],
               kk_ref[...], ka_ref[...], rk_ref[...], seg_ref[...], outs)


def _rwkv_prep(h, first, seq_len, weights, tm):
    m = h.shape[0]
    const = lambda a: pl.BlockSpec(a.shape, lambda i: (0, 0))
    row_spec = pl.BlockSpec((tm, SHIFT_COLS), lambda i: (i, COL_ZS))
    if first is None:
        kern = functools.partial(_prep_prompt_kernel, tiles_per_seq=seq_len // tm)
        second = h
        second_spec = pl.BlockSpec((8, SHIFT_COLS), lambda i: (jnp.maximum(i * (tm // 8) - 1, 0), COL_ZS))
    else:
        kern = functools.partial(_prep_sample_kernel, seq_len=seq_len)
        second = first
        second_spec = pl.BlockSpec((tm, SHIFT_COLS), lambda i: (i, 0))
    out_spec = pl.BlockSpec((tm, R_WIDTH), lambda i: (i, 0))
    return pl.pallas_call(
        kern,
        grid=(m // tm,),
        in_specs=[row_spec, second_spec] + [const(a) for a in weights],
        out_specs=[out_spec] * 7,
        out_shape=[jax.ShapeDtypeStruct((m, R_WIDTH), dt) for dt in (BF16,) * 5 + (F32, BF16)],
        compiler_params=_params("parallel"),
        name="rwkv_prep",
    )(h, second, *weights)


CHUNKS_PER_STEP = 2


def _rwkv_chunk_kernel(*refs, nb):
    ins, s0_ref, o_ref, st_ref = refs[0:18], refs[18], refs[19], refs[20]
    yo_sc, ti_sc, mb_sc, ut_sc, vs_sc, dec_sc = refs[21:27]
    n = pl.program_id(1)
    c = CHUNK
    c2 = 2 * c
    lane = lax.broadcasted_iota(jnp.int32, (1, LANES), 1)
    m_lo = (lane < R_HD).astype(F32)
    m_hi = 1.0 - m_lo
    row_c = lax.broadcasted_iota(jnp.int32, (c, 4 * c), 0)
    col_c = lax.broadcasted_iota(jnp.int32, (c, 4 * c), 1)
    tri3 = ((row_c >= col_c % c) & (col_c < 3 * c)).astype(BF16)
    row2 = lax.broadcasted_iota(jnp.int32, (c2, c2), 0)
    col2 = lax.broadcasted_iota(jnp.int32, (c2, c2), 1)
    strict = row2 > col2
    incl = row2 >= col2
    eye = (row2 == col2).astype(F32)

    def stack(x):
        return jnp.concatenate([x * m_lo, x * m_hi], axis=0)

    def mm(a, b):
        return jnp.dot(a.astype(BF16), b.astype(BF16), preferred_element_type=F32)

    def nt(a, b):
        return lax.dot_general(a.astype(BF16), b.astype(BF16), (((1,), (1,)), ((), ())),
                               preferred_element_type=F32)

    each = lambda f, *xs: [f(*a) for a in zip(*xs)]
    chains = [(bi, slice(p * LANES, (p + 1) * LANES), p) for bi in range(nb) for p in range(N_PAIRS)]

    def cumsum(lw_all):
        hi = lw_all.astype(BF16)
        r1 = lw_all - hi.astype(F32)
        mid = r1.astype(BF16)
        lo = (r1 - mid.astype(F32)).astype(BF16)
        return jnp.dot(tri3, jnp.concatenate([hi, mid, lo, jnp.zeros_like(lo)], axis=0),
                       preferred_element_type=F32)

    def prepare(refs, row0, slot):
        lw_ref = refs[5]
        rs = slice(row0, row0 + c)
        g_all = [cumsum(lw_ref[bi, rs, :]) for bi in range(nb)]
        yield
        load = lambda ref: [ref[bi, rs, sl].astype(F32) for bi, sl, _ in chains]
        r, kp, v, kk, b, lw = (load(ref) for ref in refs)
        g = [g_all[bi][:, sl] for bi, sl, _ in chains]
        g_last = each(lambda x: x[c - 1:c, :], g)
        e_g = each(jnp.exp, g)
        e_ng = each(lambda x: jnp.exp(-x), g)
        e_gm = each(lambda x, y: jnp.exp(x - y), g, lw)
        e_cg = each(lambda x, y: jnp.exp(y - x), g, g_last)
        ad_s = each(lambda x, y: stack(-(x * y)), kk, e_gm)
        rd_s = each(lambda x, y: stack(x * y), r, e_g)
        bd_s = each(lambda x, y: stack(x * y), b, e_ng)
        kd_s = each(lambda x, y: stack(x * y), kp, e_ng)
        be_s = each(lambda x, y: stack(x * y), b, e_cg)
        ke_s = each(lambda x, y: stack(x * y), kp, e_cg)
        for idx in range(len(chains)):
            yo_sc[slot, idx, 0:c2, 0:c2] = ad_s[idx].astype(BF16)
            yo_sc[slot, idx, c2:2 * c2, 0:c2] = rd_s[idx].astype(BF16)
            ut_sc[slot, idx, :, 0:c2] = jnp.transpose(be_s[idx]).astype(BF16)
            ut_sc[slot, idx, :, c2:2 * c2] = jnp.transpose(ke_s[idx]).astype(BF16)
            vs_sc[slot, idx] = stack(v[idx]).astype(BF16)
            dec_sc[slot, idx] = jnp.transpose(jnp.broadcast_to(jnp.exp(g_last[idx]), (LANES, LANES)))
        yield
        z = each(lambda a1, a2, b1, b2: nt(jnp.concatenate([a1, a2], axis=0),
                                           jnp.concatenate([b1, b2], axis=0)), ad_s, rd_s, bd_s, kd_s)
        yield
        for idx, zz in enumerate(z):
            yo_sc[slot, idx, 0:c2, c2:2 * c2] = jnp.where(strict, zz[0:c2, c2:2 * c2], 0.0).astype(BF16)
            yo_sc[slot, idx, c2:2 * c2, c2:2 * c2] = jnp.where(
                incl, zz[c2:2 * c2, c2:2 * c2], 0.0).astype(BF16)
            mb_sc[slot, idx] = jnp.where(incl, zz[c2:2 * c2, 0:c2], 0.0).astype(BF16)
        n_b = each(lambda x: jnp.where(strict, x[0:c2, 0:c2], 0.0), z)
        t_inv = each(lambda x: eye + x, n_b)
        q = each(lambda x: mm(x, x), n_b)
        yield
        steps = int(math.log2(c))
        for i in range(1, steps):
            if i < steps - 1:
                res = each(lambda x, tt: mm(x, jnp.concatenate([x, tt], axis=1)), q, t_inv)
                q = each(lambda x: x[:, 0:c2], res)
                t_inv = each(lambda tt, x: tt + x[:, c2:2 * c2], t_inv, res)
            else:
                t_inv = each(lambda tt, x: tt + mm(x, tt), t_inv, q)
            yield
        for idx, tt in enumerate(t_inv):
            ti_sc[slot, idx] = tt.astype(BF16)

    def advance(slot, row0):
        idxs = list(range(len(chains)))
        st = [st_ref[bi, p] for bi, _, p in chains]
        vs = [vs_sc[slot, i] for i in idxs]
        yo = [jnp.dot(yo_sc[slot, i], jnp.concatenate([s.astype(BF16), w], axis=0),
                      preferred_element_type=F32) for i, s, w in zip(idxs, st, vs)]
        yield
        u = [jnp.dot(ti_sc[slot, i], x[0:c2].astype(BF16), preferred_element_type=F32)
             for i, x in zip(idxs, yo)]
        ub = [x.astype(BF16) for x in u]
        yield
        o_bd = [x[c2:2 * c2] + jnp.dot(mb_sc[slot, i], w, preferred_element_type=F32)
                for i, x, w in zip(idxs, yo, ub)]
        for (bi, sl, _), o in zip(chains, o_bd):
            o_ref[bi, row0:row0 + c, sl] = o[0:c] + o[c:c2]
        yield
        upd = [jnp.dot(ut_sc[slot, i], jnp.concatenate([w, x], axis=0), preferred_element_type=F32)
               for i, w, x in zip(idxs, ub, vs)]
        for i, ((bi, _, p), s, d) in enumerate(zip(chains, st, upd)):
            st_ref[bi, p] = dec_sc[slot, i] * s + d

    def interleave(*stages):
        live = list(stages)
        while live:
            for gen in list(live):
                if next(gen, StopIteration) is StopIteration:
                    live.remove(gen)

    @pl.when(n == 0)
    def _():
        st_ref[...] = s0_ref[...]
        interleave(prepare(ins[0:6], 0, 0))

    interleave(advance(0, 0), prepare(ins[6:12], 0, 1))
    interleave(advance(1, c), prepare(ins[12:18], 0, 0))


def _rwkv_recurrence(r, kp, v, kk, b, lw, s0, nb):
    bsz, t, _ = r.shape
    assert CHUNKS_PER_STEP == 2
    rows = CHUNKS_PER_STEP * CHUNK
    nsteps = t // rows
    last = t // CHUNK - 1
    chunk = lambda which: pl.BlockSpec((nb, CHUNK, R_WIDTH), lambda i, n: (i, which(n), 0))
    chunk_specs = ([chunk(lambda n: 0)] * 6 + [chunk(lambda n: 2 * n + 1)] * 6
                   + [chunk(lambda n: jnp.minimum(2 * n + 2, last))] * 6)
    seq_spec = pl.BlockSpec((nb, rows, R_WIDTH), lambda i, n: (i, n, 0))
    st_spec = pl.BlockSpec((nb, N_PAIRS, LANES, LANES), lambda i, n: (i, 0, 0, 0))
    nchain = nb * N_PAIRS
    c2 = 2 * CHUNK
    slots = 2
    scratch = [pltpu.VMEM((slots, nchain, 2 * c2, 2 * c2), BF16),
               pltpu.VMEM((slots, nchain, c2, c2), BF16),
               pltpu.VMEM((slots, nchain, c2, c2), BF16),
               pltpu.VMEM((slots, nchain, c2, 2 * c2), BF16),
               pltpu.VMEM((slots, nchain, c2, c2), BF16),
               pltpu.VMEM((slots, nchain, LANES, LANES), F32)]
    return pl.pallas_call(
        functools.partial(_rwkv_chunk_kernel, nb=nb),
        grid=(bsz // nb, nsteps),
        in_specs=chunk_specs + [st_spec],
        out_specs=[seq_spec, st_spec],
        out_shape=[jax.ShapeDtypeStruct((bsz, t, R_WIDTH), F32),
                   jax.ShapeDtypeStruct(s0.shape, F32)],
        scratch_shapes=scratch,
        compiler_params=_params("parallel", "arbitrary"),
        name="rwkv_chunks",
    )(*([r, kp, v, kk, b, lw] * 3), s0)


def _pairs_to_state(st):
    bsz = st.shape[0]
    s6 = st.reshape(bsz, N_PAIRS, 2, R_HD, 2, R_HD)
    s = jnp.stack([s6[:, :, 0, :, 0, :], s6[:, :, 1, :, 1, :]], axis=2)
    return jnp.swapaxes(s.reshape(bsz, R_HEADS, R_HD, R_HD), -1, -2)


SHORT_ROWS = 64


def _rwkv_short_kernel(r_ref, kp_ref, v_ref, kk_ref, b_ref, lw_ref, s_ref, o_ref, sn_ref, *, t_len):
    rows = SHORT_ROWS
    ng = rows // t_len
    r2 = 2 * rows
    assert 2 * t_len == 8 and r2 == LANES
    lane = lax.broadcasted_iota(jnp.int32, (1, LANES), 1)
    m_lo = (lane < R_HD).astype(F32)
    m_hi = 1.0 - m_lo
    lane_blk = lane // t_len
    row_c = lax.broadcasted_iota(jnp.int32, (rows, 4 * rows), 0)
    col_c = lax.broadcasted_iota(jnp.int32, (rows, 4 * rows), 1)
    j_c = col_c % rows
    same_c = ((row_c // t_len) == (j_c // t_len)) & (col_c < 3 * rows)
    cum_lhs = jnp.concatenate([(same_c & (row_c >= j_c)).astype(BF16), same_c.astype(BF16)], axis=0)
    row2 = lax.broadcasted_iota(jnp.int32, (r2, r2), 0)
    col2 = lax.broadcasted_iota(jnp.int32, (r2, r2), 1)
    blk = (row2 // t_len) == (col2 // t_len)
    strict = blk & (row2 > col2)
    incl = blk & (row2 >= col2)
    first_seq = lax.broadcasted_iota(jnp.int32, (8, LANES), 0) < t_len
    zero_half = jnp.zeros((R_HD, R_HD), F32)

    def stack(x):
        return jnp.concatenate([x * m_lo, x * m_hi], axis=0)

    def half(x):
        return jnp.concatenate([x * m_lo, pltpu.roll(x, R_HD, 1) * m_lo], axis=0)

    def mm(a, b):
        return jnp.dot(a.astype(BF16), b.astype(BF16), preferred_element_type=F32)

    def nt(a, b):
        return lax.dot_general(a.astype(BF16), b.astype(BF16), (((1,), (1,)), ((), ())),
                               preferred_element_type=F32)

    each = lambda f, *xs: [f(*a) for a in zip(*xs)]

    lw_all = lw_ref[...]
    hi = lw_all.astype(BF16)
    r1 = lw_all - hi.astype(F32)
    mid = r1.astype(BF16)
    lo = (r1 - mid.astype(F32)).astype(BF16)
    g_both = jnp.dot(cum_lhs, jnp.concatenate([hi, mid, lo, jnp.zeros_like(lo)], axis=0),
                     preferred_element_type=F32)

    pairs = list(range(N_PAIRS))
    sls = [slice(p * LANES, (p + 1) * LANES) for p in pairs]
    load = lambda ref: [ref[:, sl].astype(F32) for sl in sls]
    r, kp, v, kk, b, lw = (load(ref) for ref in (r_ref, kp_ref, v_ref, kk_ref, b_ref, lw_ref))
    g = [g_both[0:rows, sl] for sl in sls]
    g_tot = [g_both[rows:r2, sl] for sl in sls]
    e_g = each(jnp.exp, g)
    e_ng = each(lambda x: jnp.exp(-x), g)
    e_gm = each(lambda x, y: jnp.exp(x - y), g, lw)
    e_cg = each(lambda x, y: jnp.exp(y - x), g, g_tot)
    ad = each(lambda x, y: -(x * y), kk, e_gm)
    rd = each(lambda x, y: x * y, r, e_g)
    bd = each(lambda x, y: x * y, b, e_ng)
    kd = each(lambda x, y: x * y, kp, e_ng)
    be_h = each(lambda x, y: half(x * y), b, e_cg)
    ke_h = each(lambda x, y: half(x * y), kp, e_cg)
    v_h = each(half, v)
    ad_h = each(half, ad)
    rd_h = each(half, rd)
    dec_h = each(lambda x: half(jnp.exp(x)), g_tot)

    z = each(lambda a1, a2, b1, b2: nt(jnp.concatenate([stack(a1), stack(a2)], axis=0),
                                       jnp.concatenate([stack(b1), stack(b2)], axis=0)), ad, rd, bd, kd)
    n_b = each(lambda x: jnp.where(strict, x[0:r2, 0:r2], 0.0), z)
    n_k = each(lambda x: jnp.where(strict, x[0:r2, r2:2 * r2], 0.0), z)
    m_b = each(lambda x: jnp.where(incl, x[r2:2 * r2, 0:r2], 0.0), z)
    m_k = each(lambda x: jnp.where(incl, x[r2:2 * r2, r2:2 * r2], 0.0), z)

    y0, o0 = [], []
    for p in pairs:
        y_tiles, o_tiles = [], []
        for hh in range(2):
            for i in range(ng // 2):
                rs = slice(hh * rows + 8 * i, hh * rows + 8 * i + 8)
                lhs = jnp.concatenate([ad_h[p][rs], rd_h[p][rs]], axis=0)[:, 0:R_HD]
                res = [nt(lhs, jnp.concatenate([s_ref[2 * i + w, 2 * p + hh], zero_half], axis=0))
                       for w in range(2)]
                y_tiles.append(jnp.where(first_seq, res[0][0:8], res[1][0:8]))
                o_tiles.append(jnp.where(first_seq, res[0][8:16], res[1][8:16]))
        y0.append(jnp.concatenate(y_tiles, axis=0))
        o0.append(jnp.concatenate(o_tiles, axis=0))

    y = each(lambda a, nk, vh: a + mm(nk, vh), y0, n_k, v_h)
    q = n_b
    steps = int(math.log2(t_len))
    for i in range(steps):
        if i < steps - 1:
            res = each(lambda x, yy: mm(x, jnp.concatenate([x, yy], axis=1)), q, y)
            q = each(lambda x: x[:, 0:r2], res)
            y = each(lambda yy, x: yy + x[:, r2:2 * r2], y, res)
        else:
            y = each(lambda yy, x: yy + mm(x, yy), y, q)
    u = y

    o_h = each(lambda a, mb, uu, mk, vh: a + mm(mb, uu) + mm(mk, vh), o0, m_b, u, m_k, v_h)
    for sl, o in zip(sls, o_h):
        o_ref[:, sl] = o[0:rows] + pltpu.roll(o[rows:r2], R_HD, 1)

    for p in pairs:
        ut = jnp.transpose(u[p])[0:R_HD]
        vt = jnp.transpose(v_h[p])[0:R_HD]
        rhs = jnp.concatenate([be_h[p], ke_h[p]], axis=0)
        keys = [(hh, s) for hh in range(2) for s in range(ng)]
        sel = [(lane_blk == hh * ng + s).astype(F32) for hh, s in keys]
        lhs = jnp.concatenate([jnp.concatenate([ut * m, vt * m], axis=1) for m in sel], axis=0)
        upd = mm(lhs, rhs)
        for idx, (hh, s) in enumerate(keys):
            row0 = hh * rows + s * t_len
            dec = dec_h[p][row0:row0 + 1, 0:R_HD]
            sn_ref[s, 2 * p + hh] = (s_ref[s, 2 * p + hh] * dec
                                     + upd[idx * R_HD:(idx + 1) * R_HD, 0:R_HD])


def _rwkv_short(r, kp, v, kk, b, lw, state, t_len):
    m = r.shape[0]
    ng = SHORT_ROWS // t_len
    seq_spec = pl.BlockSpec((SHORT_ROWS, R_WIDTH), lambda i: (i, 0))
    st_spec = pl.BlockSpec((ng, R_HEADS, R_HD, R_HD), lambda i: (i, 0, 0, 0))
    return pl.pallas_call(
        functools.partial(_rwkv_short_kernel, t_len=t_len),
        grid=(m // SHORT_ROWS,),
        in_specs=[seq_spec] * 6 + [st_spec],
        out_specs=[seq_spec, st_spec],
        out_shape=[jax.ShapeDtypeStruct((m, R_WIDTH), F32), jax.ShapeDtypeStruct(state.shape, F32)],
        compiler_params=_params("parallel"),
        name="rwkv_short",
    )(r, kp, v, kk, b, lw, state)


ATTN_ROWS = SPAN * max(d for _, d in A_GROUPS)
ATTN_BATCH = 8


def _attn_prompt_kernel(*refs):
    ins = refs[:5 * N_GROUPS]
    o_ref = refs[5 * N_GROUPS]
    o_sc = refs[5 * N_GROUPS + 1:5 * N_GROUPS + 1 + N_GROUPS]
    l_sc = refs[5 * N_GROUPS + 1 + N_GROUPS:]
    n = pl.program_id(1)
    i = lax.broadcasted_iota(jnp.int32, (SPAN, 2 * SPAN), 0)
    j = lax.broadcasted_iota(jnp.int32, (SPAN, 2 * SPAN), 1)
    dist = SPAN + i - j
    band = (dist >= 0) & (dist <= SPAN)
    band_first = band & ((n > 0) | (j >= SPAN))
    scale = 1.0 / math.sqrt(A_HD)
    cat = lambda a, b: jnp.concatenate([a, b], axis=0).astype(BF16)

    for g, (_, dil) in enumerate(A_GROUPS):
        q_ref, kp_ref, kc_ref, vp_ref, vc_ref = ins[5 * g:5 * g + 5]

        def rows(run, rho):
            start = run * SPAN * dil + rho
            return pl.ds(start, SPAN, stride=dil) if dil > 1 else pl.ds(start, SPAN)

        all_probs = [(run, rho) for run in range(ATTN_ROWS // (SPAN * dil)) for rho in range(dil)]
        for b0 in range(0, len(all_probs), ATTN_BATCH):
            probs = all_probs[b0:b0 + ATTN_BATCH]
            q = [q_ref[rows(run, rho), :].astype(BF16) for run, rho in probs]
            k = [cat(kp_ref[rows(0, rho), :] if run == 0 else kc_ref[rows(run - 1, rho), :],
                     kc_ref[rows(run, rho), :]) for run, rho in probs]
            v = [cat(vp_ref[rows(0, rho), :] if run == 0 else vc_ref[rows(run - 1, rho), :],
                     vc_ref[rows(run, rho), :]) for run, rho in probs]
            s = [lax.dot_general(a, b, (((1,), (1,)), ((), ())), preferred_element_type=F32) * scale
                 for a, b in zip(q, k)]
            s = [jnp.where(band_first if run == 0 else band, x, NEG) for x, (run, _) in zip(s, probs)]
            m = [jnp.max(x, axis=-1, keepdims=True) for x in s]
            p = [jnp.exp(x - y) for x, y in zip(s, m)]
            l = [jnp.sum(x, axis=-1, keepdims=True) for x in p]
            o = [jnp.dot(x.astype(BF16), y, preferred_element_type=F32) for x, y in zip(p, v)]
            for (run, rho), oo, ll, mx in zip(probs, o, l, m):
                o_sc[g][rows(run, rho), :] = oo / ll
                l_sc[g][rows(run, rho), :] = jnp.broadcast_to(mx + jnp.log(ll), (SPAN, A_HD))

    lse = [ref[...] for ref in l_sc]
    top = functools.reduce(jnp.maximum, lse)
    e = [jnp.exp(x - top) for x in lse]
    den = functools.reduce(jnp.add, e)
    o_ref[...] = functools.reduce(jnp.add, [(w / den) * ref[...] for w, ref in zip(e, o_sc)])


def _attn_prompt(h, bsz, t):
    nblk = t // ATTN_ROWS
    in_specs = []
    for g, (_, dil) in enumerate(A_GROUPS):
        prev_rows = SPAN * dil
        per_blk = ATTN_ROWS // prev_rows
        col = lambda which, hd, g=g: (_col_q, _col_k, _col_v)[which](g) // A_HD + hd
        cur = lambda which, col=col: pl.BlockSpec(
            (ATTN_ROWS, A_HD), lambda b, n, hd: (b * nblk + n, col(which, hd)))
        prev = lambda which, col=col, per_blk=per_blk, prev_rows=prev_rows: pl.BlockSpec(
            (prev_rows, A_HD),
            lambda b, n, hd: (jnp.maximum((b * nblk + n) * per_blk - 1, 0), col(which, hd)))
        in_specs += [cur(0), prev(1), cur(1), prev(2), cur(2)]
    return pl.pallas_call(
        _attn_prompt_kernel,
        grid=(bsz, nblk, A_HEADS),
        in_specs=in_specs,
        out_specs=pl.BlockSpec((ATTN_ROWS, A_HD), lambda b, n, hd: (b * nblk + n, hd)),
        out_shape=jax.ShapeDtypeStruct((bsz * t, A_WIDTH), F32),
        scratch_shapes=[pltpu.VMEM((ATTN_ROWS, A_HD), F32)] * (2 * N_GROUPS),
        compiler_params=_params("parallel", "arbitrary", "arbitrary"),
        name="attn_prompt",
    )(*([h] * (5 * N_GROUPS)))


SAMPLE_ROWS = 8


def _attn_sample_kernel(*refs, t_new):
    q_refs, k_refs, v_refs = refs[0:3], refs[3:6], refs[6:9]
    c_refs, o_ref = refs[9:12], refs[12]
    scale = 1.0 / math.sqrt(A_HD)
    kvh_n = 2 * A_HEADS
    n_seq = SAMPLE_ROWS // t_new
    n_q = A_HEADS * SAMPLE_ROWS
    n_c = SPAN * kvh_n
    n_l = n_c + LANES
    row = lax.broadcasted_iota(jnp.int32, (n_q, n_l), 0)
    lane = lax.broadcasted_iota(jnp.int32, (n_q, n_l), 1)
    h_r, seq_r, t_r = row // SAMPLE_ROWS, (row % SAMPLE_ROWS) // t_new, row % t_new
    in_buf = lane < n_c
    j_l = lane // kvh_n
    new = lane - n_c
    t_l = new % t_new
    own = (in_buf & ((lane % kvh_n) == h_r)) | (
        ~in_buf & ((new // SAMPLE_ROWS) == h_r) & (((new % SAMPLE_ROWS) // t_new) == seq_r))
    n_pad = LANES - kvh_n * SAMPLE_ROWS
    pad = [jnp.zeros((n_pad, A_HD), F32)] if n_pad else []
    heads = lambda ref: [ref[:, hd * A_HD:(hd + 1) * A_HD] for hd in range(A_HEADS)]
    outs, lses = [], []
    for g in range(N_GROUPS):
        q = jnp.concatenate(heads(q_refs[g]), axis=0).astype(BF16)
        kv_new = jnp.concatenate(heads(k_refs[g]) + heads(v_refs[g]) + pad, axis=0)
        kv2, s = [], []
        for seq in range(n_seq):
            for t in ([None] if g == 0 else range(t_new)):
                buf = c_refs[g][seq, :, 0 if t is None else t].reshape(n_c, A_HD)
                kv2.append(jnp.concatenate([buf, kv_new], axis=0).astype(BF16))
                if t is None:
                    valid = own & (seq_r == seq) & ((in_buf & (j_l >= t_r)) | (~in_buf & (t_l <= t_r)))
                else:
                    valid = own & (seq_r == seq) & (t_r == t) & (in_buf | (t_l == t))
                st = lax.dot_general(q, kv2[-1], (((1,), (1,)), ((), ())), preferred_element_type=F32)
                s.append(jnp.where(valid, st * scale, NEG))
        m = functools.reduce(jnp.maximum, [jnp.max(x, axis=-1, keepdims=True) for x in s])
        p = [jnp.exp(x - m) for x in s]
        l = functools.reduce(jnp.add, [jnp.sum(x, axis=-1, keepdims=True) for x in p])
        shift = lambda x: jnp.concatenate(
            [pltpu.roll(x[:, 0:n_c], A_HEADS, 1), pltpu.roll(x[:, n_c:n_l], A_HEADS * SAMPLE_ROWS, 1)],
            axis=1)
        o = functools.reduce(jnp.add, [
            jnp.dot(shift(x).astype(BF16), y, preferred_element_type=F32) for x, y in zip(p, kv2)])
        outs.append(o / l)
        lses.append(m + jnp.log(l))
    top = functools.reduce(jnp.maximum, lses)
    e = [jnp.exp(x - top) for x in lses]
    den = functools.reduce(jnp.add, e)
    o = functools.reduce(jnp.add, [(w / den) * x for w, x in zip(e, outs)])
    for hd in range(A_HEADS):
        o_ref[:, hd * A_HD:(hd + 1) * A_HD] = o[hd * SAMPLE_ROWS:(hd + 1) * SAMPLE_ROWS]


def _attn_sample(h, caches, bsz, t_new):
    kvh_n = 2 * A_HEADS
    n_seq = SAMPLE_ROWS // t_new
    assert SAMPLE_ROWS % t_new == 0 and bsz % n_seq == 0
    views = []
    specs = []
    for (win, dil), cache in zip(A_GROUPS, caches):
        assert cache.shape[1] == win and win // dil == SPAN and (dil == 1 or t_new <= dil)
        views.append(cache.reshape(bsz, SPAN, dil, kvh_n, A_HD))
        specs.append(pl.BlockSpec((n_seq, SPAN, min(dil, t_new), kvh_n, A_HD), lambda i: (i, 0, 0, 0, 0)))
    col = lambda which, g: pl.BlockSpec(
        (SAMPLE_ROWS, A_WIDTH), lambda i: (i, (_col_q, _col_k, _col_v)[which](g) // A_WIDTH))
    qkv_specs = [col(which, g) for which in range(3) for g in range(N_GROUPS)]
    return pl.pallas_call(
        functools.partial(_attn_sample_kernel, t_new=t_new),
        grid=(bsz // n_seq,),
        in_specs=qkv_specs + specs,
        out_specs=pl.BlockSpec((SAMPLE_ROWS, A_WIDTH), lambda i: (i, 0)),
        out_shape=jax.ShapeDtypeStruct((bsz * t_new, A_WIDTH), F32),
        compiler_params=_params("parallel"),
        name="attn_sample",
    )(*([h] * (3 * N_GROUPS)), *views)


def _final_kernel(x_ref, o_ref, bon_ref, zr_ref, oa_ref, zgg_ref, seg_ref, lnxg_ref, lnxb_ref, bg_ref,
                  woa_ref, wob_ref, wout_ref, lng_ref, lnb_ref, y_ref):
    seg2 = seg_ref[...]
    o = o_ref[...]
    mu = _seg_sum(o, seg2) * (1.0 / R_HD)
    d = o - mu
    var = _seg_sum(d * d, seg2) * (1.0 / R_HD)
    on = d * lax.rsqrt(var + GN_EPS) * lnxg_ref[...] + lnxb_ref[...]
    zr = zr_ref[...].astype(F32)
    y_r = (on + bon_ref[...].astype(F32)) * (zr * jax.nn.sigmoid(zr))

    z_a = zgg_ref[:, 0:A_WIDTH].astype(F32)
    y_a = oa_ref[...] * (z_a * jax.nn.sigmoid(z_a))

    g_r = zgg_ref[:, A_WIDTH:A_WIDTH + D_MODEL].astype(F32)
    g_a = zgg_ref[:, A_WIDTH + D_MODEL:A_WIDTH + 2 * D_MODEL].astype(F32)
    gate_r = jax.nn.sigmoid(g_r + bg_ref[:, 0:D_MODEL])
    gate_a = jax.nn.sigmoid(g_a + bg_ref[:, D_MODEL:2 * D_MODEL])
    mix = (gate_r * jnp.dot(y_r.astype(BF16), woa_ref[...], preferred_element_type=F32)
           + gate_a * jnp.dot(y_a.astype(BF16), wob_ref[...], preferred_element_type=F32))
    yy = ALPHA * x_ref[...] + jnp.dot(mix.astype(BF16), wout_ref[...], preferred_element_type=F32)
    mean = jnp.mean(yy, axis=-1, keepdims=True)
    cen = yy - mean
    variance = jnp.mean(cen * cen, axis=-1, keepdims=True)
    y_ref[...] = cen * lax.rsqrt(variance + LN_EPS) * lng_ref[...] + lnb_ref[...]


def _final(x, o, bonus, h16, o_attn, weights, tm):
    m = x.shape[0]
    row = lambda w, j=0: pl.BlockSpec((tm, w), lambda i: (i, j))
    const = lambda a: pl.BlockSpec(a.shape, lambda i: (0, 0))
    in_specs = ([row(D_MODEL), row(R_WIDTH), row(R_WIDTH), row(R_WIDTH, COL_ZR // R_WIDTH), row(A_WIDTH),
                 row(ZGG_W, COL_ZGG // ZGG_W)] + [const(a) for a in weights])
    return pl.pallas_call(
        _final_kernel,
        grid=(m // tm,),
        in_specs=in_specs,
        out_specs=row(D_MODEL),
        out_shape=jax.ShapeDtypeStruct((m, D_MODEL), F32),
        compiler_params=_params("parallel"),
        name="merge_out_ln",
    )(x, o, bonus, h16, o_attn, h16, *weights)


def _layer(x2, bsz, t, shift_prev, wkv0, caches, wts):
    m = x2.shape[0]
    h, h16 = _in_proj(x2, wts["w_proj"], 2048 if m % 2048 == 0 else m)

    tm = 1024 if m % 1024 == 0 and t % 1024 == 0 else (512 if m % 512 == 0 else m)
    if caches is None:
        first = None
    else:
        first = jnp.zeros((bsz, t, SHIFT_COLS), F32).at[:, 0].set(shift_prev).reshape(m, SHIFT_COLS)
    r, kp, v, kk, b, lw, bonus = _rwkv_prep(h, first, t, wts["prep"], tm)

    if t % CHUNK == 0 and wkv0 is None:
        seq = lambda a: a.reshape(bsz, t, R_WIDTH)
        s0 = jnp.zeros((bsz, N_PAIRS, LANES, LANES), F32)
        o_seq, st = _rwkv_recurrence(seq(r), seq(kp), seq(v), seq(kk), seq(b), seq(lw), s0, 2)
        o_rwkv = o_seq.reshape(m, R_WIDTH)
        wkv_new = _pairs_to_state(st)
    else:
        o_rwkv, wkv_new = _rwkv_short(r, kp, v, kk, b, lw, wkv0, t)

    def rows_cols(r0, c0, width):
        if t % 8 == 0:
            return h.reshape(bsz, t, N_PROJ32)[:, r0:, c0:c0 + width]
        return h[:, c0:c0 + width].reshape(bsz, t, width)[:, r0:]

    if caches is None:
        o_attn = _attn_prompt(h, bsz, t)
    else:
        o_attn = _attn_sample(h, caches, bsz, t)

    tmf = 512 if m % 512 == 0 else m
    y = _final(x2, o_rwkv, bonus, h16, o_attn, wts["final"], tmf)

    new_kv = []
    for g, (win, _) in enumerate(A_GROUPS):
        keep = min(win, t) if caches is None else t
        new_kv.append(rows_cols(t - keep, _col_k(g), 2 * A_WIDTH).reshape(bsz, keep, 2, A_HEADS, A_HD))
    shift_new = rows_cols(t - 1, COL_ZS, SHIFT_COLS)[:, 0]
    return y.reshape(bsz, t, D_MODEL), new_kv, wkv_new, shift_new


def kernel(x_prompt, x_sample, cache_kv_g1, cache_kv_g2, cache_kv_g3, state_rwkv_wkv, state_rwkv_shift,
           w_in, b_gate, mu_shift, w0, w_w2, a0, w_a2, k_k, k_a, r_k, lnx_g, lnx_b,
           w_oa, w_ob, w_out, ln_g, ln_b):
    assert w_in.shape[0] == DEPTH
    bp, tp, _ = x_prompt.shape
    bs, ts, _ = x_sample.shape
    xp = x_prompt.reshape(bp * tp, D_MODEL)
    xs = x_sample.reshape(bs * ts, D_MODEL)
    head = lax.broadcasted_iota(jnp.int32, (LANES, LANES), 0) // R_HD
    seg = (head == head.T).astype(BF16)
    seg2 = jnp.concatenate([seg, seg], axis=0)
    row = lambda a: a.reshape(1, -1)
    acc = [[] for _ in range(12)]
    for l in range(DEPTH):
        wf = w_in[l]
        c0 = SHIFT_COLS
        c1 = c0 + R_WIDTH
        c2 = c1 + 3 * A_QKV_W
        qkv_col = lambda which, g: wf[:, c1 + (which * N_GROUPS + g) * A_WIDTH:
                                      c1 + (which * N_GROUPS + g + 1) * A_WIDTH]
        w_proj = jnp.concatenate(
            [wf[:, :c0], jnp.zeros((D_MODEL, COL_Q - c0), F32)] + [qkv_col(0, g) for g in range(N_GROUPS)]
            + [qkv_col(which, g) for g in range(N_GROUPS) for which in (1, 2)]
            + [wf[:, c2:], wf[:, c0:c1]], axis=1).astype(BF16)
        zero = jnp.zeros((LORA, R_WIDTH), F32)
        w_lora = jnp.concatenate([jnp.concatenate([w_w2[l], zero], axis=1),
                                  jnp.concatenate([zero, w_a2[l]], axis=1)], axis=0)
        wl_hi = w_lora.astype(BF16)
        wl_lo = (w_lora - wl_hi.astype(F32)).astype(BF16)
        wts = {
            "w_proj": w_proj,
            "prep": (row(mu_shift[l]), row(w0[l]), row(a0[l]), jnp.concatenate([wl_hi, wl_hi], axis=0),
                     wl_lo, row(k_k[l]), row(k_a[l]), row(r_k[l]), seg2),
            "final": (seg2, row(lnx_g[l]), row(lnx_b[l]), row(b_gate[l]), w_oa[l].astype(BF16),
                      w_ob[l].astype(BF16), w_out[l].astype(BF16), row(ln_g[l]), row(ln_b[l])),
        }
        yp, kv_p, s_p, sh_p = _layer(xp, bp, tp, None, None, None, wts)
        ys, kv_s, s_s, sh_s = _layer(xs, bs, ts, state_rwkv_shift[l], state_rwkv_wkv[l],
                                     (cache_kv_g1[l], cache_kv_g2[l], cache_kv_g3[l]), wts)
        xp = yp.reshape(bp * tp, D_MODEL)
        xs = ys.reshape(bs * ts, D_MODEL)
        for idx, val in zip(range(2, 12), (kv_p[0], kv_s[0], kv_p[1], kv_s[1], kv_p[2], kv_s[2],
                                           s_p, s_s, sh_p, sh_s)):
            acc[idx].append(val)
    outs = [xp.reshape(bp, tp, D_MODEL), xs.reshape(bs, ts, D_MODEL)]
    outs += [jnp.stack(a) for a in acc[2:]]
    return tuple(outs)
```

```python
import functools
import math

import jax
import jax.numpy as jnp
from jax import lax
from jax.experimental import pallas as pl
from jax.experimental.pallas import tpu as pltpu

F32 = jnp.float32
BF16 = jnp.bfloat16

D_MODEL = 1024
R_HEADS = 8
R_HD = 64
R_WIDTH = R_HEADS * R_HD
LORA = 64
SHIFT_COLS = 3 * R_WIDTH + 2 * LORA
GN_EPS = 64e-5
A_GROUPS = ((128, 1), (512, 4), (2048, 16))
N_GROUPS = 3
A_HEADS = 4
A_HD = 128
A_WIDTH = A_HEADS * A_HD
A_QKV_W = N_GROUPS * A_WIDTH
SPAN = 128
DEPTH = 1
ALPHA = (2 * DEPTH) ** 0.25
LN_EPS = 1e-5
NEG = -1e30

LANES = 128
CHUNK = 64
N_PAIRS = R_HEADS // 2
VMEM_LIMIT = 48 * 1024 * 1024

PROJ_TN = 512
COL_ZS = 0
COL_Q = 2048
COL_KV = COL_Q + A_QKV_W
N_PROJ32 = COL_KV + 2 * A_QKV_W
ZGG_W = A_WIDTH + 2 * D_MODEL
COL_ZGG = 0
COL_ZR = ZGG_W
N_PROJ16 = COL_ZR + R_WIDTH
assert COL_Q % A_WIDTH == 0 and COL_KV % A_WIDTH == 0 and COL_ZR % R_WIDTH == 0
assert N_PROJ32 % PROJ_TN == 0 and N_PROJ16 % PROJ_TN == 0


def _col_q(g):
    return COL_Q + g * A_WIDTH


def _col_k(g):
    return COL_KV + g * 2 * A_WIDTH


def _col_v(g):
    return _col_k(g) + A_WIDTH


def _params(*sem):
    return pltpu.CompilerParams(dimension_semantics=sem, vmem_limit_bytes=VMEM_LIMIT)


def _split2(x):
    hi = x.astype(BF16)
    return hi, (x - hi.astype(F32)).astype(BF16)


def _seg_sum(x, seg2):
    cols = []
    for c in range(x.shape[1] // LANES):
        hi, lo = _split2(x[:, c * LANES:(c + 1) * LANES])
        cols.append(jnp.dot(jnp.concatenate([hi, lo], axis=1), seg2, preferred_element_type=F32))
    return jnp.concatenate(cols, axis=1)


def _mm_kernel(x_ref, w_ref, o32_ref, o16_ref, *, n32):
    j = pl.program_id(1)
    product = lambda: jnp.dot(x_ref[...].astype(BF16), w_ref[...], preferred_element_type=F32)

    @pl.when(j < n32)
    def _():
        o32_ref[...] = product()

    @pl.when(j >= n32)
    def _():
        o16_ref[...] = product().astype(BF16)


def _in_proj(x, w, tm):
    m, k = x.shape
    n32 = N_PROJ32 // PROJ_TN
    n16 = N_PROJ16 // PROJ_TN
    return pl.pallas_call(
        functools.partial(_mm_kernel, n32=n32),
        grid=(m // tm, n32 + n16),
        in_specs=[pl.BlockSpec((tm, k), lambda i, j: (i, 0)),
                  pl.BlockSpec((k, PROJ_TN), lambda i, j: (0, j))],
        out_specs=[pl.BlockSpec((tm, PROJ_TN), lambda i, j: (i, jnp.minimum(j, n32 - 1))),
                   pl.BlockSpec((tm, PROJ_TN), lambda i, j: (i, jnp.maximum(j - n32, 0)))],
        out_shape=[jax.ShapeDtypeStruct((m, N_PROJ32), F32), jax.ShapeDtypeStruct((m, N_PROJ16), BF16)],
        compiler_params=_params("parallel", "arbitrary"),
        name="in_proj",
    )(x, w)


def _prep_math(zs, prev, mu, w0, a0, wl_a, wl_b, k_k, k_a, r_k, seg2, outs):
    r_o, kp_o, v_o, kk_o, b_o, lw_o, bon_o = outs
    zsm = zs + mu * (prev - zs)
    r = zsm[:, 0:R_WIDTH]
    k = zsm[:, R_WIDTH:2 * R_WIDTH]
    v = zsm[:, 2 * R_WIDTH:3 * R_WIDTH]
    slab = zsm[:, 3 * R_WIDTH:SHIFT_COLS]
    lane = lax.broadcasted_iota(jnp.int32, slab.shape, 1)
    slab = jnp.where(lane < LORA, jnp.tanh(slab), slab)
    s_hi, s_lo = _split2(slab)
    lin = (jnp.dot(jnp.concatenate([s_hi, s_lo], axis=1), wl_a, preferred_element_type=F32)
           + jnp.dot(s_hi, wl_b, preferred_element_type=F32))
    wlin = w0 + lin[:, 0:R_WIDTH]
    alin = a0 + lin[:, R_WIDTH:2 * R_WIDTH]
    nw = -wlin
    softplus = jnp.maximum(nw, 0.0) + jnp.log1p(jnp.exp(-jnp.abs(nw)))
    w = -softplus - 0.5
    lw = -jnp.exp(w)
    a = jax.nn.sigmoid(alin)
    kkr = k * k_k
    kk = kkr / jnp.maximum(jnp.sqrt(_seg_sum(kkr * kkr, seg2)), 1e-12)
    kp = k * (1.0 + (a - 1.0) * k_a)
    r_o[...] = r.astype(r_o.dtype)
    kp_o[...] = kp.astype(kp_o.dtype)
    v_o[...] = v.astype(v_o.dtype)
    kk_o[...] = kk.astype(kk_o.dtype)
    b_o[...] = (kk * a).astype(b_o.dtype)
    lw_o[...] = lw
    bon_o[...] = (_seg_sum(r * kp * r_k, seg2) * v).astype(bon_o.dtype)


def _prep_prompt_kernel(zs_ref, p8_ref, mu_ref, w0_ref, a0_ref, wla_ref, wlb_ref, kk_ref, ka_ref,
                        rk_ref, seg_ref, *outs, tiles_per_seq):
    i = pl.program_id(0)
    zs = zs_ref[...]
    rolled = pltpu.roll(zs, 1, 0)
    carry = jnp.where(i % tiles_per_seq == 0, 0.0, p8_ref[7:8, :])
    row = lax.broadcasted_iota(jnp.int32, zs.shape, 0)
    prev = jnp.where(row == 0, carry, rolled)
    _prep_math(zs, prev, mu_ref[...], w0_ref[...], a0_ref[...], wla_ref[...], wlb_ref[...],
               kk_ref[...], ka_ref[...], rk_ref[...], seg_ref[...], outs)


def _prep_sample_kernel(zs_ref, first_ref, mu_ref, w0_ref, a0_ref, wla_ref, wlb_ref, kk_ref, ka_ref,
                        rk_ref, seg_ref, *outs, seq_len):
    zs = zs_ref[...]
    rolled = pltpu.roll(zs, 1, 0)
    row = lax.broadcasted_iota(jnp.int32, zs.shape, 0)
    prev = jnp.where(row % seq_len == 0, first_ref[...], rolled)
    _prep_math(zs, prev, mu_ref[...], w0_ref[...], a0_ref[...], wla_ref[...], wlb_ref[...],
               kk_ref[...], ka_ref[...], rk_ref[...], seg_ref[...], outs)


def _rwkv_prep(h, first, seq_len, weights, tm):
    m = h.shape[0]
    const = lambda a: pl.BlockSpec(a.shape, lambda i: (0, 0))
    row_spec = pl.BlockSpec((tm, SHIFT_COLS), lambda i: (i, COL_ZS))
    if first is None:
        kern = functools.partial(_prep_prompt_kernel, tiles_per_seq=seq_len // tm)
        second = h
        second_spec = pl.BlockSpec((8, SHIFT_COLS), lambda i: (jnp.maximum(i * (tm // 8) - 1, 0), COL_ZS))
    else:
        kern = functools.partial(_prep_sample_kernel, seq_len=seq_len)
        second = first
        second_spec = pl.BlockSpec((tm, SHIFT_COLS), lambda i: (i, 0))
    out_spec = pl.BlockSpec((tm, R_WIDTH), lambda i: (i, 0))
    return pl.pallas_call(
        kern,
        grid=(m // tm,),
        in_specs=[row_spec, second_spec] + [const(a) for a in weights],
        out_specs=[out_spec] * 7,
        out_shape=[jax.ShapeDtypeStruct((m, R_WIDTH), dt) for dt in (BF16,) * 5 + (F32, BF16)],
        compiler_params=_params("parallel"),
        name="rwkv_prep",
    )(h, second, *weights)


CHUNKS_PER_STEP = 2


def _rwkv_chunk_kernel(*refs, nb):
    ins, s0_ref, o_ref, st_ref = refs[0:18], refs[18], refs[19], refs[20]
    yo_sc, ti_sc, mb_sc, ut_sc, vs_sc, dec_sc = refs[21:27]
    n = pl.program_id(1)
    c = CHUNK
    c2 = 2 * c
    lane = lax.broadcasted_iota(jnp.int32, (1, LANES), 1)
    m_lo = (lane < R_HD).astype(F32)
    m_hi = 1.0 - m_lo
    row_c = lax.broadcasted_iota(jnp.int32, (c, 4 * c), 0)
    col_c = lax.broadcasted_iota(jnp.int32, (c, 4 * c), 1)
    tri3 = ((row_c >= col_c % c) & (col_c < 3 * c)).astype(BF16)
    row2 = lax.broadcasted_iota(jnp.int32, (c2, c2), 0)
    col2 = lax.broadcasted_iota(jnp.int32, (c2, c2), 1)
    strict = row2 > col2
    incl = row2 >= col2
    eye = (row2 == col2).astype(F32)

    def stack(x):
        return jnp.concatenate([x * m_lo, x * m_hi], axis=0)

    def mm(a, b):
        return jnp.dot(a.astype(BF16), b.astype(BF16), preferred_element_type=F32)

    def nt(a, b):
        return lax.dot_general(a.astype(BF16), b.astype(BF16), (((1,), (1,)), ((), ())),
                               preferred_element_type=F32)

    each = lambda f, *xs: [f(*a) for a in zip(*xs)]
    chains = [(bi, slice(p * LANES, (p + 1) * LANES), p) for bi in range(nb) for p in range(N_PAIRS)]

    def cumsum(lw_all):
        hi = lw_all.astype(BF16)
        r1 = lw_all - hi.astype(F32)
        mid = r1.astype(BF16)
        lo = (r1 - mid.astype(F32)).astype(BF16)
        return jnp.dot(tri3, jnp.concatenate([hi, mid, lo, jnp.zeros_like(lo)], axis=0),
                       preferred_element_type=F32)

    def prepare(refs, row0, slot):
        lw_ref = refs[5]
        rs = slice(row0, row0 + c)
        g_all = [cumsum(lw_ref[bi, rs, :]) for bi in range(nb)]
        yield
        load = lambda ref: [ref[bi, rs, sl].astype(F32) for bi, sl, _ in chains]
        r, kp, v, kk, b, lw = (load(ref) for ref in refs)
        g = [g_all[bi][:, sl] for bi, sl, _ in chains]
        g_last = each(lambda x: x[c - 1:c, :], g)
        e_g = each(jnp.exp, g)
        e_ng = each(lambda x: jnp.exp(-x), g)
        e_gm = each(lambda x, y: jnp.exp(x - y), g, lw)
        e_cg = each(lambda x, y: jnp.exp(y - x), g, g_last)
        ad_s = each(lambda x, y: stack(-(x * y)), kk, e_gm)
        rd_s = each(lambda x, y: stack(x * y), r, e_g)
        bd_s = each(lambda x, y: stack(x * y), b, e_ng)
        kd_s = each(lambda x, y: stack(x * y), kp, e_ng)
        be_s = each(lambda x, y: stack(x * y), b, e_cg)
        ke_s = each(lambda x, y: stack(x * y), kp, e_cg)
        for idx in range(len(chains)):
            yo_sc[slot, idx, 0:c2, 0:c2] = ad_s[idx].astype(BF16)
            yo_sc[slot, idx, c2:2 * c2, 0:c2] = rd_s[idx].astype(BF16)
            ut_sc[slot, idx, :, 0:c2] = jnp.transpose(be_s[idx]).astype(BF16)
            ut_sc[slot, idx, :, c2:2 * c2] = jnp.transpose(ke_s[idx]).astype(BF16)
            vs_sc[slot, idx] = stack(v[idx]).astype(BF16)
            dec_sc[slot, idx] = jnp.transpose(jnp.broadcast_to(jnp.exp(g_last[idx]), (LANES, LANES)))
        yield
        z = each(lambda a1, a2, b1, b2: nt(jnp.concatenate([a1, a2], axis=0),
                                           jnp.concatenate([b1, b2], axis=0)), ad_s, rd_s, bd_s, kd_s)
        yield
        for idx, zz in enumerate(z):
            yo_sc[slot, idx, 0:c2, c2:2 * c2] = jnp.where(strict, zz[0:c2, c2:2 * c2], 0.0).astype(BF16)
            yo_sc[slot, idx, c2:2 * c2, c2:2 * c2] = jnp.where(
                incl, zz[c2:2 * c2, c2:2 * c2], 0.0).astype(BF16)
            mb_sc[slot, idx] = jnp.where(incl, zz[c2:2 * c2, 0:c2], 0.0).astype(BF16)
        n_b = each(lambda x: jnp.where(strict, x[0:c2, 0:c2], 0.0), z)
        t_inv = each(lambda x: eye + x, n_b)
        q = each(lambda x: mm(x, x), n_b)
        yield
        steps = int(math.log2(c))
        for i in range(1, steps):
            if i < steps - 1:
                res = each(lambda x, tt: mm(x, jnp.concatenate([x, tt], axis=1)), q, t_inv)
                q = each(lambda x: x[:, 0:c2], res)
                t_inv = each(lambda tt, x: tt + x[:, c2:2 * c2], t_inv, res)
            else:
                t_inv = each(lambda tt, x: tt + mm(x, tt), t_inv, q)
            yield
        for idx, tt in enumerate(t_inv):
            ti_sc[slot, idx] = tt.astype(BF16)

    def advance(slot, row0):
        idxs = list(range(len(chains)))
        st = [st_ref[bi, p] for bi, _, p in chains]
        vs = [vs_sc[slot, i] for i in idxs]
        yo = [jnp.dot(yo_sc[slot, i], jnp.concatenate([s.astype(BF16), w], axis=0),
                      preferred_element_type=F32) for i, s, w in zip(idxs, st, vs)]
        yield
        u = [jnp.dot(ti_sc[slot, i], x[0:c2].astype(BF16), preferred_element_type=F32)
             for i, x in zip(idxs, yo)]
        ub = [x.astype(BF16) for x in u]
        yield
        o_bd = [x[c2:2 * c2] + jnp.dot(mb_sc[slot, i], w, preferred_element_type=F32)
                for i, x, w in zip(idxs, yo, ub)]
        for (bi, sl, _), o in zip(chains, o_bd):
            o_ref[bi, row0:row0 + c, sl] = o[0:c] + o[c:c2]
        yield
        upd = [jnp.dot(ut_sc[slot, i], jnp.concatenate([w, x], axis=0), preferred_element_type=F32)
               for i, w, x in zip(idxs, ub, vs)]
        for i, ((bi, _, p), s, d) in enumerate(zip(chains, st, upd)):
            st_ref[bi, p] = dec_sc[slot, i] * s + d

    def interleave(*stages):
        live = list(stages)
        while live:
            for gen in list(live):
                if next(gen, StopIteration) is StopIteration:
                    live.remove(gen)

    @pl.when(n == 0)
    def _():
        st_ref[...] = s0_ref[...]
        interleave(prepare(ins[0:6], 0, 0))

    interleave(advance(0, 0), prepare(ins[6:12], 0, 1))
    interleave(advance(1, c), prepare(ins[12:18], 0, 0))


def _rwkv_recurrence(r, kp, v, kk, b, lw, s0, nb):
    bsz, t, _ = r.shape
    assert CHUNKS_PER_STEP == 2
    rows = CHUNKS_PER_STEP * CHUNK
    nsteps = t // rows
    last = t // CHUNK - 1
    chunk = lambda which: pl.BlockSpec((nb, CHUNK, R_WIDTH), lambda i, n: (i, which(n), 0))
    chunk_specs = ([chunk(lambda n: 0)] * 6 + [chunk(lambda n: 2 * n + 1)] * 6
                   + [chunk(lambda n: jnp.minimum(2 * n + 2, last))] * 6)
    seq_spec = pl.BlockSpec((nb, rows, R_WIDTH), lambda i, n: (i, n, 0))
    st_spec = pl.BlockSpec((nb, N_PAIRS, LANES, LANES), lambda i, n: (i, 0, 0, 0))
    nchain = nb * N_PAIRS
    c2 = 2 * CHUNK
    slots = 2
    scratch = [pltpu.VMEM((slots, nchain, 2 * c2, 2 * c2), BF16),
               pltpu.VMEM((slots, nchain, c2, c2), BF16),
               pltpu.VMEM((slots, nchain, c2, c2), BF16),
               pltpu.VMEM((slots, nchain, c2, 2 * c2), BF16),
               pltpu.VMEM((slots, nchain, c2, c2), BF16),
               pltpu.VMEM((slots, nchain, LANES, LANES), F32)]
    return pl.pallas_call(
        functools.partial(_rwkv_chunk_kernel, nb=nb),
        grid=(bsz // nb, nsteps),
        in_specs=chunk_specs + [st_spec],
        out_specs=[seq_spec, st_spec],
        out_shape=[jax.ShapeDtypeStruct((bsz, t, R_WIDTH), F32),
                   jax.ShapeDtypeStruct(s0.shape, F32)],
        scratch_shapes=scratch,
        compiler_params=_params("parallel", "arbitrary"),
        name="rwkv_chunks",
    )(*([r, kp, v, kk, b, lw] * 3), s0)


def _pairs_to_state(st):
    bsz = st.shape[0]
    s6 = st.reshape(bsz, N_PAIRS, 2, R_HD, 2, R_HD)
    s = jnp.stack([s6[:, :, 0, :, 0, :], s6[:, :, 1, :, 1, :]], axis=2)
    return jnp.swapaxes(s.reshape(bsz, R_HEADS, R_HD, R_HD), -1, -2)


SHORT_ROWS = 64


def _rwkv_short_kernel(r_ref, kp_ref, v_ref, kk_ref, b_ref, lw_ref, s_ref, o_ref, sn_ref, *, t_len):
    rows = SHORT_ROWS
    ng = rows // t_len
    r2 = 2 * rows
    assert 2 * t_len == 8 and r2 == LANES
    lane = lax.broadcasted_iota(jnp.int32, (1, LANES), 1)
    m_lo = (lane < R_HD).astype(F32)
    m_hi = 1.0 - m_lo
    lane_blk = lane // t_len
    row_c = lax.broadcasted_iota(jnp.int32, (rows, 4 * rows), 0)
    col_c = lax.broadcasted_iota(jnp.int32, (rows, 4 * rows), 1)
    j_c = col_c % rows
    same_c = ((row_c // t_len) == (j_c // t_len)) & (col_c < 3 * rows)
    cum_lhs = jnp.concatenate([(same_c & (row_c >= j_c)).astype(BF16), same_c.astype(BF16)], axis=0)
    row2 = lax.broadcasted_iota(jnp.int32, (r2, r2), 0)
    col2 = lax.broadcasted_iota(jnp.int32, (r2, r2), 1)
    blk = (row2 // t_len) == (col2 // t_len)
    strict = blk & (row2 > col2)
    incl = blk & (row2 >= col2)
    first_seq = lax.broadcasted_iota(jnp.int32, (8, LANES), 0) < t_len
    zero_half = jnp.zeros((R_HD, R_HD), F32)

    def stack(x):
        return jnp.concatenate([x * m_lo, x * m_hi], axis=0)

    def half(x):
        return jnp.concatenate([x * m_lo, pltpu.roll(x, R_HD, 1) * m_lo], axis=0)

    def mm(a, b):
        return jnp.dot(a.astype(BF16), b.astype(BF16), preferred_element_type=F32)

    def nt(a, b):
        return lax.dot_general(a.astype(BF16), b.astype(BF16), (((1,), (1,)), ((), ())),
                               preferred_element_type=F32)

    each = lambda f, *xs: [f(*a) for a in zip(*xs)]

    lw_all = lw_ref[...]
    hi = lw_all.astype(BF16)
    r1 = lw_all - hi.astype(F32)
    mid = r1.astype(BF16)
    lo = (r1 - mid.astype(F32)).astype(BF16)
    g_both = jnp.dot(cum_lhs, jnp.concatenate([hi, mid, lo, jnp.zeros_like(lo)], axis=0),
                     preferred_element_type=F32)

    pairs = list(range(N_PAIRS))
    sls = [slice(p * LANES, (p + 1) * LANES) for p in pairs]
    load = lambda ref: [ref[:, sl].astype(F32) for sl in sls]
    r, kp, v, kk, b, lw = (load(ref) for ref in (r_ref, kp_ref, v_ref, kk_ref, b_ref, lw_ref))
    g = [g_both[0:rows, sl] for sl in sls]
    g_tot = [g_both[rows:r2, sl] for sl in sls]
    e_g = each(jnp.exp, g)
    e_ng = each(lambda x: jnp.exp(-x), g)
    e_gm = each(lambda x, y: jnp.exp(x - y), g, lw)
    e_cg = each(lambda x, y: jnp.exp(y - x), g, g_tot)
    ad = each(lambda x, y: -(x * y), kk, e_gm)
    rd = each(lambda x, y: x * y, r, e_g)
    bd = each(lambda x, y: x * y, b, e_ng)
    kd = each(lambda x, y: x * y, kp, e_ng)
    be_h = each(lambda x, y: half(x * y), b, e_cg)
    ke_h = each(lambda x, y: half(x * y), kp, e_cg)
    v_h = each(half, v)
    ad_h = each(half, ad)
    rd_h = each(half, rd)
    dec_h = each(lambda x: half(jnp.exp(x)), g_tot)

    z = each(lambda a1, a2, b1, b2: nt(jnp.concatenate([stack(a1), stack(a2)], axis=0),
                                       jnp.concatenate([stack(b1), stack(b2)], axis=0)), ad, rd, bd, kd)
    n_b = each(lambda x: jnp.where(strict, x[0:r2, 0:r2], 0.0), z)
    n_k = each(lambda x: jnp.where(strict, x[0:r2, r2:2 * r2], 0.0), z)
    m_b = each(lambda x: jnp.where(incl, x[r2:2 * r2, 0:r2], 0.0), z)
    m_k = each(lambda x: jnp.where(incl, x[r2:2 * r2, r2:2 * r2], 0.0), z)

    y0, o0 = [], []
    for p in pairs:
        y_tiles, o_tiles = [], []
        for hh in range(2):
            for i in range(ng // 2):
                rs = slice(hh * rows + 8 * i, hh * rows + 8 * i + 8)
                lhs = jnp.concatenate([ad_h[p][rs], rd_h[p][rs]], axis=0)[:, 0:R_HD]
                res = [nt(lhs, jnp.concatenate([s_ref[2 * i + w, 2 * p + hh], zero_half], axis=0))
                       for w in range(2)]
                y_tiles.append(jnp.where(first_seq, res[0][0:8], res[1][0:8]))
                o_tiles.append(jnp.where(first_seq, res[0][8:16], res[1][8:16]))
        y0.append(jnp.concatenate(y_tiles, axis=0))
        o0.append(jnp.concatenate(o_tiles, axis=0))

    y = each(lambda a, nk, vh: a + mm(nk, vh), y0, n_k, v_h)
    q = n_b
    steps = int(math.log2(t_len))
    for i in range(steps):
        if i < steps - 1:
            res = each(lambda x, yy: mm(x, jnp.concatenate([x, yy], axis=1)), q, y)
            q = each(lambda x: x[:, 0:r2], res)
            y = each(lambda yy, x: yy + x[:, r2:2 * r2], y, res)
        else:
            y = each(lambda yy, x: yy + mm(x, yy), y, q)
    u = y

    o_h = each(lambda a, mb, uu, mk, vh: a + mm(mb, uu) + mm(mk, vh), o0, m_b, u, m_k, v_h)
    for sl, o in zip(sls, o_h):
        o_ref[:, sl] = o[0:rows] + pltpu.roll(o[rows:r2], R_HD, 1)

    for p in pairs:
        ut = jnp.transpose(u[p])[0:R_HD]
        vt = jnp.transpose(v_h[p])[0:R_HD]
        rhs = jnp.concatenate([be_h[p], ke_h[p]], axis=0)
        keys = [(hh, s) for hh in range(2) for s in range(ng)]
        sel = [(lane_blk == hh * ng + s).astype(F32) for hh, s in keys]
        lhs = jnp.concatenate([jnp.concatenate([ut * m, vt * m], axis=1) for m in sel], axis=0)
        upd = mm(lhs, rhs)
        for idx, (hh, s) in enumerate(keys):
            row0 = hh * rows + s * t_len
            dec = dec_h[p][row0:row0 + 1, 0:R_HD]
            sn_ref[s, 2 * p + hh] = (s_ref[s, 2 * p + hh] * dec
                                     + upd[idx * R_HD:(idx + 1) * R_HD, 0:R_HD])


def _rwkv_short(r, kp, v, kk, b, lw, state, t_len):
    m = r.shape[0]
    ng = SHORT_ROWS // t_len
    seq_spec = pl.BlockSpec((SHORT_ROWS, R_WIDTH), lambda i: (i, 0))
    st_spec = pl.BlockSpec((ng, R_HEADS, R_HD, R_HD), lambda i: (i, 0, 0, 0))
    return pl.pallas_call(
        functools.partial(_rwkv_short_kernel, t_len=t_len),
        grid=(m // SHORT_ROWS,),
        in_specs=[seq_spec] * 6 + [st_spec],
        out_specs=[seq_spec, st_spec],
        out_shape=[jax.ShapeDtypeStruct((m, R_WIDTH), F32), jax.ShapeDtypeStruct(state.shape, F32)],
        compiler_params=_params("parallel"),
        name="rwkv_short",
    )(r, kp, v, kk, b, lw, state)


ATTN_ROWS = SPAN * max(d for _, d in A_GROUPS)
ATTN_BATCH = 8


def _attn_prompt_kernel(*refs):
    ins = refs[:5 * N_GROUPS]
    o_ref = refs[5 * N_GROUPS]
    o_sc = refs[5 * N_GROUPS + 1:5 * N_GROUPS + 1 + N_GROUPS]
    l_sc = refs[5 * N_GROUPS + 1 + N_GROUPS:]
    n = pl.program_id(1)
    i = lax.broadcasted_iota(jnp.int32, (SPAN, 2 * SPAN), 0)
    j = lax.broadcasted_iota(jnp.int32, (SPAN, 2 * SPAN), 1)
    dist = SPAN + i - j
    band = (dist >= 0) & (dist <= SPAN)
    band_first = band & ((n > 0) | (j >= SPAN))
    scale = 1.0 / math.sqrt(A_HD)
    cat = lambda a, b: jnp.concatenate([a, b], axis=0).astype(BF16)

    for g, (_, dil) in enumerate(A_GROUPS):
        q_ref, kp_ref, kc_ref, vp_ref, vc_ref = ins[5 * g:5 * g + 5]

        def rows(run, rho):
            start = run * SPAN * dil + rho
            return pl.ds(start, SPAN, stride=dil) if dil > 1 else pl.ds(start, SPAN)

        all_probs = [(run, rho) for run in range(ATTN_ROWS // (SPAN * dil)) for rho in range(dil)]
        for b0 in range(0, len(all_probs), ATTN_BATCH):
            probs = all_probs[b0:b0 + ATTN_BATCH]
            q = [q_ref[rows(run, rho), :].astype(BF16) for run, rho in probs]
            k = [cat(kp_ref[rows(0, rho), :] if run == 0 else kc_ref[rows(run - 1, rho), :],
                     kc_ref[rows(run, rho), :]) for run, rho in probs]
            v = [cat(vp_ref[rows(0, rho), :] if run == 0 else vc_ref[rows(run - 1, rho), :],
                     vc_ref[rows(run, rho), :]) for run, rho in probs]
            s = [lax.dot_general(a, b, (((1,), (1,)), ((), ())), preferred_element_type=F32) * scale
                 for a, b in zip(q, k)]
            s = [jnp.where(band_first if run == 0 else band, x, NEG) for x, (run, _) in zip(s, probs)]
            m = [jnp.max(x, axis=-1, keepdims=True) for x in s]
            p = [jnp.exp(x - y) for x, y in zip(s, m)]
            l = [jnp.sum(x, axis=-1, keepdims=True) for x in p]
            o = [jnp.dot(x.astype(BF16), y, preferred_element_type=F32) for x, y in zip(p, v)]
            for (run, rho), oo, ll, mx in zip(probs, o, l, m):
                o_sc[g][rows(run, rho), :] = oo / ll
                l_sc[g][rows(run, rho), :] = jnp.broadcast_to(mx + jnp.log(ll), (SPAN, A_HD))

    lse = [ref[...] for ref in l_sc]
    top = functools.reduce(jnp.maximum, lse)
    e = [jnp.exp(x - top) for x in lse]
    den = functools.reduce(jnp.add, e)
    o_ref[...] = functools.reduce(jnp.add, [(w / den) * ref[...] for w, ref in zip(e, o_sc)])


def _attn_prompt(h, bsz, t):
    nblk = t // ATTN_ROWS
    in_specs = []
    for g, (_, dil) in enumerate(A_GROUPS):
        prev_rows = SPAN * dil
        per_blk = ATTN_ROWS // prev_rows
        col = lambda which, hd, g=g: (_col_q, _col_k, _col_v)[which](g) // A_HD + hd
        cur = lambda which, col=col: pl.BlockSpec(
            (ATTN_ROWS, A_HD), lambda b, n, hd: (b * nblk + n, col(which, hd)))
        prev = lambda which, col=col, per_blk=per_blk, prev_rows=prev_rows: pl.BlockSpec(
            (prev_rows, A_HD),
            lambda b, n, hd: (jnp.maximum((b * nblk + n) * per_blk - 1, 0), col(which, hd)))
        in_specs += [cur(0), prev(1), cur(1), prev(2), cur(2)]
    return pl.pallas_call(
        _attn_prompt_kernel,
        grid=(bsz, nblk, A_HEADS),
        in_specs=in_specs,
        out_specs=pl.BlockSpec((ATTN_ROWS, A_HD), lambda b, n, hd: (b * nblk + n, hd)),
        out_shape=jax.ShapeDtypeStruct((bsz * t, A_WIDTH), F32),
        scratch_shapes=[pltpu.VMEM((ATTN_ROWS, A_HD), F32)] * (2 * N_GROUPS),
        compiler_params=_params("parallel", "arbitrary", "arbitrary"),
        name="attn_prompt",
    )(*([h] * (5 * N_GROUPS)))


SAMPLE_ROWS = 8


def _attn_sample_kernel(*refs, t_new):
    q_refs, k_refs, v_refs = refs[0:3], refs[3:6], refs[6:9]
    c_refs, o_ref = refs[9:12], refs[12]
    scale = 1.0 / math.sqrt(A_HD)
    kvh_n = 2 * A_HEADS
    n_seq = SAMPLE_ROWS // t_new
    n_q = A_HEADS * SAMPLE_ROWS
    n_c = SPAN * kvh_n
    n_l = n_c + LANES
    row = lax.broadcasted_iota(jnp.int32, (n_q, n_l), 0)
    lane = lax.broadcasted_iota(jnp.int32, (n_q, n_l), 1)
    h_r, seq_r, t_r = row // SAMPLE_ROWS, (row % SAMPLE_ROWS) // t_new, row % t_new
    in_buf = lane < n_c
    j_l = lane // kvh_n
    new = lane - n_c
    t_l = new % t_new
    own = (in_buf & ((lane % kvh_n) == h_r)) | (
        ~in_buf & ((new // SAMPLE_ROWS) == h_r) & (((new % SAMPLE_ROWS) // t_new) == seq_r))
    n_pad = LANES - kvh_n * SAMPLE_ROWS
    pad = [jnp.zeros((n_pad, A_HD), F32)] if n_pad else []
    heads = lambda ref: [ref[:, hd * A_HD:(hd + 1) * A_HD] for hd in range(A_HEADS)]
    bias = {}
    for seq in range(n_seq):
        dense = own & (seq_r == seq) & ((in_buf & (j_l >= t_r)) | (~in_buf & (t_l <= t_r)))
        bias[seq, None] = jnp.where(dense, 0.0, NEG)
        for t in range(t_new):
            dilated = own & (seq_r == seq) & (t_r == t) & (in_buf | (t_l == t))
            bias[seq, t] = jnp.where(dilated, 0.0, NEG)
    groups = range(N_GROUPS)
    kv2_all, s_all = [], []
    for g in groups:
        q = jnp.concatenate(heads(q_refs[g]), axis=0).astype(BF16)
        kv_new = jnp.concatenate(heads(k_refs[g]) + heads(v_refs[g]) + pad, axis=0)
        kv2, s = [], []
        for seq in range(n_seq):
            for t in ([None] if g == 0 else range(t_new)):
                buf = c_refs[g][seq, :, 0 if t is None else t].reshape(n_c, A_HD)
                kv2.append(jnp.concatenate([buf, kv_new], axis=0).astype(BF16))
                st = lax.dot_general(q, kv2[-1], (((1,), (1,)), ((), ())), preferred_element_type=F32)
                s.append(st * scale + bias[seq, t])
        kv2_all.append(kv2)
        s_all.append(s)
    m_all = [functools.reduce(jnp.maximum, [jnp.max(x, axis=-1, keepdims=True) for x in s]) for s in s_all]
    p_all = [[jnp.exp(x - m) for x in s] for s, m in zip(s_all, m_all)]
    l_all = [functools.reduce(jnp.add, [jnp.sum(x, axis=-1, keepdims=True) for x in p]) for p in p_all]
    shift = lambda x: jnp.concatenate(
        [pltpu.roll(x[:, 0:n_c], A_HEADS, 1), pltpu.roll(x[:, n_c:n_l], A_HEADS * SAMPLE_ROWS, 1)],
        axis=1)
    o_all = [functools.reduce(jnp.add, [jnp.dot(shift(x).astype(BF16), y, preferred_element_type=F32)
                                        for x, y in zip(p, kv2)]) for p, kv2 in zip(p_all, kv2_all)]
    outs = [o / l for o, l in zip(o_all, l_all)]
    lses = [m + jnp.log(l) for m, l in zip(m_all, l_all)]
    top = functools.reduce(jnp.maximum, lses)
    e = [jnp.exp(x - top) for x in lses]
    den = functools.reduce(jnp.add, e)
    o = functools.reduce(jnp.add, [(w / den) * x for w, x in zip(e, outs)])
    for hd in range(A_HEADS):
        o_ref[:, hd * A_HD:(hd + 1) * A_HD] = o[hd * SAMPLE_ROWS:(hd + 1) * SAMPLE_ROWS]


def _attn_sample(h, caches, bsz, t_new):
    kvh_n = 2 * A_HEADS
    n_seq = SAMPLE_ROWS // t_new
    assert SAMPLE_ROWS % t_new == 0 and bsz % n_seq == 0
    views = []
    specs = []
    for (win, dil), cache in zip(A_GROUPS, caches):
        assert cache.shape[1] == win and win // dil == SPAN and (dil == 1 or t_new <= dil)
        views.append(cache.reshape(bsz, SPAN, dil, kvh_n, A_HD))
        specs.append(pl.BlockSpec((n_seq, SPAN, min(dil, t_new), kvh_n, A_HD), lambda i: (i, 0, 0, 0, 0)))
    col = lambda which, g: pl.BlockSpec(
        (SAMPLE_ROWS, A_WIDTH), lambda i: (i, (_col_q, _col_k, _col_v)[which](g) // A_WIDTH))
    qkv_specs = [col(which, g) for which in range(3) for g in range(N_GROUPS)]
    return pl.pallas_call(
        functools.partial(_attn_sample_kernel, t_new=t_new),
        grid=(bsz // n_seq,),
        in_specs=qkv_specs + specs,
        out_specs=pl.BlockSpec((SAMPLE_ROWS, A_WIDTH), lambda i: (i, 0)),
        out_shape=jax.ShapeDtypeStruct((bsz * t_new, A_WIDTH), F32),
        compiler_params=_params("parallel"),
        name="attn_sample",
    )(*([h] * (3 * N_GROUPS)), *views)


def _final_kernel(x_ref, o_ref, bon_ref, zr_ref, oa_ref, zgg_ref, seg_ref, lnxg_ref, lnxb_ref, bg_ref,
                  woa_ref, wob_ref, wout_ref, lng_ref, lnb_ref, y_ref):
    seg2 = seg_ref[...]
    o = o_ref[...]
    mu = _seg_sum(o, seg2) * (1.0 / R_HD)
    d = o - mu
    var = _seg_sum(d * d, seg2) * (1.0 / R_HD)
    on = d * lax.rsqrt(var + GN_EPS) * lnxg_ref[...] + lnxb_ref[...]
    zr = zr_ref[...].astype(F32)
    y_r = (on + bon_ref[...].astype(F32)) * (zr * jax.nn.sigmoid(zr))

    z_a = zgg_ref[:, 0:A_WIDTH].astype(F32)
    y_a = oa_ref[...] * (z_a * jax.nn.sigmoid(z_a))

    g_r = zgg_ref[:, A_WIDTH:A_WIDTH + D_MODEL].astype(F32)
    g_a = zgg_ref[:, A_WIDTH + D_MODEL:A_WIDTH + 2 * D_MODEL].astype(F32)
    gate_r = jax.nn.sigmoid(g_r + bg_ref[:, 0:D_MODEL])
    gate_a = jax.nn.sigmoid(g_a + bg_ref[:, D_MODEL:2 * D_MODEL])
    mix = (gate_r * jnp.dot(y_r.astype(BF16), woa_ref[...], preferred_element_type=F32)
           + gate_a * jnp.dot(y_a.astype(BF16), wob_ref[...], preferred_element_type=F32))
    yy = ALPHA * x_ref[...] + jnp.dot(mix.astype(BF16), wout_ref[...], preferred_element_type=F32)
    mean = jnp.mean(yy, axis=-1, keepdims=True)
    cen = yy - mean
    variance = jnp.mean(cen * cen, axis=-1, keepdims=True)
    y_ref[...] = cen * lax.rsqrt(variance + LN_EPS) * lng_ref[...] + lnb_ref[...]


def _final(x, o, bonus, h16, o_attn, weights, tm):
    m = x.shape[0]
    row = lambda w, j=0: pl.BlockSpec((tm, w), lambda i: (i, j))
    const = lambda a: pl.BlockSpec(a.shape, lambda i: (0, 0))
    in_specs = ([row(D_MODEL), row(R_WIDTH), row(R_WIDTH), row(R_WIDTH, COL_ZR // R_WIDTH), row(A_WIDTH),
                 row(ZGG_W, COL_ZGG // ZGG_W)] + [const(a) for a in weights])
    return pl.pallas_call(
        _final_kernel,
        grid=(m // tm,),
        in_specs=in_specs,
        out_specs=row(D_MODEL),
        out_shape=jax.ShapeDtypeStruct((m, D_MODEL), F32),
        compiler_params=_params("parallel"),
        name="merge_out_ln",
    )(x, o, bonus, h16, o_attn, h16, *weights)


def _layer(x2, bsz, t, shift_prev, wkv0, caches, wts):
    m = x2.shape[0]
    h, h16 = _in_proj(x2, wts["w_proj"], 2048 if m % 2048 == 0 else m)

    tm = 1024 if m % 1024 == 0 and t % 1024 == 0 else (512 if m % 512 == 0 else m)
    if caches is None:
        first = None
    else:
        first = jnp.zeros((bsz, t, SHIFT_COLS), F32).at[:, 0].set(shift_prev).reshape(m, SHIFT_COLS)
    r, kp, v, kk, b, lw, bonus = _rwkv_prep(h, first, t, wts["prep"], tm)

    if t % CHUNK == 0 and wkv0 is None:
        seq = lambda a: a.reshape(bsz, t, R_WIDTH)
        s0 = jnp.zeros((bsz, N_PAIRS, LANES, LANES), F32)
        o_seq, st = _rwkv_recurrence(seq(r), seq(kp), seq(v), seq(kk), seq(b), seq(lw), s0, 2)
        o_rwkv = o_seq.reshape(m, R_WIDTH)
        wkv_new = _pairs_to_state(st)
    else:
        o_rwkv, wkv_new = _rwkv_short(r, kp, v, kk, b, lw, wkv0, t)

    def rows_cols(r0, c0, width):
        if t % 8 == 0:
            return h.reshape(bsz, t, N_PROJ32)[:, r0:, c0:c0 + width]
        return h[:, c0:c0 + width].reshape(bsz, t, width)[:, r0:]

    if caches is None:
        o_attn = _attn_prompt(h, bsz, t)
    else:
        o_attn = _attn_sample(h, caches, bsz, t)

    tmf = 512 if m % 512 == 0 else m
    y = _final(x2, o_rwkv, bonus, h16, o_attn, wts["final"], tmf)

    new_kv = []
    for g, (win, _) in enumerate(A_GROUPS):
        keep = min(win, t) if caches is None else t
        new_kv.append(rows_cols(t - keep, _col_k(g), 2 * A_WIDTH).reshape(bsz, keep, 2, A_HEADS, A_HD))
    shift_new = rows_cols(t - 1, COL_ZS, SHIFT_COLS)[:, 0]
    return y.reshape(bsz, t, D_MODEL), new_kv, wkv_new, shift_new


def kernel(x_prompt, x_sample, cache_kv_g1, cache_kv_g2, cache_kv_g3, state_rwkv_wkv, state_rwkv_shift,
           w_in, b_gate, mu_shift, w0, w_w2, a0, w_a2, k_k, k_a, r_k, lnx_g, lnx_b,
           w_oa, w_ob, w_out, ln_g, ln_b):
    assert w_in.shape[0] == DEPTH
    bp, tp, _ = x_prompt.shape
    bs, ts, _ = x_sample.shape
    xp = x_prompt.reshape(bp * tp, D_MODEL)
    xs = x_sample.reshape(bs * ts, D_MODEL)
    head = lax.broadcasted_iota(jnp.int32, (LANES, LANES), 0) // R_HD
    seg = (head == head.T).astype(BF16)
    seg2 = jnp.concatenate([seg, seg], axis=0)
    row = lambda a: a.reshape(1, -1)
    acc = [[] for _ in range(12)]
    for l in range(DEPTH):
        wf = w_in[l]
        c0 = SHIFT_COLS
        c1 = c0 + R_WIDTH
        c2 = c1 + 3 * A_QKV_W
        qkv_col = lambda which, g: wf[:, c1 + (which * N_GROUPS + g) * A_WIDTH:
                                      c1 + (which * N_GROUPS + g + 1) * A_WIDTH]
        w_proj = jnp.concatenate(
            [wf[:, :c0], jnp.zeros((D_MODEL, COL_Q - c0), F32)] + [qkv_col(0, g) for g in range(N_GROUPS)]
            + [qkv_col(which, g) for g in range(N_GROUPS) for which in (1, 2)]
            + [wf[:, c2:], wf[:, c0:c1]], axis=1).astype(BF16)
        zero = jnp.zeros((LORA, R_WIDTH), F32)
        w_lora = jnp.concatenate([jnp.concatenate([w_w2[l], zero], axis=1),
                                  jnp.concatenate([zero, w_a2[l]], axis=1)], axis=0)
        wl_hi = w_lora.astype(BF16)
        wl_lo = (w_lora - wl_hi.astype(F32)).astype(BF16)
        wts = {
            "w_proj": w_proj,
            "prep": (row(mu_shift[l]), row(w0[l]), row(a0[l]), jnp.concatenate([wl_hi, wl_hi], axis=0),
                     wl_lo, row(k_k[l]), row(k_a[l]), row(r_k[l]), seg2),
            "final": (seg2, row(lnx_g[l]), row(lnx_b[l]), row(b_gate[l]), w_oa[l].astype(BF16),
                      w_ob[l].astype(BF16), w_out[l].astype(BF16), row(ln_g[l]), row(ln_b[l])),
        }
        yp, kv_p, s_p, sh_p = _layer(xp, bp, tp, None, None, None, wts)
        ys, kv_s, s_s, sh_s = _layer(xs, bs, ts, state_rwkv_shift[l], state_rwkv_wkv[l],
                                     (cache_kv_g1[l], cache_kv_g2[l], cache_kv_g3[l]), wts)
        xp = yp.reshape(bp * tp, D_MODEL)
        xs = ys.reshape(bs * ts, D_MODEL)
        for idx, val in zip(range(2, 12), (kv_p[0], kv_s[0], kv_p[1], kv_s[1], kv_p[2], kv_s[2],
                                           s_p, s_s, sh_p, sh_s)):
            acc[idx].append(val)
    outs = [xp.reshape(bp, tp, D_MODEL), xs.reshape(bs, ts, D_MODEL)]
    outs += [jnp.stack(a) for a in acc[2:]]
    return tuple(outs)
```

```python
import functools
import math

import jax
import jax.numpy as jnp
from jax import lax
from jax.experimental import pallas as pl
from jax.experimental.pallas import tpu as pltpu

F32 = jnp.float32
BF16 = jnp.bfloat16

D_MODEL = 1024
R_HEADS = 8
R_HD = 64
R_WIDTH = R_HEADS * R_HD
LORA = 64
SHIFT_COLS = 3 * R_WIDTH + 2 * LORA
GN_EPS = 64e-5
A_GROUPS = ((128, 1), (512, 4), (2048, 16))
N_GROUPS = 3
A_HEADS = 4
A_HD = 128
A_WIDTH = A_HEADS * A_HD
A_QKV_W = N_GROUPS * A_WIDTH
SPAN = 128
DEPTH = 1
ALPHA = (2 * DEPTH) ** 0.25
LN_EPS = 1e-5
NEG = -1e30

LANES = 128
CHUNK = 64
N_PAIRS = R_HEADS // 2
VMEM_LIMIT = 48 * 1024 * 1024

PROJ_TN = 512
COL_ZS = 0
COL_Q = 2048
COL_KV = COL_Q + A_QKV_W
N_PROJ32 = COL_KV + 2 * A_QKV_W
ZGG_W = A_WIDTH + 2 * D_MODEL
COL_ZGG = 0
COL_ZR = ZGG_W
N_PROJ16 = COL_ZR + R_WIDTH
assert COL_Q % A_WIDTH == 0 and COL_KV % A_WIDTH == 0 and COL_ZR % R_WIDTH == 0
assert N_PROJ32 % PROJ_TN == 0 and N_PROJ16 % PROJ_TN == 0


def _col_q(g):
    return COL_Q + g * A_WIDTH


def _col_k(g):
    return COL_KV + g * 2 * A_WIDTH


def _col_v(g):
    return _col_k(g) + A_WIDTH


def _params(*sem):
    return pltpu.CompilerParams(dimension_semantics=sem, vmem_limit_bytes=VMEM_LIMIT)


def _split2(x):
    hi = x.astype(BF16)
    return hi, (x - hi.astype(F32)).astype(BF16)


def _seg_sum(x, seg2):
    cols = []
    for c in range(x.shape[1] // LANES):
        hi, lo = _split2(x[:, c * LANES:(c + 1) * LANES])
        cols.append(jnp.dot(jnp.concatenate([hi, lo], axis=1), seg2, preferred_element_type=F32))
    return jnp.concatenate(cols, axis=1)


def _mm_kernel(x_ref, w_ref, o32_ref, o16_ref, *, n32):
    j = pl.program_id(1)
    product = lambda: jnp.dot(x_ref[...].astype(BF16), w_ref[...], preferred_element_type=F32)

    @pl.when(j < n32)
    def _():
        o32_ref[...] = product()

    @pl.when(j >= n32)
    def _():
        o16_ref[...] = product().astype(BF16)


def _in_proj(x, w, tm):
    m, k = x.shape
    n32 = N_PROJ32 // PROJ_TN
    n16 = N_PROJ16 // PROJ_TN
    return pl.pallas_call(
        functools.partial(_mm_kernel, n32=n32),
        grid=(m // tm, n32 + n16),
        in_specs=[pl.BlockSpec((tm, k), lambda i, j: (i, 0)),
                  pl.BlockSpec((k, PROJ_TN), lambda i, j: (0, j))],
        out_specs=[pl.BlockSpec((tm, PROJ_TN), lambda i, j: (i, jnp.minimum(j, n32 - 1))),
                   pl.BlockSpec((tm, PROJ_TN), lambda i, j: (i, jnp.maximum(j - n32, 0)))],
        out_shape=[jax.ShapeDtypeStruct((m, N_PROJ32), F32), jax.ShapeDtypeStruct((m, N_PROJ16), BF16)],
        compiler_params=_params("parallel", "arbitrary"),
        name="in_proj",
    )(x, w)


PACK_R, PACK_K, PACK_V, PACK_KK, PACK_B = (i * R_WIDTH for i in range(5))
PACK_W = 5 * R_WIDTH


def _prep_math(zs, prev, mu, w0, a0, wl_a, wl_b, k_k, k_a, r_k, seg2, outs):
    pack_o, lw_o, bon_o = outs
    zsm = zs + mu * (prev - zs)
    r = zsm[:, 0:R_WIDTH]
    k = zsm[:, R_WIDTH:2 * R_WIDTH]
    v = zsm[:, 2 * R_WIDTH:3 * R_WIDTH]
    slab = zsm[:, 3 * R_WIDTH:SHIFT_COLS]
    lane = lax.broadcasted_iota(jnp.int32, slab.shape, 1)
    slab = jnp.where(lane < LORA, jnp.tanh(slab), slab)
    s_hi, s_lo = _split2(slab)
    lin = (jnp.dot(jnp.concatenate([s_hi, s_lo], axis=1), wl_a, preferred_element_type=F32)
           + jnp.dot(s_hi, wl_b, preferred_element_type=F32))
    wlin = w0 + lin[:, 0:R_WIDTH]
    alin = a0 + lin[:, R_WIDTH:2 * R_WIDTH]
    nw = -wlin
    softplus = jnp.maximum(nw, 0.0) + jnp.log1p(jnp.exp(-jnp.abs(nw)))
    w = -softplus - 0.5
    lw = -jnp.exp(w)
    a = jax.nn.sigmoid(alin)
    kkr = k * k_k
    kk = kkr / jnp.maximum(jnp.sqrt(_seg_sum(kkr * kkr, seg2)), 1e-12)
    kp = k * (1.0 + (a - 1.0) * k_a)
    for slot, val in enumerate((r, kp, v, kk, kk * a)):
        pack_o[:, slot * R_WIDTH:(slot + 1) * R_WIDTH] = val.astype(pack_o.dtype)
    lw_o[...] = lw
    bon_o[...] = (_seg_sum(r * kp * r_k, seg2) * v).astype(bon_o.dtype)


def _prep_prompt_kernel(zs_ref, p8_ref, mu_ref, w0_ref, a0_ref, wla_ref, wlb_ref, kk_ref, ka_ref,
                        rk_ref, seg_ref, *outs, tiles_per_seq):
    i = pl.program_id(0)
    zs = zs_ref[...]
    rolled = pltpu.roll(zs, 1, 0)
    carry = jnp.where(i % tiles_per_seq == 0, 0.0, p8_ref[7:8, :])
    row = lax.broadcasted_iota(jnp.int32, zs.shape, 0)
    prev = jnp.where(row == 0, carry, rolled)
    _prep_math(zs, prev, mu_ref[...], w0_ref[...], a0_ref[...], wla_ref[...], wlb_ref[...],
               kk_ref[...], ka_ref[...], rk_ref[...], seg_ref[...], outs)


def _prep_sample_kernel(zs_ref, first_ref, mu_ref, w0_ref, a0_ref, wla_ref, wlb_ref, kk_ref, ka_ref,
                        rk_ref, seg_ref, *outs, seq_len):
    zs = zs_ref[...]
    rolled = pltpu.roll(zs, 1, 0)
    row = lax.broadcasted_iota(jnp.int32, zs.shape, 0)
    prev = jnp.where(row % seq_len == 0, first_ref[...], rolled)
    _prep_math(zs, prev, mu_ref[...], w0_ref[...], a0_ref[...], wla_ref[...], wlb_ref[...],
               kk_ref[...], ka_ref[...], rk_ref[...], seg_ref[...], outs)


def _rwkv_prep(h, first, seq_len, weights, tm):
    m = h.shape[0]
    const = lambda a: pl.BlockSpec(a.shape, lambda i: (0, 0))
    row_spec = pl.BlockSpec((tm, SHIFT_COLS), lambda i: (i, COL_ZS))
    if first is None:
        kern = functools.partial(_prep_prompt_kernel, tiles_per_seq=seq_len // tm)
        second = h
        second_spec = pl.BlockSpec((8, SHIFT_COLS), lambda i: (jnp.maximum(i * (tm // 8) - 1, 0), COL_ZS))
    else:
        kern = functools.partial(_prep_sample_kernel, seq_len=seq_len)
        second = first
        second_spec = pl.BlockSpec((tm, SHIFT_COLS), lambda i: (i, 0))
    out_spec = lambda w: pl.BlockSpec((tm, w), lambda i: (i, 0))
    return pl.pallas_call(
        kern,
        grid=(m // tm,),
        in_specs=[row_spec, second_spec] + [const(a) for a in weights],
        out_specs=[out_spec(PACK_W), out_spec(R_WIDTH), out_spec(R_WIDTH)],
        out_shape=[jax.ShapeDtypeStruct((m, PACK_W), BF16), jax.ShapeDtypeStruct((m, R_WIDTH), F32),
                   jax.ShapeDtypeStruct((m, R_WIDTH), BF16)],
        compiler_params=_params("parallel"),
        name="rwkv_prep",
    )(h, second, *weights)


CHUNKS_PER_STEP = 2


def _rwkv_chunk_kernel(*refs, nb):
    ins, s0_ref, o_ref, st_ref = refs[0:6], refs[6], refs[7], refs[8]
    yo_sc, ti_sc, mb_sc, ut_sc, vs_sc, dec_sc = refs[9:15]
    n = pl.program_id(1)
    c = CHUNK
    c2 = 2 * c
    lane = lax.broadcasted_iota(jnp.int32, (1, LANES), 1)
    m_lo = (lane < R_HD).astype(F32)
    m_hi = 1.0 - m_lo
    row_c = lax.broadcasted_iota(jnp.int32, (c, 4 * c), 0)
    col_c = lax.broadcasted_iota(jnp.int32, (c, 4 * c), 1)
    tri3 = ((row_c >= col_c % c) & (col_c < 3 * c)).astype(BF16)
    row2 = lax.broadcasted_iota(jnp.int32, (c2, c2), 0)
    col2 = lax.broadcasted_iota(jnp.int32, (c2, c2), 1)
    strict = row2 > col2
    incl = row2 >= col2
    eye = (row2 == col2).astype(F32)

    def stack(x):
        return jnp.concatenate([x * m_lo, x * m_hi], axis=0)

    def mm(a, b):
        return jnp.dot(a.astype(BF16), b.astype(BF16), preferred_element_type=F32)

    def nt(a, b):
        return lax.dot_general(a.astype(BF16), b.astype(BF16), (((1,), (1,)), ((), ())),
                               preferred_element_type=F32)

    each = lambda f, *xs: [f(*a) for a in zip(*xs)]
    chains = [(bi, slice(p * LANES, (p + 1) * LANES), p) for bi in range(nb) for p in range(N_PAIRS)]

    def cumsum(lw_all):
        hi = lw_all.astype(BF16)
        r1 = lw_all - hi.astype(F32)
        mid = r1.astype(BF16)
        lo = (r1 - mid.astype(F32)).astype(BF16)
        return jnp.dot(tri3, jnp.concatenate([hi, mid, lo, jnp.zeros_like(lo)], axis=0),
                       preferred_element_type=F32)

    def prepare(refs, row0, slot):
        pack_ref, lw_ref = refs
        rs = slice(row0, row0 + c)
        g_all = [cumsum(lw_ref[bi, rs, :]) for bi in range(nb)]
        yield
        load = lambda c0: [pack_ref[bi, rs, c0 + sl.start:c0 + sl.stop].astype(F32) for bi, sl, _ in chains]
        r, kp, v, kk, b = (load(c0) for c0 in (PACK_R, PACK_K, PACK_V, PACK_KK, PACK_B))
        lw = [lw_ref[bi, rs, sl] for bi, sl, _ in chains]
        g = [g_all[bi][:, sl] for bi, sl, _ in chains]
        g_last = each(lambda x: x[c - 1:c, :], g)
        e_g = each(jnp.exp, g)
        e_ng = each(lambda x: jnp.exp(-x), g)
        e_gm = each(lambda x, y: jnp.exp(x - y), g, lw)
        e_cg = each(lambda x, y: jnp.exp(y - x), g, g_last)
        ad_s = each(lambda x, y: stack(-(x * y)), kk, e_gm)
        rd_s = each(lambda x, y: stack(x * y), r, e_g)
        bd_s = each(lambda x, y: stack(x * y), b, e_ng)
        kd_s = each(lambda x, y: stack(x * y), kp, e_ng)
        be_s = each(lambda x, y: stack(x * y), b, e_cg)
        ke_s = each(lambda x, y: stack(x * y), kp, e_cg)
        for idx in range(len(chains)):
            yo_sc[slot, idx, 0:c2, 0:c2] = ad_s[idx].astype(BF16)
            yo_sc[slot, idx, c2:2 * c2, 0:c2] = rd_s[idx].astype(BF16)
            ut_sc[slot, idx, :, 0:c2] = jnp.transpose(be_s[idx]).astype(BF16)
            ut_sc[slot, idx, :, c2:2 * c2] = jnp.transpose(ke_s[idx]).astype(BF16)
            vs_sc[slot, idx] = stack(v[idx]).astype(BF16)
            dec_sc[slot, idx] = jnp.transpose(jnp.broadcast_to(jnp.exp(g_last[idx]), (LANES, LANES)))
        yield
        z = each(lambda a1, a2, b1, b2: nt(jnp.concatenate([a1, a2], axis=0),
                                           jnp.concatenate([b1, b2], axis=0)), ad_s, rd_s, bd_s, kd_s)
        yield
        for idx, zz in enumerate(z):
            yo_sc[slot, idx, 0:c2, c2:2 * c2] = jnp.where(strict, zz[0:c2, c2:2 * c2], 0.0).astype(BF16)
            yo_sc[slot, idx, c2:2 * c2, c2:2 * c2] = jnp.where(
                incl, zz[c2:2 * c2, c2:2 * c2], 0.0).astype(BF16)
            mb_sc[slot, idx] = jnp.where(incl, zz[c2:2 * c2, 0:c2], 0.0).astype(BF16)
        n_b = each(lambda x: jnp.where(strict, x[0:c2, 0:c2], 0.0), z)
        t_inv = each(lambda x: eye + x, n_b)
        q = each(lambda x: mm(x, x), n_b)
        yield
        steps = int(math.log2(c))
        for i in range(1, steps):
            if i < steps - 1:
                res = each(lambda x, tt: mm(x, jnp.concatenate([x, tt], axis=1)), q, t_inv)
                q = each(lambda x: x[:, 0:c2], res)
                t_inv = each(lambda tt, x: tt + x[:, c2:2 * c2], t_inv, res)
            else:
                t_inv = each(lambda tt, x: tt + mm(x, tt), t_inv, q)
            yield
        for idx, tt in enumerate(t_inv):
            ti_sc[slot, idx] = tt.astype(BF16)

    def advance(slot, row0):
        idxs = list(range(len(chains)))
        st = [st_ref[bi, p] for bi, _, p in chains]
        vs = [vs_sc[slot, i] for i in idxs]
        yo = [jnp.dot(yo_sc[slot, i], jnp.concatenate([s.astype(BF16), w], axis=0),
                      preferred_element_type=F32) for i, s, w in zip(idxs, st, vs)]
        yield
        u = [jnp.dot(ti_sc[slot, i], x[0:c2].astype(BF16), preferred_element_type=F32)
             for i, x in zip(idxs, yo)]
        ub = [x.astype(BF16) for x in u]
        yield
        o_bd = [x[c2:2 * c2] + jnp.dot(mb_sc[slot, i], w, preferred_element_type=F32)
                for i, x, w in zip(idxs, yo, ub)]
        for (bi, sl, _), o in zip(chains, o_bd):
            o_ref[bi, row0:row0 + c, sl] = o[0:c] + o[c:c2]
        yield
        upd = [jnp.dot(ut_sc[slot, i], jnp.concatenate([w, x], axis=0), preferred_element_type=F32)
               for i, w, x in zip(idxs, ub, vs)]
        for i, ((bi, _, p), s, d) in enumerate(zip(chains, st, upd)):
            st_ref[bi, p] = dec_sc[slot, i] * s + d

    def interleave(*stages):
        live = list(stages)
        while live:
            for gen in list(live):
                if next(gen, StopIteration) is StopIteration:
                    live.remove(gen)

    @pl.when(n == 0)
    def _():
        st_ref[...] = s0_ref[...]
        interleave(prepare(ins[0:2], 0, 0))

    interleave(advance(0, 0), prepare(ins[2:4], 0, 1))
    interleave(advance(1, c), prepare(ins[4:6], 0, 0))


def _rwkv_recurrence(pack, lw, s0, nb):
    bsz, t, _ = lw.shape
    assert CHUNKS_PER_STEP == 2
    rows = CHUNKS_PER_STEP * CHUNK
    nsteps = t // rows
    last = t // CHUNK - 1
    chunk = lambda which: [pl.BlockSpec((nb, CHUNK, w), lambda i, n: (i, which(n), 0))
                           for w in (PACK_W, R_WIDTH)]
    chunk_specs = (chunk(lambda n: 0) + chunk(lambda n: 2 * n + 1)
                   + chunk(lambda n: jnp.minimum(2 * n + 2, last)))
    seq_spec = pl.BlockSpec((nb, rows, R_WIDTH), lambda i, n: (i, n, 0))
    st_spec = pl.BlockSpec((nb, N_PAIRS, LANES, LANES), lambda i, n: (i, 0, 0, 0))
    nchain = nb * N_PAIRS
    c2 = 2 * CHUNK
    slots = 2
    scratch = [pltpu.VMEM((slots, nchain, 2 * c2, 2 * c2), BF16),
               pltpu.VMEM((slots, nchain, c2, c2), BF16),
               pltpu.VMEM((slots, nchain, c2, c2), BF16),
               pltpu.VMEM((slots, nchain, c2, 2 * c2), BF16),
               pltpu.VMEM((slots, nchain, c2, c2), BF16),
               pltpu.VMEM((slots, nchain, LANES, LANES), F32)]
    return pl.pallas_call(
        functools.partial(_rwkv_chunk_kernel, nb=nb),
        grid=(bsz // nb, nsteps),
        in_specs=chunk_specs + [st_spec],
        out_specs=[seq_spec, st_spec],
        out_shape=[jax.ShapeDtypeStruct((bsz, t, R_WIDTH), F32),
                   jax.ShapeDtypeStruct(s0.shape, F32)],
        scratch_shapes=scratch,
        compiler_params=_params("parallel", "arbitrary"),
        name="rwkv_chunks",
    )(*([pack, lw] * 3), s0)


def _pairs_to_state(st):
    bsz = st.shape[0]
    s6 = st.reshape(bsz, N_PAIRS, 2, R_HD, 2, R_HD)
    s = jnp.stack([s6[:, :, 0, :, 0, :], s6[:, :, 1, :, 1, :]], axis=2)
    return jnp.swapaxes(s.reshape(bsz, R_HEADS, R_HD, R_HD), -1, -2)


SHORT_ROWS = 64


def _rwkv_short_kernel(pack_ref, lw_ref, s_ref, o_ref, sn_ref, *, t_len):
    rows = SHORT_ROWS
    ng = rows // t_len
    r2 = 2 * rows
    assert 2 * t_len == 8 and r2 == LANES
    lane = lax.broadcasted_iota(jnp.int32, (1, LANES), 1)
    m_lo = (lane < R_HD).astype(F32)
    m_hi = 1.0 - m_lo
    lane_blk = lane // t_len
    row_c = lax.broadcasted_iota(jnp.int32, (rows, 4 * rows), 0)
    col_c = lax.broadcasted_iota(jnp.int32, (rows, 4 * rows), 1)
    j_c = col_c % rows
    same_c = ((row_c // t_len) == (j_c // t_len)) & (col_c < 3 * rows)
    cum_lhs = jnp.concatenate([(same_c & (row_c >= j_c)).astype(BF16), same_c.astype(BF16)], axis=0)
    row2 = lax.broadcasted_iota(jnp.int32, (r2, r2), 0)
    col2 = lax.broadcasted_iota(jnp.int32, (r2, r2), 1)
    blk = (row2 // t_len) == (col2 // t_len)
    strict = blk & (row2 > col2)
    incl = blk & (row2 >= col2)
    first_seq = lax.broadcasted_iota(jnp.int32, (8, LANES), 0) < t_len
    zero_half = jnp.zeros((R_HD, R_HD), F32)

    def stack(x):
        return jnp.concatenate([x * m_lo, x * m_hi], axis=0)

    def half(x):
        return jnp.concatenate([x * m_lo, pltpu.roll(x, R_HD, 1) * m_lo], axis=0)

    def mm(a, b):
        return jnp.dot(a.astype(BF16), b.astype(BF16), preferred_element_type=F32)

    def nt(a, b):
        return lax.dot_general(a.astype(BF16), b.astype(BF16), (((1,), (1,)), ((), ())),
                               preferred_element_type=F32)

    each = lambda f, *xs: [f(*a) for a in zip(*xs)]

    lw_all = lw_ref[...]
    hi = lw_all.astype(BF16)
    r1 = lw_all - hi.astype(F32)
    mid = r1.astype(BF16)
    lo = (r1 - mid.astype(F32)).astype(BF16)
    g_both = jnp.dot(cum_lhs, jnp.concatenate([hi, mid, lo, jnp.zeros_like(lo)], axis=0),
                     preferred_element_type=F32)

    pairs = list(range(N_PAIRS))
    sls = [slice(p * LANES, (p + 1) * LANES) for p in pairs]
    load = lambda c0: [pack_ref[:, c0 + sl.start:c0 + sl.stop].astype(F32) for sl in sls]
    r, kp, v, kk, b = (load(c0) for c0 in (PACK_R, PACK_K, PACK_V, PACK_KK, PACK_B))
    lw = [lw_ref[:, sl] for sl in sls]
    g = [g_both[0:rows, sl] for sl in sls]
    g_tot = [g_both[rows:r2, sl] for sl in sls]
    e_g = each(jnp.exp, g)
    e_ng = each(lambda x: jnp.exp(-x), g)
    e_gm = each(lambda x, y: jnp.exp(x - y), g, lw)
    e_cg = each(lambda x, y: jnp.exp(y - x), g, g_tot)
    ad = each(lambda x, y: -(x * y), kk, e_gm)
    rd = each(lambda x, y: x * y, r, e_g)
    bd = each(lambda x, y: x * y, b, e_ng)
    kd = each(lambda x, y: x * y, kp, e_ng)
    be_h = each(lambda x, y: half(x * y), b, e_cg)
    ke_h = each(lambda x, y: half(x * y), kp, e_cg)
    v_h = each(half, v)
    ad_h = each(half, ad)
    rd_h = each(half, rd)
    dec_h = each(lambda x: half(jnp.exp(x)), g_tot)

    z = each(lambda a1, a2, b1, b2: nt(jnp.concatenate([stack(a1), stack(a2)], axis=0),
                                       jnp.concatenate([stack(b1), stack(b2)], axis=0)), ad, rd, bd, kd)
    n_b = each(lambda x: jnp.where(strict, x[0:r2, 0:r2], 0.0), z)
    n_k = each(lambda x: jnp.where(strict, x[0:r2, r2:2 * r2], 0.0), z)
    m_b = each(lambda x: jnp.where(incl, x[r2:2 * r2, 0:r2], 0.0), z)
    m_k = each(lambda x: jnp.where(incl, x[r2:2 * r2, r2:2 * r2], 0.0), z)

    y0, o0 = [], []
    for p in pairs:
        y_tiles, o_tiles = [], []
        for hh in range(2):
            for i in range(ng // 2):
                rs = slice(hh * rows + 8 * i, hh * rows + 8 * i + 8)
                lhs = jnp.concatenate([ad_h[p][rs], rd_h[p][rs]], axis=0)[:, 0:R_HD]
                res = [nt(lhs, jnp.concatenate([s_ref[2 * i + w, 2 * p + hh], zero_half], axis=0))
                       for w in range(2)]
                y_tiles.append(jnp.where(first_seq, res[0][0:8], res[1][0:8]))
                o_tiles.append(jnp.where(first_seq, res[0][8:16], res[1][8:16]))
        y0.append(jnp.concatenate(y_tiles, axis=0))
        o0.append(jnp.concatenate(o_tiles, axis=0))

    y = each(lambda a, nk, vh: a + mm(nk, vh), y0, n_k, v_h)
    q = n_b
    steps = int(math.log2(t_len))
    for i in range(steps):
        if i < steps - 1:
            res = each(lambda x, yy: mm(x, jnp.concatenate([x, yy], axis=1)), q, y)
            q = each(lambda x: x[:, 0:r2], res)
            y = each(lambda yy, x: yy + x[:, r2:2 * r2], y, res)
        else:
            y = each(lambda yy, x: yy + mm(x, yy), y, q)
    u = y

    o_h = each(lambda a, mb, uu, mk, vh: a + mm(mb, uu) + mm(mk, vh), o0, m_b, u, m_k, v_h)
    for sl, o in zip(sls, o_h):
        o_ref[:, sl] = o[0:rows] + pltpu.roll(o[rows:r2], R_HD, 1)

    for p in pairs:
        ut = jnp.transpose(u[p])[0:R_HD]
        vt = jnp.transpose(v_h[p])[0:R_HD]
        rhs = jnp.concatenate([be_h[p], ke_h[p]], axis=0)
        keys = [(hh, s) for hh in range(2) for s in range(ng)]
        sel = [(lane_blk == hh * ng + s).astype(F32) for hh, s in keys]
        lhs = jnp.concatenate([jnp.concatenate([ut * m, vt * m], axis=1) for m in sel], axis=0)
        upd = mm(lhs, rhs)
        for idx, (hh, s) in enumerate(keys):
            row0 = hh * rows + s * t_len
            dec = dec_h[p][row0:row0 + 1, 0:R_HD]
            sn_ref[s, 2 * p + hh] = (s_ref[s, 2 * p + hh] * dec
                                     + upd[idx * R_HD:(idx + 1) * R_HD, 0:R_HD])


def _rwkv_short(pack, lw, state, t_len):
    m = lw.shape[0]
    ng = SHORT_ROWS // t_len
    seq_spec = pl.BlockSpec((SHORT_ROWS, R_WIDTH), lambda i: (i, 0))
    pack_spec = pl.BlockSpec((SHORT_ROWS, PACK_W), lambda i: (i, 0))
    st_spec = pl.BlockSpec((ng, R_HEADS, R_HD, R_HD), lambda i: (i, 0, 0, 0))
    return pl.pallas_call(
        functools.partial(_rwkv_short_kernel, t_len=t_len),
        grid=(m // SHORT_ROWS,),
        in_specs=[pack_spec, seq_spec, st_spec],
        out_specs=[seq_spec, st_spec],
        out_shape=[jax.ShapeDtypeStruct((m, R_WIDTH), F32), jax.ShapeDtypeStruct(state.shape, F32)],
        compiler_params=_params("parallel"),
        name="rwkv_short",
    )(pack, lw, state)


ATTN_ROWS = SPAN * max(d for _, d in A_GROUPS)
ATTN_BATCH = 8


def _attn_prompt_kernel(*refs):
    ins = refs[:5 * N_GROUPS]
    o_ref = refs[5 * N_GROUPS]
    o_sc = refs[5 * N_GROUPS + 1:5 * N_GROUPS + 1 + N_GROUPS]
    l_sc = refs[5 * N_GROUPS + 1 + N_GROUPS:]
    n = pl.program_id(1)
    i = lax.broadcasted_iota(jnp.int32, (SPAN, 2 * SPAN), 0)
    j = lax.broadcasted_iota(jnp.int32, (SPAN, 2 * SPAN), 1)
    dist = SPAN + i - j
    band = (dist >= 0) & (dist <= SPAN)
    band_first = band & ((n > 0) | (j >= SPAN))
    scale = 1.0 / math.sqrt(A_HD)
    cat = lambda a, b: jnp.concatenate([a, b], axis=0).astype(BF16)

    for g, (_, dil) in enumerate(A_GROUPS):
        q_ref, kp_ref, kc_ref, vp_ref, vc_ref = ins[5 * g:5 * g + 5]

        def rows(run, rho):
            start = run * SPAN * dil + rho
            return pl.ds(start, SPAN, stride=dil) if dil > 1 else pl.ds(start, SPAN)

        all_probs = [(run, rho) for run in range(ATTN_ROWS // (SPAN * dil)) for rho in range(dil)]
        for b0 in range(0, len(all_probs), ATTN_BATCH):
            probs = all_probs[b0:b0 + ATTN_BATCH]
            q = [q_ref[rows(run, rho), :].astype(BF16) for run, rho in probs]
            k = [cat(kp_ref[rows(0, rho), :] if run == 0 else kc_ref[rows(run - 1, rho), :],
                     kc_ref[rows(run, rho), :]) for run, rho in probs]
            v = [cat(vp_ref[rows(0, rho), :] if run == 0 else vc_ref[rows(run - 1, rho), :],
                     vc_ref[rows(run, rho), :]) for run, rho in probs]
            s = [lax.dot_general(a, b, (((1,), (1,)), ((), ())), preferred_element_type=F32) * scale
                 for a, b in zip(q, k)]
            s = [jnp.where(band_first if run == 0 else band, x, NEG) for x, (run, _) in zip(s, probs)]
            m = [jnp.max(x, axis=-1, keepdims=True) for x in s]
            p = [jnp.exp(x - y) for x, y in zip(s, m)]
            l = [jnp.sum(x, axis=-1, keepdims=True) for x in p]
            o = [jnp.dot(x.astype(BF16), y, preferred_element_type=F32) for x, y in zip(p, v)]
            for (run, rho), oo, ll, mx in zip(probs, o, l, m):
                o_sc[g][rows(run, rho), :] = oo / ll
                l_sc[g][rows(run, rho), :] = jnp.broadcast_to(mx + jnp.log(ll), (SPAN, A_HD))

    lse = [ref[...] for ref in l_sc]
    top = functools.reduce(jnp.maximum, lse)
    e = [jnp.exp(x - top) for x in lse]
    den = functools.reduce(jnp.add, e)
    o_ref[...] = functools.reduce(jnp.add, [(w / den) * ref[...] for w, ref in zip(e, o_sc)])


def _attn_prompt(h, bsz, t):
    nblk = t // ATTN_ROWS
    in_specs = []
    for g, (_, dil) in enumerate(A_GROUPS):
        prev_rows = SPAN * dil
        per_blk = ATTN_ROWS // prev_rows
        col = lambda which, hd, g=g: (_col_q, _col_k, _col_v)[which](g) // A_HD + hd
        cur = lambda which, col=col: pl.BlockSpec(
            (ATTN_ROWS, A_HD), lambda b, n, hd: (b * nblk + n, col(which, hd)))
        prev = lambda which, col=col, per_blk=per_blk, prev_rows=prev_rows: pl.BlockSpec(
            (prev_rows, A_HD),
            lambda b, n, hd: (jnp.maximum((b * nblk + n) * per_blk - 1, 0), col(which, hd)))
        in_specs += [cur(0), prev(1), cur(1), prev(2), cur(2)]
    return pl.pallas_call(
        _attn_prompt_kernel,
        grid=(bsz, nblk, A_HEADS),
        in_specs=in_specs,
        out_specs=pl.BlockSpec((ATTN_ROWS, A_HD), lambda b, n, hd: (b * nblk + n, hd)),
        out_shape=jax.ShapeDtypeStruct((bsz * t, A_WIDTH), F32),
        scratch_shapes=[pltpu.VMEM((ATTN_ROWS, A_HD), F32)] * (2 * N_GROUPS),
        compiler_params=_params("parallel", "arbitrary", "arbitrary"),
        name="attn_prompt",
    )(*([h] * (5 * N_GROUPS)))


SAMPLE_ROWS = 8


def _attn_sample_kernel(*refs, t_new):
    h_ref, c_refs, o_ref = refs[0], refs[1:4], refs[4]
    scale = 1.0 / math.sqrt(A_HD)
    kvh_n = 2 * A_HEADS
    n_seq = SAMPLE_ROWS // t_new
    n_q = A_HEADS * SAMPLE_ROWS
    n_c = SPAN * kvh_n
    n_l = n_c + LANES
    row = lax.broadcasted_iota(jnp.int32, (n_q, n_l), 0)
    lane = lax.broadcasted_iota(jnp.int32, (n_q, n_l), 1)
    h_r, seq_r, t_r = row // SAMPLE_ROWS, (row % SAMPLE_ROWS) // t_new, row % t_new
    in_buf = lane < n_c
    j_l = lane // kvh_n
    new = lane - n_c
    t_l = new % t_new
    own = (in_buf & ((lane % kvh_n) == h_r)) | (
        ~in_buf & ((new // SAMPLE_ROWS) == h_r) & (((new % SAMPLE_ROWS) // t_new) == seq_r))
    n_pad = LANES - kvh_n * SAMPLE_ROWS
    pad = [jnp.zeros((n_pad, A_HD), F32)] if n_pad else []
    heads = lambda c0: [h_ref[:, c0 + hd * A_HD:c0 + (hd + 1) * A_HD] for hd in range(A_HEADS)]
    bias = {}
    for seq in range(n_seq):
        dense = own & (seq_r == seq) & ((in_buf & (j_l >= t_r)) | (~in_buf & (t_l <= t_r)))
        bias[seq, None] = jnp.where(dense, 0.0, NEG)
        for t in range(t_new):
            dilated = own & (seq_r == seq) & (t_r == t) & (in_buf | (t_l == t))
            bias[seq, t] = jnp.where(dilated, 0.0, NEG)
    groups = range(N_GROUPS)
    kv2_all, s_all = [], []
    for g in groups:
        q = jnp.concatenate(heads(_col_q(g)), axis=0).astype(BF16)
        kv_new = jnp.concatenate(heads(_col_k(g)) + heads(_col_v(g)) + pad, axis=0)
        kv2, s = [], []
        for seq in range(n_seq):
            for t in ([None] if g == 0 else range(t_new)):
                buf = c_refs[g][seq, :, 0 if t is None else t].reshape(n_c, A_HD)
                kv2.append(jnp.concatenate([buf, kv_new], axis=0).astype(BF16))
                st = lax.dot_general(q, kv2[-1], (((1,), (1,)), ((), ())), preferred_element_type=F32)
                s.append(st * scale + bias[seq, t])
        kv2_all.append(kv2)
        s_all.append(s)
    m_all = [functools.reduce(jnp.maximum, [jnp.max(x, axis=-1, keepdims=True) for x in s]) for s in s_all]
    p_all = [[jnp.exp(x - m) for x in s] for s, m in zip(s_all, m_all)]
    l_all = [functools.reduce(jnp.add, [jnp.sum(x, axis=-1, keepdims=True) for x in p]) for p in p_all]
    shift = lambda x: jnp.concatenate(
        [pltpu.roll(x[:, 0:n_c], A_HEADS, 1), pltpu.roll(x[:, n_c:n_l], A_HEADS * SAMPLE_ROWS, 1)],
        axis=1)
    o_all = [functools.reduce(jnp.add, [jnp.dot(shift(x).astype(BF16), y, preferred_element_type=F32)
                                        for x, y in zip(p, kv2)]) for p, kv2 in zip(p_all, kv2_all)]
    outs = [o / l for o, l in zip(o_all, l_all)]
    lses = [m + jnp.log(l) for m, l in zip(m_all, l_all)]
    top = functools.reduce(jnp.maximum, lses)
    e = [jnp.exp(x - top) for x in lses]
    den = functools.reduce(jnp.add, e)
    o = functools.reduce(jnp.add, [(w / den) * x for w, x in zip(e, outs)])
    for hd in range(A_HEADS):
        o_ref[:, hd * A_HD:(hd + 1) * A_HD] = o[hd * SAMPLE_ROWS:(hd + 1) * SAMPLE_ROWS]


def _attn_sample(h, caches, bsz, t_new):
    kvh_n = 2 * A_HEADS
    n_seq = SAMPLE_ROWS // t_new
    assert SAMPLE_ROWS % t_new == 0 and bsz % n_seq == 0
    views = []
    specs = []
    for (win, dil), cache in zip(A_GROUPS, caches):
        assert cache.shape[1] == win and win // dil == SPAN and (dil == 1 or t_new <= dil)
        views.append(cache.reshape(bsz, SPAN, dil, kvh_n, A_HD))
        specs.append(pl.BlockSpec((n_seq, SPAN, min(dil, t_new), kvh_n, A_HD), lambda i: (i, 0, 0, 0, 0)))
    return pl.pallas_call(
        functools.partial(_attn_sample_kernel, t_new=t_new),
        grid=(bsz // n_seq,),
        in_specs=[pl.BlockSpec((SAMPLE_ROWS, N_PROJ32), lambda i: (i, 0))] + specs,
        out_specs=pl.BlockSpec((SAMPLE_ROWS, A_WIDTH), lambda i: (i, 0)),
        out_shape=jax.ShapeDtypeStruct((bsz * t_new, A_WIDTH), F32),
        compiler_params=_params("parallel"),
        name="attn_sample",
    )(h, *views)


def _final_kernel(x_ref, o_ref, bon_ref, zr_ref, oa_ref, zgg_ref, seg_ref, lnxg_ref, lnxb_ref, bg_ref,
                  woa_ref, wob_ref, wout_ref, lng_ref, lnb_ref, y_ref):
    seg2 = seg_ref[...]
    o = o_ref[...]
    mu = _seg_sum(o, seg2) * (1.0 / R_HD)
    d = o - mu
    var = _seg_sum(d * d, seg2) * (1.0 / R_HD)
    on = d * lax.rsqrt(var + GN_EPS) * lnxg_ref[...] + lnxb_ref[...]
    zr = zr_ref[...].astype(F32)
    y_r = (on + bon_ref[...].astype(F32)) * (zr * jax.nn.sigmoid(zr))

    z_a = zgg_ref[:, 0:A_WIDTH].astype(F32)
    y_a = oa_ref[...] * (z_a * jax.nn.sigmoid(z_a))

    g_r = zgg_ref[:, A_WIDTH:A_WIDTH + D_MODEL].astype(F32)
    g_a = zgg_ref[:, A_WIDTH + D_MODEL:A_WIDTH + 2 * D_MODEL].astype(F32)
    gate_r = jax.nn.sigmoid(g_r + bg_ref[:, 0:D_MODEL])
    gate_a = jax.nn.sigmoid(g_a + bg_ref[:, D_MODEL:2 * D_MODEL])
    mix = (gate_r * jnp.dot(y_r.astype(BF16), woa_ref[...], preferred_element_type=F32)
           + gate_a * jnp.dot(y_a.astype(BF16), wob_ref[...], preferred_element_type=F32))
    yy = ALPHA * x_ref[...] + jnp.dot(mix.astype(BF16), wout_ref[...], preferred_element_type=F32)
    mean = jnp.mean(yy, axis=-1, keepdims=True)
    cen = yy - mean
    variance = jnp.mean(cen * cen, axis=-1, keepdims=True)
    y_ref[...] = cen * lax.rsqrt(variance + LN_EPS) * lng_ref[...] + lnb_ref[...]


def _final(x, o, bonus, h16, o_attn, weights, tm):
    m = x.shape[0]
    row = lambda w, j=0: pl.BlockSpec((tm, w), lambda i: (i, j))
    const = lambda a: pl.BlockSpec(a.shape, lambda i: (0, 0))
    in_specs = ([row(D_MODEL), row(R_WIDTH), row(R_WIDTH), row(R_WIDTH, COL_ZR // R_WIDTH), row(A_WIDTH),
                 row(ZGG_W, COL_ZGG // ZGG_W)] + [const(a) for a in weights])
    return pl.pallas_call(
        _final_kernel,
        grid=(m // tm,),
        in_specs=in_specs,
        out_specs=row(D_MODEL),
        out_shape=jax.ShapeDtypeStruct((m, D_MODEL), F32),
        compiler_params=_params("parallel"),
        name="merge_out_ln",
    )(x, o, bonus, h16, o_attn, h16, *weights)


def _layer(x2, bsz, t, shift_prev, wkv0, caches, wts):
    m = x2.shape[0]
    h, h16 = _in_proj(x2, wts["w_proj"], 2048 if m % 2048 == 0 else m)

    tm = 1024 if m % 1024 == 0 and t % 1024 == 0 else (512 if m % 512 == 0 else m)
    if caches is None:
        first = None
    else:
        first = jnp.zeros((bsz, t, SHIFT_COLS), F32).at[:, 0].set(shift_prev).reshape(m, SHIFT_COLS)
    pack, lw, bonus = _rwkv_prep(h, first, t, wts["prep"], tm)

    if t % CHUNK == 0 and wkv0 is None:
        s0 = jnp.zeros((bsz, N_PAIRS, LANES, LANES), F32)
        o_seq, st = _rwkv_recurrence(pack.reshape(bsz, t, PACK_W), lw.reshape(bsz, t, R_WIDTH), s0, 2)
        o_rwkv = o_seq.reshape(m, R_WIDTH)
        wkv_new = _pairs_to_state(st)
    else:
        o_rwkv, wkv_new = _rwkv_short(pack, lw, wkv0, t)

    def rows_cols(r0, c0, width):
        if t % 8 == 0:
            return h.reshape(bsz, t, N_PROJ32)[:, r0:, c0:c0 + width]
        return h[:, c0:c0 + width].reshape(bsz, t, width)[:, r0:]

    if caches is None:
        o_attn = _attn_prompt(h, bsz, t)
    else:
        o_attn = _attn_sample(h, caches, bsz, t)

    tmf = 512 if m % 512 == 0 else m
    y = _final(x2, o_rwkv, bonus, h16, o_attn, wts["final"], tmf)

    new_kv = []
    for g, (win, _) in enumerate(A_GROUPS):
        keep = min(win, t) if caches is None else t
        new_kv.append(rows_cols(t - keep, _col_k(g), 2 * A_WIDTH).reshape(bsz, keep, 2, A_HEADS, A_HD))
    shift_new = rows_cols(t - 1, COL_ZS, SHIFT_COLS)[:, 0]
    return y.reshape(bsz, t, D_MODEL), new_kv, wkv_new, shift_new


def kernel(x_prompt, x_sample, cache_kv_g1, cache_kv_g2, cache_kv_g3, state_rwkv_wkv, state_rwkv_shift,
           w_in, b_gate, mu_shift, w0, w_w2, a0, w_a2, k_k, k_a, r_k, lnx_g, lnx_b,
           w_oa, w_ob, w_out, ln_g, ln_b):
    assert w_in.shape[0] == DEPTH
    bp, tp, _ = x_prompt.shape
    bs, ts, _ = x_sample.shape
    xp = x_prompt.reshape(bp * tp, D_MODEL)
    xs = x_sample.reshape(bs * ts, D_MODEL)
    head = lax.broadcasted_iota(jnp.int32, (LANES, LANES), 0) // R_HD
    seg = (head == head.T).astype(BF16)
    seg2 = jnp.concatenate([seg, seg], axis=0)
    row = lambda a: a.reshape(1, -1)
    acc = [[] for _ in range(12)]
    for l in range(DEPTH):
        wf = w_in[l]
        c0 = SHIFT_COLS
        c1 = c0 + R_WIDTH
        c2 = c1 + 3 * A_QKV_W
        qkv_col = lambda which, g: wf[:, c1 + (which * N_GROUPS + g) * A_WIDTH:
                                      c1 + (which * N_GROUPS + g + 1) * A_WIDTH]
        w_proj = jnp.concatenate(
            [wf[:, :c0], jnp.zeros((D_MODEL, COL_Q - c0), F32)] + [qkv_col(0, g) for g in range(N_GROUPS)]
            + [qkv_col(which, g) for g in range(N_GROUPS) for which in (1, 2)]
            + [wf[:, c2:], wf[:, c0:c1]], axis=1).astype(BF16)
        zero = jnp.zeros((LORA, R_WIDTH), F32)
        w_lora = jnp.concatenate([jnp.concatenate([w_w2[l], zero], axis=1),
                                  jnp.concatenate([zero, w_a2[l]], axis=1)], axis=0)
        wl_hi = w_lora.astype(BF16)
        wl_lo = (w_lora - wl_hi.astype(F32)).astype(BF16)
        wts = {
            "w_proj": w_proj,
            "prep": (row(mu_shift[l]), row(w0[l]), row(a0[l]), jnp.concatenate([wl_hi, wl_hi], axis=0),
                     wl_lo, row(k_k[l]), row(k_a[l]), row(r_k[l]), seg2),
            "final": (seg2, row(lnx_g[l]), row(lnx_b[l]), row(b_gate[l]), w_oa[l].astype(BF16),
                      w_ob[l].astype(BF16), w_out[l].astype(BF16), row(ln_g[l]), row(ln_b[l])),
        }
        yp, kv_p, s_p, sh_p = _layer(xp, bp, tp, None, None, None, wts)
        ys, kv_s, s_s, sh_s = _layer(xs, bs, ts, state_rwkv_shift[l], state_rwkv_wkv[l],
                                     (cache_kv_g1[l], cache_kv_g2[l], cache_kv_g3[l]), wts)
        xp = yp.reshape(bp * tp, D_MODEL)
        xs = ys.reshape(bs * ts, D_MODEL)
        for idx, val in zip(range(2, 12), (kv_p[0], kv_s[0], kv_p[1], kv_s[1], kv_p[2], kv_s[2],
                                           s_p, s_s, sh_p, sh_s)):
            acc[idx].append(val)
    outs = [xp.reshape(bp, tp, D_MODEL), xs.reshape(bs, ts, D_MODEL)]
    outs += [jnp.stack(a) for a in acc[2:]]
    return tuple(outs)
```

```python
import functools
import math

import jax
import jax.numpy as jnp
from jax import lax
from jax.experimental import pallas as pl
from jax.experimental.pallas import tpu as pltpu

F32 = jnp.float32
BF16 = jnp.bfloat16

D_MODEL = 1024
R_HEADS = 8
R_HD = 64
R_WIDTH = R_HEADS * R_HD
LORA = 64
SHIFT_COLS = 3 * R_WIDTH + 2 * LORA
GN_EPS = 64e-5
A_GROUPS = ((128, 1), (512, 4), (2048, 16))
N_GROUPS = 3
A_HEADS = 4
A_HD = 128
A_WIDTH = A_HEADS * A_HD
A_QKV_W = N_GROUPS * A_WIDTH
SPAN = 128
DEPTH = 1
ALPHA = (2 * DEPTH) ** 0.25
LN_EPS = 1e-5
NEG = -1e30

LANES = 128
CHUNK = 64
N_PAIRS = R_HEADS // 2
VMEM_LIMIT = 48 * 1024 * 1024
PROJ_VMEM_LIMIT = 56 * 1024 * 1024

PROJ_TN = 1024
COL_ZS = 0
COL_Q = 2048
COL_KV = COL_Q + A_QKV_W
N_PROJ32 = -(-(COL_KV + 2 * A_QKV_W) // PROJ_TN) * PROJ_TN
ZGG_W = A_WIDTH + 2 * D_MODEL
COL_ZGG = 0
COL_ZR = ZGG_W
N_PROJ16 = COL_ZR + R_WIDTH
assert COL_Q % A_WIDTH == 0 and COL_KV % A_WIDTH == 0 and COL_ZR % R_WIDTH == 0
assert N_PROJ32 % PROJ_TN == 0 and N_PROJ16 % PROJ_TN == 0


def _col_q(g):
    return COL_Q + g * A_WIDTH


def _col_k(g):
    return COL_KV + g * 2 * A_WIDTH


def _col_v(g):
    return _col_k(g) + A_WIDTH


def _params(*sem):
    return pltpu.CompilerParams(dimension_semantics=sem, vmem_limit_bytes=VMEM_LIMIT)


def _split2(x):
    hi = x.astype(BF16)
    return hi, (x - hi.astype(F32)).astype(BF16)


def _seg_sum(x, seg2):
    cols = []
    for c in range(x.shape[1] // LANES):
        hi, lo = _split2(x[:, c * LANES:(c + 1) * LANES])
        cols.append(jnp.dot(jnp.concatenate([hi, lo], axis=1), seg2, preferred_element_type=F32))
    return jnp.concatenate(cols, axis=1)


def _mm_kernel(x_ref, w_ref, o32_ref, o16_ref, *, n32):
    j = pl.program_id(1)
    product = lambda: jnp.dot(x_ref[...].astype(BF16), w_ref[...], preferred_element_type=F32)

    @pl.when(j < n32)
    def _():
        o32_ref[...] = product()

    @pl.when(j >= n32)
    def _():
        o16_ref[...] = product().astype(BF16)


def _in_proj(x, w, tm):
    m, k = x.shape
    n32 = N_PROJ32 // PROJ_TN
    n16 = N_PROJ16 // PROJ_TN
    return pl.pallas_call(
        functools.partial(_mm_kernel, n32=n32),
        grid=(m // tm, n32 + n16),
        in_specs=[pl.BlockSpec((tm, k), lambda i, j: (i, 0)),
                  pl.BlockSpec((k, PROJ_TN), lambda i, j: (0, j))],
        out_specs=[pl.BlockSpec((tm, PROJ_TN), lambda i, j: (i, jnp.minimum(j, n32 - 1))),
                   pl.BlockSpec((tm, PROJ_TN), lambda i, j: (i, jnp.maximum(j - n32, 0)))],
        out_shape=[jax.ShapeDtypeStruct((m, N_PROJ32), F32), jax.ShapeDtypeStruct((m, N_PROJ16), BF16)],
        compiler_params=pltpu.CompilerParams(dimension_semantics=("parallel", "arbitrary"),
                                             vmem_limit_bytes=PROJ_VMEM_LIMIT),
        name="in_proj",
    )(x, w)


PACK_R, PACK_K, PACK_V, PACK_KK, PACK_B = (i * R_WIDTH for i in range(5))
PACK_W = 5 * R_WIDTH


def _prep_math(zs, prev, mu, w0, a0, wl_a, wl_b, k_k, k_a, r_k, seg2, outs):
    pack_o, lw_o, bon_o = outs
    zsm = zs + mu * (prev - zs)
    r = zsm[:, 0:R_WIDTH]
    k = zsm[:, R_WIDTH:2 * R_WIDTH]
    v = zsm[:, 2 * R_WIDTH:3 * R_WIDTH]
    slab = zsm[:, 3 * R_WIDTH:SHIFT_COLS]
    lane = lax.broadcasted_iota(jnp.int32, slab.shape, 1)
    slab = jnp.where(lane < LORA, jnp.tanh(slab), slab)
    s_hi, s_lo = _split2(slab)
    lin = (jnp.dot(jnp.concatenate([s_hi, s_lo], axis=1), wl_a, preferred_element_type=F32)
           + jnp.dot(s_hi, wl_b, preferred_element_type=F32))
    wlin = w0 + lin[:, 0:R_WIDTH]
    alin = a0 + lin[:, R_WIDTH:2 * R_WIDTH]
    nw = -wlin
    softplus = jnp.maximum(nw, 0.0) + jnp.log1p(jnp.exp(-jnp.abs(nw)))
    w = -softplus - 0.5
    lw = -jnp.exp(w)
    a = jax.nn.sigmoid(alin)
    kkr = k * k_k
    kk = kkr / jnp.maximum(jnp.sqrt(_seg_sum(kkr * kkr, seg2)), 1e-12)
    kp = k * (1.0 + (a - 1.0) * k_a)
    for slot, val in enumerate((r, kp, v, kk, kk * a)):
        pack_o[:, slot * R_WIDTH:(slot + 1) * R_WIDTH] = val.astype(pack_o.dtype)
    lw_o[...] = lw
    bon_o[...] = (_seg_sum(r * kp * r_k, seg2) * v).astype(bon_o.dtype)


def _prep_prompt_kernel(zs_ref, p8_ref, mu_ref, w0_ref, a0_ref, wla_ref, wlb_ref, kk_ref, ka_ref,
                        rk_ref, seg_ref, *outs, tiles_per_seq):
    i = pl.program_id(0)
    zs = zs_ref[...]
    rolled = pltpu.roll(zs, 1, 0)
    carry = jnp.where(i % tiles_per_seq == 0, 0.0, p8_ref[7:8, :])
    row = lax.broadcasted_iota(jnp.int32, zs.shape, 0)
    prev = jnp.where(row == 0, carry, rolled)
    _prep_math(zs, prev, mu_ref[...], w0_ref[...], a0_ref[...], wla_ref[...], wlb_ref[...],
               kk_ref[...], ka_ref[...], rk_ref[...], seg_ref[...], outs)


def _prep_sample_kernel(zs_ref, first_ref, mu_ref, w0_ref, a0_ref, wla_ref, wlb_ref, kk_ref, ka_ref,
                        rk_ref, seg_ref, *outs, seq_len):
    zs = zs_ref[...]
    rolled = pltpu.roll(zs, 1, 0)
    row = lax.broadcasted_iota(jnp.int32, zs.shape, 0)
    prev = jnp.where(row % seq_len == 0, first_ref[...], rolled)
    _prep_math(zs, prev, mu_ref[...], w0_ref[...], a0_ref[...], wla_ref[...], wlb_ref[...],
               kk_ref[...], ka_ref[...], rk_ref[...], seg_ref[...], outs)


def _rwkv_prep(h, first, seq_len, weights, tm):
    m = h.shape[0]
    const = lambda a: pl.BlockSpec(a.shape, lambda i: (0, 0))
    row_spec = pl.BlockSpec((tm, SHIFT_COLS), lambda i: (i, COL_ZS))
    if first is None:
        kern = functools.partial(_prep_prompt_kernel, tiles_per_seq=seq_len // tm)
        second = h
        second_spec = pl.BlockSpec((8, SHIFT_COLS), lambda i: (jnp.maximum(i * (tm // 8) - 1, 0), COL_ZS))
    else:
        kern = functools.partial(_prep_sample_kernel, seq_len=seq_len)
        second = first
        second_spec = pl.BlockSpec((tm, SHIFT_COLS), lambda i: (i, 0))
    out_spec = lambda w: pl.BlockSpec((tm, w), lambda i: (i, 0))
    return pl.pallas_call(
        kern,
        grid=(m // tm,),
        in_specs=[row_spec, second_spec] + [const(a) for a in weights],
        out_specs=[out_spec(PACK_W), out_spec(R_WIDTH), out_spec(R_WIDTH)],
        out_shape=[jax.ShapeDtypeStruct((m, PACK_W), BF16), jax.ShapeDtypeStruct((m, R_WIDTH), F32),
                   jax.ShapeDtypeStruct((m, R_WIDTH), BF16)],
        compiler_params=_params("parallel"),
        name="rwkv_prep",
    )(h, second, *weights)


CHUNKS_PER_STEP = 2


def _rwkv_chunk_kernel(*refs, nb):
    ins, s0_ref, o_ref, st_ref = refs[0:6], refs[6], refs[7], refs[8]
    yo_sc, ti_sc, mb_sc, ut_sc, vs_sc, dec_sc = refs[9:15]
    n = pl.program_id(1)
    c = CHUNK
    c2 = 2 * c
    lane = lax.broadcasted_iota(jnp.int32, (1, LANES), 1)
    m_lo = (lane < R_HD).astype(F32)
    m_hi = 1.0 - m_lo
    row_c = lax.broadcasted_iota(jnp.int32, (c, 4 * c), 0)
    col_c = lax.broadcasted_iota(jnp.int32, (c, 4 * c), 1)
    tri3 = ((row_c >= col_c % c) & (col_c < 3 * c)).astype(BF16)
    row2 = lax.broadcasted_iota(jnp.int32, (c2, c2), 0)
    col2 = lax.broadcasted_iota(jnp.int32, (c2, c2), 1)
    strict = row2 > col2
    incl = row2 >= col2
    eye = (row2 == col2).astype(F32)

    def stack(x):
        return jnp.concatenate([x * m_lo, x * m_hi], axis=0)

    def mm(a, b):
        return jnp.dot(a.astype(BF16), b.astype(BF16), preferred_element_type=F32)

    def nt(a, b):
        return lax.dot_general(a.astype(BF16), b.astype(BF16), (((1,), (1,)), ((), ())),
                               preferred_element_type=F32)

    each = lambda f, *xs: [f(*a) for a in zip(*xs)]
    chains = [(bi, slice(p * LANES, (p + 1) * LANES), p) for bi in range(nb) for p in range(N_PAIRS)]

    def cumsum(lw_all):
        hi = lw_all.astype(BF16)
        r1 = lw_all - hi.astype(F32)
        mid = r1.astype(BF16)
        lo = (r1 - mid.astype(F32)).astype(BF16)
        return jnp.dot(tri3, jnp.concatenate([hi, mid, lo, jnp.zeros_like(lo)], axis=0),
                       preferred_element_type=F32)

    def prepare(refs, row0, slot):
        pack_ref, lw_ref = refs
        rs = slice(row0, row0 + c)
        g_all = [cumsum(lw_ref[bi, rs, :]) for bi in range(nb)]
        yield
        load = lambda c0: [pack_ref[bi, rs, c0 + sl.start:c0 + sl.stop].astype(F32) for bi, sl, _ in chains]
        r, kp, v, kk, b = (load(c0) for c0 in (PACK_R, PACK_K, PACK_V, PACK_KK, PACK_B))
        lw = [lw_ref[bi, rs, sl] for bi, sl, _ in chains]
        g = [g_all[bi][:, sl] for bi, sl, _ in chains]
        g_last = each(lambda x: x[c - 1:c, :], g)
        e_g = each(jnp.exp, g)
        e_ng = each(lambda x: jnp.exp(-x), g)
        e_gm = each(lambda x, y: jnp.exp(x - y), g, lw)
        e_cg = each(lambda x, y: jnp.exp(y - x), g, g_last)
        ad_s = each(lambda x, y: stack(-(x * y)), kk, e_gm)
        rd_s = each(lambda x, y: stack(x * y), r, e_g)
        bd_s = each(lambda x, y: stack(x * y), b, e_ng)
        kd_s = each(lambda x, y: stack(x * y), kp, e_ng)
        be_s = each(lambda x, y: stack(x * y), b, e_cg)
        ke_s = each(lambda x, y: stack(x * y), kp, e_cg)
        for idx in range(len(chains)):
            yo_sc[slot, idx, 0:c2, 0:c2] = ad_s[idx].astype(BF16)
            yo_sc[slot, idx, c2:2 * c2, 0:c2] = rd_s[idx].astype(BF16)
            ut_sc[slot, idx, :, 0:c2] = jnp.transpose(be_s[idx]).astype(BF16)
            ut_sc[slot, idx, :, c2:2 * c2] = jnp.transpose(ke_s[idx]).astype(BF16)
            vs_sc[slot, idx] = stack(v[idx]).astype(BF16)
            dec_sc[slot, idx] = jnp.transpose(jnp.broadcast_to(jnp.exp(g_last[idx]), (LANES, LANES)))
        yield
        z = each(lambda a1, a2, b1, b2: nt(jnp.concatenate([a1, a2], axis=0),
                                           jnp.concatenate([b1, b2], axis=0)), ad_s, rd_s, bd_s, kd_s)
        yield
        for idx, zz in enumerate(z):
            yo_sc[slot, idx, 0:c2, c2:2 * c2] = jnp.where(strict, zz[0:c2, c2:2 * c2], 0.0).astype(BF16)
            yo_sc[slot, idx, c2:2 * c2, c2:2 * c2] = jnp.where(
                incl, zz[c2:2 * c2, c2:2 * c2], 0.0).astype(BF16)
            mb_sc[slot, idx] = jnp.where(incl, zz[c2:2 * c2, 0:c2], 0.0).astype(BF16)
        n_b = each(lambda x: jnp.where(strict, x[0:c2, 0:c2], 0.0), z)
        t_inv = each(lambda x: eye + x, n_b)
        q = each(lambda x: mm(x, x), n_b)
        yield
        steps = int(math.log2(c))
        for i in range(1, steps):
            if i < steps - 1:
                res = each(lambda x, tt: mm(x, jnp.concatenate([x, tt], axis=1)), q, t_inv)
                q = each(lambda x: x[:, 0:c2], res)
                t_inv = each(lambda tt, x: tt + x[:, c2:2 * c2], t_inv, res)
            else:
                t_inv = each(lambda tt, x: tt + mm(x, tt), t_inv, q)
            yield
        for idx, tt in enumerate(t_inv):
            ti_sc[slot, idx] = tt.astype(BF16)

    def advance(slot, row0):
        idxs = list(range(len(chains)))
        st = [st_ref[bi, p] for bi, _, p in chains]
        vs = [vs_sc[slot, i] for i in idxs]
        yo = [jnp.dot(yo_sc[slot, i], jnp.concatenate([s.astype(BF16), w], axis=0),
                      preferred_element_type=F32) for i, s, w in zip(idxs, st, vs)]
        yield
        u = [jnp.dot(ti_sc[slot, i], x[0:c2].astype(BF16), preferred_element_type=F32)
             for i, x in zip(idxs, yo)]
        ub = [x.astype(BF16) for x in u]
        yield
        o_bd = [x[c2:2 * c2] + jnp.dot(mb_sc[slot, i], w, preferred_element_type=F32)
                for i, x, w in zip(idxs, yo, ub)]
        for (bi, sl, _), o in zip(chains, o_bd):
            o_ref[bi, row0:row0 + c, sl] = o[0:c] + o[c:c2]
        yield
        upd = [jnp.dot(ut_sc[slot, i], jnp.concatenate([w, x], axis=0), preferred_element_type=F32)
               for i, w, x in zip(idxs, ub, vs)]
        for i, ((bi, _, p), s, d) in enumerate(zip(chains, st, upd)):
            st_ref[bi, p] = dec_sc[slot, i] * s + d

    def interleave(*stages):
        live = list(stages)
        while live:
            for gen in list(live):
                if next(gen, StopIteration) is StopIteration:
                    live.remove(gen)

    @pl.when(n == 0)
    def _():
        st_ref[...] = s0_ref[...]
        interleave(prepare(ins[0:2], 0, 0))

    interleave(advance(0, 0), prepare(ins[2:4], 0, 1))
    interleave(advance(1, c), prepare(ins[4:6], 0, 0))


def _rwkv_recurrence(pack, lw, s0, nb):
    bsz, t, _ = lw.shape
    assert CHUNKS_PER_STEP == 2
    rows = CHUNKS_PER_STEP * CHUNK
    nsteps = t // rows
    last = t // CHUNK - 1
    chunk = lambda which: [pl.BlockSpec((nb, CHUNK, w), lambda i, n: (i, which(n), 0))
                           for w in (PACK_W, R_WIDTH)]
    chunk_specs = (chunk(lambda n: 0) + chunk(lambda n: 2 * n + 1)
                   + chunk(lambda n: jnp.minimum(2 * n + 2, last)))
    seq_spec = pl.BlockSpec((nb, rows, R_WIDTH), lambda i, n: (i, n, 0))
    st_spec = pl.BlockSpec((nb, N_PAIRS, LANES, LANES), lambda i, n: (i, 0, 0, 0))
    nchain = nb * N_PAIRS
    c2 = 2 * CHUNK
    slots = 2
    scratch = [pltpu.VMEM((slots, nchain, 2 * c2, 2 * c2), BF16),
               pltpu.VMEM((slots, nchain, c2, c2), BF16),
               pltpu.VMEM((slots, nchain, c2, c2), BF16),
               pltpu.VMEM((slots, nchain, c2, 2 * c2), BF16),
               pltpu.VMEM((slots, nchain, c2, c2), BF16),
               pltpu.VMEM((slots, nchain, LANES, LANES), F32)]
    return pl.pallas_call(
        functools.partial(_rwkv_chunk_kernel, nb=nb),
        grid=(bsz // nb, nsteps),
        in_specs=chunk_specs + [st_spec],
        out_specs=[seq_spec, st_spec],
        out_shape=[jax.ShapeDtypeStruct((bsz, t, R_WIDTH), F32),
                   jax.ShapeDtypeStruct(s0.shape, F32)],
        scratch_shapes=scratch,
        compiler_params=_params("parallel", "arbitrary"),
        name="rwkv_chunks",
    )(*([pack, lw] * 3), s0)


def _pairs_to_state(st):
    bsz = st.shape[0]
    s6 = st.reshape(bsz, N_PAIRS, 2, R_HD, 2, R_HD)
    s = jnp.stack([s6[:, :, 0, :, 0, :], s6[:, :, 1, :, 1, :]], axis=2)
    return jnp.swapaxes(s.reshape(bsz, R_HEADS, R_HD, R_HD), -1, -2)


SHORT_ROWS = 64


def _rwkv_short_kernel(pack_ref, lw_ref, s_ref, o_ref, sn_ref, *, t_len):
    rows = SHORT_ROWS
    ng = rows // t_len
    r2 = 2 * rows
    assert 2 * t_len == 8 and r2 == LANES
    lane = lax.broadcasted_iota(jnp.int32, (1, LANES), 1)
    m_lo = (lane < R_HD).astype(F32)
    m_hi = 1.0 - m_lo
    lane_blk = lane // t_len
    row_c = lax.broadcasted_iota(jnp.int32, (rows, 4 * rows), 0)
    col_c = lax.broadcasted_iota(jnp.int32, (rows, 4 * rows), 1)
    j_c = col_c % rows
    same_c = ((row_c // t_len) == (j_c // t_len)) & (col_c < 3 * rows)
    cum_lhs = jnp.concatenate([(same_c & (row_c >= j_c)).astype(BF16), same_c.astype(BF16)], axis=0)
    row2 = lax.broadcasted_iota(jnp.int32, (r2, r2), 0)
    col2 = lax.broadcasted_iota(jnp.int32, (r2, r2), 1)
    blk = (row2 // t_len) == (col2 // t_len)
    strict = blk & (row2 > col2)
    incl = blk & (row2 >= col2)
    first_seq = lax.broadcasted_iota(jnp.int32, (8, LANES), 0) < t_len
    zero_half = jnp.zeros((R_HD, R_HD), F32)

    def stack(x):
        return jnp.concatenate([x * m_lo, x * m_hi], axis=0)

    def half(x):
        return jnp.concatenate([x * m_lo, pltpu.roll(x, R_HD, 1) * m_lo], axis=0)

    def mm(a, b):
        return jnp.dot(a.astype(BF16), b.astype(BF16), preferred_element_type=F32)

    def nt(a, b):
        return lax.dot_general(a.astype(BF16), b.astype(BF16), (((1,), (1,)), ((), ())),
                               preferred_element_type=F32)

    each = lambda f, *xs: [f(*a) for a in zip(*xs)]

    lw_all = lw_ref[...]
    hi = lw_all.astype(BF16)
    r1 = lw_all - hi.astype(F32)
    mid = r1.astype(BF16)
    lo = (r1 - mid.astype(F32)).astype(BF16)
    g_both = jnp.dot(cum_lhs, jnp.concatenate([hi, mid, lo, jnp.zeros_like(lo)], axis=0),
                     preferred_element_type=F32)

    pairs = list(range(N_PAIRS))
    sls = [slice(p * LANES, (p + 1) * LANES) for p in pairs]
    load = lambda c0: [pack_ref[:, c0 + sl.start:c0 + sl.stop].astype(F32) for sl in sls]
    r, kp, v, kk, b = (load(c0) for c0 in (PACK_R, PACK_K, PACK_V, PACK_KK, PACK_B))
    lw = [lw_ref[:, sl] for sl in sls]
    g = [g_both[0:rows, sl] for sl in sls]
    g_tot = [g_both[rows:r2, sl] for sl in sls]
    e_g = each(jnp.exp, g)
    e_ng = each(lambda x: jnp.exp(-x), g)
    e_gm = each(lambda x, y: jnp.exp(x - y), g, lw)
    e_cg = each(lambda x, y: jnp.exp(y - x), g, g_tot)
    ad = each(lambda x, y: -(x * y), kk, e_gm)
    rd = each(lambda x, y: x * y, r, e_g)
    bd = each(lambda x, y: x * y, b, e_ng)
    kd = each(lambda x, y: x * y, kp, e_ng)
    be_h = each(lambda x, y: half(x * y), b, e_cg)
    ke_h = each(lambda x, y: half(x * y), kp, e_cg)
    v_h = each(half, v)
    ad_h = each(half, ad)
    rd_h = each(half, rd)
    dec_h = each(lambda x: half(jnp.exp(x)), g_tot)

    z = each(lambda a1, a2, b1, b2: nt(jnp.concatenate([stack(a1), stack(a2)], axis=0),
                                       jnp.concatenate([stack(b1), stack(b2)], axis=0)), ad, rd, bd, kd)
    n_b = each(lambda x: jnp.where(strict, x[0:r2, 0:r2], 0.0), z)
    n_k = each(lambda x: jnp.where(strict, x[0:r2, r2:2 * r2], 0.0), z)
    m_b = each(lambda x: jnp.where(incl, x[r2:2 * r2, 0:r2], 0.0), z)
    m_k = each(lambda x: jnp.where(incl, x[r2:2 * r2, r2:2 * r2], 0.0), z)

    y0, o0 = [], []
    for p in pairs:
        y_tiles, o_tiles = [], []
        for hh in range(2):
            for i in range(ng // 2):
                rs = slice(hh * rows + 8 * i, hh * rows + 8 * i + 8)
                lhs = jnp.concatenate([ad_h[p][rs], rd_h[p][rs]], axis=0)[:, 0:R_HD]
                res = [nt(lhs, jnp.concatenate([s_ref[2 * i + w, 2 * p + hh], zero_half], axis=0))
                       for w in range(2)]
                y_tiles.append(jnp.where(first_seq, res[0][0:8], res[1][0:8]))
                o_tiles.append(jnp.where(first_seq, res[0][8:16], res[1][8:16]))
        y0.append(jnp.concatenate(y_tiles, axis=0))
        o0.append(jnp.concatenate(o_tiles, axis=0))

    y = each(lambda a, nk, vh: a + mm(nk, vh), y0, n_k, v_h)
    q = n_b
    steps = int(math.log2(t_len))
    for i in range(steps):
        if i < steps - 1:
            res = each(lambda x, yy: mm(x, jnp.concatenate([x, yy], axis=1)), q, y)
            q = each(lambda x: x[:, 0:r2], res)
            y = each(lambda yy, x: yy + x[:, r2:2 * r2], y, res)
        else:
            y = each(lambda yy, x: yy + mm(x, yy), y, q)
    u = y

    o_h = each(lambda a, mb, uu, mk, vh: a + mm(mb, uu) + mm(mk, vh), o0, m_b, u, m_k, v_h)
    for sl, o in zip(sls, o_h):
        o_ref[:, sl] = o[0:rows] + pltpu.roll(o[rows:r2], R_HD, 1)

    for p in pairs:
        ut = jnp.transpose(u[p])[0:R_HD]
        vt = jnp.transpose(v_h[p])[0:R_HD]
        rhs = jnp.concatenate([be_h[p], ke_h[p]], axis=0)
        keys = [(hh, s) for hh in range(2) for s in range(ng)]
        sel = [(lane_blk == hh * ng + s).astype(F32) for hh, s in keys]
        lhs = jnp.concatenate([jnp.concatenate([ut * m, vt * m], axis=1) for m in sel], axis=0)
        upd = mm(lhs, rhs)
        for idx, (hh, s) in enumerate(keys):
            row0 = hh * rows + s * t_len
            dec = dec_h[p][row0:row0 + 1, 0:R_HD]
            sn_ref[s, 2 * p + hh] = (s_ref[s, 2 * p + hh] * dec
                                     + upd[idx * R_HD:(idx + 1) * R_HD, 0:R_HD])


def _rwkv_short(pack, lw, state, t_len):
    m = lw.shape[0]
    ng = SHORT_ROWS // t_len
    seq_spec = pl.BlockSpec((SHORT_ROWS, R_WIDTH), lambda i: (i, 0))
    pack_spec = pl.BlockSpec((SHORT_ROWS, PACK_W), lambda i: (i, 0))
    st_spec = pl.BlockSpec((ng, R_HEADS, R_HD, R_HD), lambda i: (i, 0, 0, 0))
    return pl.pallas_call(
        functools.partial(_rwkv_short_kernel, t_len=t_len),
        grid=(m // SHORT_ROWS,),
        in_specs=[pack_spec, seq_spec, st_spec],
        out_specs=[seq_spec, st_spec],
        out_shape=[jax.ShapeDtypeStruct((m, R_WIDTH), F32), jax.ShapeDtypeStruct(state.shape, F32)],
        compiler_params=_params("parallel"),
        name="rwkv_short",
    )(pack, lw, state)


ATTN_ROWS = SPAN * max(d for _, d in A_GROUPS)
ATTN_BATCH = 8


def _attn_prompt_kernel(*refs):
    ins = refs[:5 * N_GROUPS]
    o_ref = refs[5 * N_GROUPS]
    o_sc = refs[5 * N_GROUPS + 1:5 * N_GROUPS + 1 + N_GROUPS]
    l_sc = refs[5 * N_GROUPS + 1 + N_GROUPS:]
    n = pl.program_id(1)
    i = lax.broadcasted_iota(jnp.int32, (SPAN, 2 * SPAN), 0)
    j = lax.broadcasted_iota(jnp.int32, (SPAN, 2 * SPAN), 1)
    dist = SPAN + i - j
    band = (dist >= 0) & (dist <= SPAN)
    band_first = band & ((n > 0) | (j >= SPAN))
    scale = 1.0 / math.sqrt(A_HD)
    cat = lambda a, b: jnp.concatenate([a, b], axis=0).astype(BF16)

    for g, (_, dil) in enumerate(A_GROUPS):
        q_ref, kp_ref, kc_ref, vp_ref, vc_ref = ins[5 * g:5 * g + 5]

        def rows(run, rho):
            start = run * SPAN * dil + rho
            return pl.ds(start, SPAN, stride=dil) if dil > 1 else pl.ds(start, SPAN)

        all_probs = [(run, rho) for run in range(ATTN_ROWS // (SPAN * dil)) for rho in range(dil)]
        for b0 in range(0, len(all_probs), ATTN_BATCH):
            probs = all_probs[b0:b0 + ATTN_BATCH]
            q = [q_ref[rows(run, rho), :].astype(BF16) for run, rho in probs]
            k = [cat(kp_ref[rows(0, rho), :] if run == 0 else kc_ref[rows(run - 1, rho), :],
                     kc_ref[rows(run, rho), :]) for run, rho in probs]
            v = [cat(vp_ref[rows(0, rho), :] if run == 0 else vc_ref[rows(run - 1, rho), :],
                     vc_ref[rows(run, rho), :]) for run, rho in probs]
            s = [lax.dot_general(a, b, (((1,), (1,)), ((), ())), preferred_element_type=F32) * scale
                 for a, b in zip(q, k)]
            s = [jnp.where(band_first if run == 0 else band, x, NEG) for x, (run, _) in zip(s, probs)]
            m = [jnp.max(x, axis=-1, keepdims=True) for x in s]
            p = [jnp.exp(x - y) for x, y in zip(s, m)]
            l = [jnp.sum(x, axis=-1, keepdims=True) for x in p]
            o = [jnp.dot(x.astype(BF16), y, preferred_element_type=F32) for x, y in zip(p, v)]
            for (run, rho), oo, ll, mx in zip(probs, o, l, m):
                o_sc[g][rows(run, rho), :] = oo / ll
                l_sc[g][rows(run, rho), :] = jnp.broadcast_to(mx + jnp.log(ll), (SPAN, A_HD))

    lse = [ref[...] for ref in l_sc]
    top = functools.reduce(jnp.maximum, lse)
    e = [jnp.exp(x - top) for x in lse]
    den = functools.reduce(jnp.add, e)
    o_ref[...] = functools.reduce(jnp.add, [(w / den) * ref[...] for w, ref in zip(e, o_sc)])


def _attn_prompt(h, bsz, t):
    nblk = t // ATTN_ROWS
    in_specs = []
    for g, (_, dil) in enumerate(A_GROUPS):
        prev_rows = SPAN * dil
        per_blk = ATTN_ROWS // prev_rows
        col = lambda which, hd, g=g: (_col_q, _col_k, _col_v)[which](g) // A_HD + hd
        cur = lambda which, col=col: pl.BlockSpec(
            (ATTN_ROWS, A_HD), lambda b, n, hd: (b * nblk + n, col(which, hd)))
        prev = lambda which, col=col, per_blk=per_blk, prev_rows=prev_rows: pl.BlockSpec(
            (prev_rows, A_HD),
            lambda b, n, hd: (jnp.maximum((b * nblk + n) * per_blk - 1, 0), col(which, hd)))
        in_specs += [cur(0), prev(1), cur(1), prev(2), cur(2)]
    return pl.pallas_call(
        _attn_prompt_kernel,
        grid=(bsz, nblk, A_HEADS),
        in_specs=in_specs,
        out_specs=pl.BlockSpec((ATTN_ROWS, A_HD), lambda b, n, hd: (b * nblk + n, hd)),
        out_shape=jax.ShapeDtypeStruct((bsz * t, A_WIDTH), F32),
        scratch_shapes=[pltpu.VMEM((ATTN_ROWS, A_HD), F32)] * (2 * N_GROUPS),
        compiler_params=_params("parallel", "arbitrary", "arbitrary"),
        name="attn_prompt",
    )(*([h] * (5 * N_GROUPS)))


SAMPLE_ROWS = 8


def _attn_sample_kernel(*refs, t_new):
    h_ref, c_refs, o_ref = refs[0], refs[1:4], refs[4]
    scale = 1.0 / math.sqrt(A_HD)
    kvh_n = 2 * A_HEADS
    n_seq = SAMPLE_ROWS // t_new
    n_q = A_HEADS * SAMPLE_ROWS
    n_c = SPAN * kvh_n
    n_l = n_c + LANES
    row = lax.broadcasted_iota(jnp.int32, (n_q, n_l), 0)
    lane = lax.broadcasted_iota(jnp.int32, (n_q, n_l), 1)
    h_r, seq_r, t_r = row // SAMPLE_ROWS, (row % SAMPLE_ROWS) // t_new, row % t_new
    in_buf = lane < n_c
    j_l = lane // kvh_n
    new = lane - n_c
    t_l = new % t_new
    own = (in_buf & ((lane % kvh_n) == h_r)) | (
        ~in_buf & ((new // SAMPLE_ROWS) == h_r) & (((new % SAMPLE_ROWS) // t_new) == seq_r))
    n_pad = LANES - kvh_n * SAMPLE_ROWS
    pad = [jnp.zeros((n_pad, A_HD), F32)] if n_pad else []
    heads = lambda c0: [h_ref[:, c0 + hd * A_HD:c0 + (hd + 1) * A_HD] for hd in range(A_HEADS)]
    bias = {}
    for seq in range(n_seq):
        dense = own & (seq_r == seq) & ((in_buf & (j_l >= t_r)) | (~in_buf & (t_l <= t_r)))
        bias[seq, None] = jnp.where(dense, 0.0, NEG)
        for t in range(t_new):
            dilated = own & (seq_r == seq) & (t_r == t) & (in_buf | (t_l == t))
            bias[seq, t] = jnp.where(dilated, 0.0, NEG)
    groups = range(N_GROUPS)
    kv2_all, s_all = [], []
    for g in groups:
        q = jnp.concatenate(heads(_col_q(g)), axis=0).astype(BF16)
        kv_new = jnp.concatenate(heads(_col_k(g)) + heads(_col_v(g)) + pad, axis=0)
        kv2, s = [], []
        for seq in range(n_seq):
            for t in ([None] if g == 0 else range(t_new)):
                buf = c_refs[g][seq, :, 0 if t is None else t].reshape(n_c, A_HD)
                kv2.append(jnp.concatenate([buf, kv_new], axis=0).astype(BF16))
                st = lax.dot_general(q, kv2[-1], (((1,), (1,)), ((), ())), preferred_element_type=F32)
                s.append(st * scale + bias[seq, t])
        kv2_all.append(kv2)
        s_all.append(s)
    m_all = [functools.reduce(jnp.maximum, [jnp.max(x, axis=-1, keepdims=True) for x in s]) for s in s_all]
    p_all = [[jnp.exp(x - m) for x in s] for s, m in zip(s_all, m_all)]
    l_all = [functools.reduce(jnp.add, [jnp.sum(x, axis=-1, keepdims=True) for x in p]) for p in p_all]
    shift = lambda x: jnp.concatenate(
        [pltpu.roll(x[:, 0:n_c], A_HEADS, 1), pltpu.roll(x[:, n_c:n_l], A_HEADS * SAMPLE_ROWS, 1)],
        axis=1)
    o_all = [functools.reduce(jnp.add, [jnp.dot(shift(x).astype(BF16), y, preferred_element_type=F32)
                                        for x, y in zip(p, kv2)]) for p, kv2 in zip(p_all, kv2_all)]
    outs = [o / l for o, l in zip(o_all, l_all)]
    lses = [m + jnp.log(l) for m, l in zip(m_all, l_all)]
    top = functools.reduce(jnp.maximum, lses)
    e = [jnp.exp(x - top) for x in lses]
    den = functools.reduce(jnp.add, e)
    o = functools.reduce(jnp.add, [(w / den) * x for w, x in zip(e, outs)])
    for hd in range(A_HEADS):
        o_ref[:, hd * A_HD:(hd + 1) * A_HD] = o[hd * SAMPLE_ROWS:(hd + 1) * SAMPLE_ROWS]


def _attn_sample(h, caches, bsz, t_new):
    kvh_n = 2 * A_HEADS
    n_seq = SAMPLE_ROWS // t_new
    assert SAMPLE_ROWS % t_new == 0 and bsz % n_seq == 0
    views = []
    specs = []
    for (win, dil), cache in zip(A_GROUPS, caches):
        assert cache.shape[1] == win and win // dil == SPAN and (dil == 1 or t_new <= dil)
        views.append(cache.reshape(bsz, SPAN, dil, kvh_n, A_HD))
        specs.append(pl.BlockSpec((n_seq, SPAN, min(dil, t_new), kvh_n, A_HD), lambda i: (i, 0, 0, 0, 0)))
    return pl.pallas_call(
        functools.partial(_attn_sample_kernel, t_new=t_new),
        grid=(bsz // n_seq,),
        in_specs=[pl.BlockSpec((SAMPLE_ROWS, N_PROJ32), lambda i: (i, 0))] + specs,
        out_specs=pl.BlockSpec((SAMPLE_ROWS, A_WIDTH), lambda i: (i, 0)),
        out_shape=jax.ShapeDtypeStruct((bsz * t_new, A_WIDTH), F32),
        compiler_params=_params("parallel"),
        name="attn_sample",
    )(h, *views)


def _final_kernel(x_ref, o_ref, bon_ref, zr_ref, oa_ref, zgg_ref, seg_ref, lnxg_ref, lnxb_ref, bg_ref,
                  woa_ref, wob_ref, wout_ref, lng_ref, lnb_ref, y_ref):
    seg2 = seg_ref[...]
    o = o_ref[...]
    mu = _seg_sum(o, seg2) * (1.0 / R_HD)
    d = o - mu
    var = _seg_sum(d * d, seg2) * (1.0 / R_HD)
    on = d * lax.rsqrt(var + GN_EPS) * lnxg_ref[...] + lnxb_ref[...]
    zr = zr_ref[...].astype(F32)
    y_r = (on + bon_ref[...].astype(F32)) * (zr * jax.nn.sigmoid(zr))

    z_a = zgg_ref[:, 0:A_WIDTH].astype(F32)
    y_a = oa_ref[...] * (z_a * jax.nn.sigmoid(z_a))

    g_r = zgg_ref[:, A_WIDTH:A_WIDTH + D_MODEL].astype(F32)
    g_a = zgg_ref[:, A_WIDTH + D_MODEL:A_WIDTH + 2 * D_MODEL].astype(F32)
    gate_r = jax.nn.sigmoid(g_r + bg_ref[:, 0:D_MODEL])
    gate_a = jax.nn.sigmoid(g_a + bg_ref[:, D_MODEL:2 * D_MODEL])
    mix = (gate_r * jnp.dot(y_r.astype(BF16), woa_ref[...], preferred_element_type=F32)
           + gate_a * jnp.dot(y_a.astype(BF16), wob_ref[...], preferred_element_type=F32))
    yy = ALPHA * x_ref[...] + jnp.dot(mix.astype(BF16), wout_ref[...], preferred_element_type=F32)
    mean = jnp.mean(yy, axis=-1, keepdims=True)
    cen = yy - mean
    variance = jnp.mean(cen * cen, axis=-1, keepdims=True)
    y_ref[...] = cen * lax.rsqrt(variance + LN_EPS) * lng_ref[...] + lnb_ref[...]


def _final(x, o, bonus, h16, o_attn, weights, tm):
    m = x.shape[0]
    row = lambda w, j=0: pl.BlockSpec((tm, w), lambda i: (i, j))
    const = lambda a: pl.BlockSpec(a.shape, lambda i: (0, 0))
    in_specs = ([row(D_MODEL), row(R_WIDTH), row(R_WIDTH), row(R_WIDTH, COL_ZR // R_WIDTH), row(A_WIDTH),
                 row(ZGG_W, COL_ZGG // ZGG_W)] + [const(a) for a in weights])
    return pl.pallas_call(
        _final_kernel,
        grid=(m // tm,),
        in_specs=in_specs,
        out_specs=row(D_MODEL),
        out_shape=jax.ShapeDtypeStruct((m, D_MODEL), F32),
        compiler_params=_params("parallel"),
        name="merge_out_ln",
    )(x, o, bonus, h16, o_attn, h16, *weights)


def _layer(x2, bsz, t, shift_prev, wkv0, caches, wts):
    m = x2.shape[0]
    h, h16 = _in_proj(x2, wts["w_proj"], 2048 if m % 2048 == 0 else m)

    tm = 1024 if m % 1024 == 0 and t % 1024 == 0 else (512 if m % 512 == 0 else m)
    if caches is None:
        first = None
    else:
        first = jnp.zeros((bsz, t, SHIFT_COLS), F32).at[:, 0].set(shift_prev).reshape(m, SHIFT_COLS)
    pack, lw, bonus = _rwkv_prep(h, first, t, wts["prep"], tm)

    if t % CHUNK == 0 and wkv0 is None:
        s0 = jnp.zeros((bsz, N_PAIRS, LANES, LANES), F32)
        o_seq, st = _rwkv_recurrence(pack.reshape(bsz, t, PACK_W), lw.reshape(bsz, t, R_WIDTH), s0, 2)
        o_rwkv = o_seq.reshape(m, R_WIDTH)
        wkv_new = _pairs_to_state(st)
    else:
        o_rwkv, wkv_new = _rwkv_short(pack, lw, wkv0, t)

    def rows_cols(r0, c0, width):
        if t % 8 == 0:
            return h.reshape(bsz, t, N_PROJ32)[:, r0:, c0:c0 + width]
        return h[:, c0:c0 + width].reshape(bsz, t, width)[:, r0:]

    if caches is None:
        o_attn = _attn_prompt(h, bsz, t)
    else:
        o_attn = _attn_sample(h, caches, bsz, t)

    tmf = 512 if m % 512 == 0 else m
    y = _final(x2, o_rwkv, bonus, h16, o_attn, wts["final"], tmf)

    new_kv = []
    for g, (win, _) in enumerate(A_GROUPS):
        keep = min(win, t) if caches is None else t
        new_kv.append(rows_cols(t - keep, _col_k(g), 2 * A_WIDTH).reshape(bsz, keep, 2, A_HEADS, A_HD))
    shift_new = rows_cols(t - 1, COL_ZS, SHIFT_COLS)[:, 0]
    return y.reshape(bsz, t, D_MODEL), new_kv, wkv_new, shift_new


def kernel(x_prompt, x_sample, cache_kv_g1, cache_kv_g2, cache_kv_g3, state_rwkv_wkv, state_rwkv_shift,
           w_in, b_gate, mu_shift, w0, w_w2, a0, w_a2, k_k, k_a, r_k, lnx_g, lnx_b,
           w_oa, w_ob, w_out, ln_g, ln_b):
    assert w_in.shape[0] == DEPTH
    bp, tp, _ = x_prompt.shape
    bs, ts, _ = x_sample.shape
    xp = x_prompt.reshape(bp * tp, D_MODEL)
    xs = x_sample.reshape(bs * ts, D_MODEL)
    head = lax.broadcasted_iota(jnp.int32, (LANES, LANES), 0) // R_HD
    seg = (head == head.T).astype(BF16)
    seg2 = jnp.concatenate([seg, seg], axis=0)
    row = lambda a: a.reshape(1, -1)
    acc = [[] for _ in range(12)]
    for l in range(DEPTH):
        wf = w_in[l]
        c0 = SHIFT_COLS
        c1 = c0 + R_WIDTH
        c2 = c1 + 3 * A_QKV_W
        qkv_col = lambda which, g: wf[:, c1 + (which * N_GROUPS + g) * A_WIDTH:
                                      c1 + (which * N_GROUPS + g + 1) * A_WIDTH]
        w_proj = jnp.concatenate(
            [wf[:, :c0], jnp.zeros((D_MODEL, COL_Q - c0), F32)] + [qkv_col(0, g) for g in range(N_GROUPS)]
            + [qkv_col(which, g) for g in range(N_GROUPS) for which in (1, 2)]
            + [jnp.zeros((D_MODEL, N_PROJ32 - COL_KV - 2 * A_QKV_W), F32), wf[:, c2:], wf[:, c0:c1]],
            axis=1).astype(BF16)
        zero = jnp.zeros((LORA, R_WIDTH), F32)
        w_lora = jnp.concatenate([jnp.concatenate([w_w2[l], zero], axis=1),
                                  jnp.concatenate([zero, w_a2[l]], axis=1)], axis=0)
        wl_hi = w_lora.astype(BF16)
        wl_lo = (w_lora - wl_hi.astype(F32)).astype(BF16)
        wts = {
            "w_proj": w_proj,
            "prep": (row(mu_shift[l]), row(w0[l]), row(a0[l]), jnp.concatenate([wl_hi, wl_hi], axis=0),
                     wl_lo, row(k_k[l]), row(k_a[l]), row(r_k[l]), seg2),
            "final": (seg2, row(lnx_g[l]), row(lnx_b[l]), row(b_gate[l]), w_oa[l].astype(BF16),
                      w_ob[l].astype(BF16), w_out[l].astype(BF16), row(ln_g[l]), row(ln_b[l])),
        }
        yp, kv_p, s_p, sh_p = _layer(xp, bp, tp, None, None, None, wts)
        ys, kv_s, s_s, sh_s = _layer(xs, bs, ts, state_rwkv_shift[l], state_rwkv_wkv[l],
                                     (cache_kv_g1[l], cache_kv_g2[l], cache_kv_g3[l]), wts)
        xp = yp.reshape(bp * tp, D_MODEL)
        xs = ys.reshape(bs * ts, D_MODEL)
        for idx, val in zip(range(2, 12), (kv_p[0], kv_s[0], kv_p[1], kv_s[1], kv_p[2], kv_s[2],
                                           s_p, s_s, sh_p, sh_s)):
            acc[idx].append(val)
    outs = [xp.reshape(bp, tp, D_MODEL), xs.reshape(bs, ts, D_MODEL)]
    outs += [jnp.stack(a) for a in acc[2:]]
    return tuple(outs)
```

```python
import functools
import math

import jax
import jax.numpy as jnp
from jax import lax
from jax.experimental import pallas as pl
from jax.experimental.pallas import tpu as pltpu

F32 = jnp.float32
BF16 = jnp.bfloat16

D_MODEL = 1024
R_HEADS = 8
R_HD = 64
R_WIDTH = R_HEADS * R_HD
LORA = 64
SHIFT_COLS = 3 * R_WIDTH + 2 * LORA
GN_EPS = 64e-5
A_GROUPS = ((128, 1), (512, 4), (2048, 16))
N_GROUPS = 3
A_HEADS = 4
A_HD = 128
A_WIDTH = A_HEADS * A_HD
A_QKV_W = N_GROUPS * A_WIDTH
SPAN = 128
DEPTH = 1
ALPHA = (2 * DEPTH) ** 0.25
LN_EPS = 1e-5
NEG = -1e30

LANES = 128
CHUNK = 64
N_PAIRS = R_HEADS // 2
VMEM_LIMIT = 48 * 1024 * 1024
PROJ_VMEM_LIMIT = 56 * 1024 * 1024

PROJ_TN = 1024
COL_ZS = 0
COL_Q = 2048
COL_KV = COL_Q + A_QKV_W
N_PROJ32 = -(-(COL_KV + 2 * A_QKV_W) // PROJ_TN) * PROJ_TN
ZGG_W = A_WIDTH + 2 * D_MODEL
COL_ZGG = 0
COL_ZR = ZGG_W
N_PROJ16 = COL_ZR + R_WIDTH
assert COL_Q % A_WIDTH == 0 and COL_KV % A_WIDTH == 0 and COL_ZR % R_WIDTH == 0
assert N_PROJ32 % PROJ_TN == 0 and N_PROJ16 % PROJ_TN == 0


def _col_q(g):
    return COL_Q + g * A_WIDTH


def _col_k(g):
    return COL_KV + g * 2 * A_WIDTH


def _col_v(g):
    return _col_k(g) + A_WIDTH


def _params(*sem):
    return pltpu.CompilerParams(dimension_semantics=sem, vmem_limit_bytes=VMEM_LIMIT)


def _split2(x):
    hi = x.astype(BF16)
    return hi, (x - hi.astype(F32)).astype(BF16)


def _seg_sum(x, seg2):
    cols = []
    for c in range(x.shape[1] // LANES):
        hi, lo = _split2(x[:, c * LANES:(c + 1) * LANES])
        cols.append(jnp.dot(jnp.concatenate([hi, lo], axis=1), seg2, preferred_element_type=F32))
    return jnp.concatenate(cols, axis=1)


def _mm_kernel(x_ref, w_ref, o32_ref, o16_ref, *, n32):
    j = pl.program_id(1)
    product = lambda: jnp.dot(x_ref[...].astype(BF16), w_ref[...], preferred_element_type=F32)

    @pl.when(j < n32)
    def _():
        o32_ref[...] = product()

    @pl.when(j >= n32)
    def _():
        o16_ref[...] = product().astype(BF16)


def _in_proj(x, w, tm):
    m, k = x.shape
    n32 = N_PROJ32 // PROJ_TN
    n16 = N_PROJ16 // PROJ_TN
    return pl.pallas_call(
        functools.partial(_mm_kernel, n32=n32),
        grid=(m // tm, n32 + n16),
        in_specs=[pl.BlockSpec((tm, k), lambda i, j: (i, 0)),
                  pl.BlockSpec((k, PROJ_TN), lambda i, j: (0, j))],
        out_specs=[pl.BlockSpec((tm, PROJ_TN), lambda i, j: (i, jnp.minimum(j, n32 - 1))),
                   pl.BlockSpec((tm, PROJ_TN), lambda i, j: (i, jnp.maximum(j - n32, 0)))],
        out_shape=[jax.ShapeDtypeStruct((m, N_PROJ32), F32), jax.ShapeDtypeStruct((m, N_PROJ16), BF16)],
        compiler_params=pltpu.CompilerParams(dimension_semantics=("parallel", "arbitrary"),
                                             vmem_limit_bytes=PROJ_VMEM_LIMIT),
        name="in_proj",
    )(x, w)


PACK_R, PACK_K, PACK_V, PACK_KK, PACK_B = (i * R_WIDTH for i in range(5))
PACK_W = 5 * R_WIDTH


def _prep_math(zs, prev, mu, w0, a0, wl_a, wl_b, k_k, k_a, r_k, seg2, outs):
    pack_o, lw_o, bon_o = outs
    zsm = zs + mu * (prev - zs)
    r = zsm[:, 0:R_WIDTH]
    k = zsm[:, R_WIDTH:2 * R_WIDTH]
    v = zsm[:, 2 * R_WIDTH:3 * R_WIDTH]
    slab = zsm[:, 3 * R_WIDTH:SHIFT_COLS]
    lane = lax.broadcasted_iota(jnp.int32, slab.shape, 1)
    slab = jnp.where(lane < LORA, jnp.tanh(slab), slab)
    s_hi, s_lo = _split2(slab)
    lin = (jnp.dot(jnp.concatenate([s_hi, s_lo], axis=1), wl_a, preferred_element_type=F32)
           + jnp.dot(s_hi, wl_b, preferred_element_type=F32))
    wlin = w0 + lin[:, 0:R_WIDTH]
    alin = a0 + lin[:, R_WIDTH:2 * R_WIDTH]
    nw = -wlin
    softplus = jnp.maximum(nw, 0.0) + jnp.log1p(jnp.exp(-jnp.abs(nw)))
    w = -softplus - 0.5
    lw = -jnp.exp(w)
    a = jax.nn.sigmoid(alin)
    kkr = k * k_k
    kk = kkr / jnp.maximum(jnp.sqrt(_seg_sum(kkr * kkr, seg2)), 1e-12)
    kp = k * (1.0 + (a - 1.0) * k_a)
    for slot, val in enumerate((r, kp, v, kk, kk * a)):
        pack_o[:, slot * R_WIDTH:(slot + 1) * R_WIDTH] = val.astype(pack_o.dtype)
    lw_o[...] = lw
    bon_o[...] = (_seg_sum(r * kp * r_k, seg2) * v).astype(bon_o.dtype)


def _prep_prompt_kernel(zs_ref, p8_ref, mu_ref, w0_ref, a0_ref, wla_ref, wlb_ref, kk_ref, ka_ref,
                        rk_ref, seg_ref, *outs, tiles_per_seq):
    i = pl.program_id(0)
    zs = zs_ref[...]
    rolled = pltpu.roll(zs, 1, 0)
    carry = jnp.where(i % tiles_per_seq == 0, 0.0, p8_ref[7:8, :])
    row = lax.broadcasted_iota(jnp.int32, zs.shape, 0)
    prev = jnp.where(row == 0, carry, rolled)
    _prep_math(zs, prev, mu_ref[...], w0_ref[...], a0_ref[...], wla_ref[...], wlb_ref[...],
               kk_ref[...], ka_ref[...], rk_ref[...], seg_ref[...], outs)


def _prep_sample_kernel(zs_ref, first_ref, mu_ref, w0_ref, a0_ref, wla_ref, wlb_ref, kk_ref, ka_ref,
                        rk_ref, seg_ref, *outs, seq_len):
    zs = zs_ref[...]
    rolled = pltpu.roll(zs, 1, 0)
    row = lax.broadcasted_iota(jnp.int32, zs.shape, 0)
    prev = jnp.where(row % seq_len == 0, first_ref[...], rolled)
    _prep_math(zs, prev, mu_ref[...], w0_ref[...], a0_ref[...], wla_ref[...], wlb_ref[...],
               kk_ref[...], ka_ref[...], rk_ref[...], seg_ref[...], outs)


def _rwkv_prep(h, first, seq_len, weights, tm):
    m = h.shape[0]
    const = lambda a: pl.BlockSpec(a.shape, lambda i: (0, 0))
    row_spec = pl.BlockSpec((tm, SHIFT_COLS), lambda i: (i, COL_ZS))
    if first is None:
        kern = functools.partial(_prep_prompt_kernel, tiles_per_seq=seq_len // tm)
        second = h
        second_spec = pl.BlockSpec((8, SHIFT_COLS), lambda i: (jnp.maximum(i * (tm // 8) - 1, 0), COL_ZS))
    else:
        kern = functools.partial(_prep_sample_kernel, seq_len=seq_len)
        second = first
        second_spec = pl.BlockSpec((tm, SHIFT_COLS), lambda i: (i, 0))
    out_spec = lambda w: pl.BlockSpec((tm, w), lambda i: (i, 0))
    return pl.pallas_call(
        kern,
        grid=(m // tm,),
        in_specs=[row_spec, second_spec] + [const(a) for a in weights],
        out_specs=[out_spec(PACK_W), out_spec(R_WIDTH), out_spec(R_WIDTH)],
        out_shape=[jax.ShapeDtypeStruct((m, PACK_W), BF16), jax.ShapeDtypeStruct((m, R_WIDTH), F32),
                   jax.ShapeDtypeStruct((m, R_WIDTH), BF16)],
        compiler_params=_params("parallel"),
        name="rwkv_prep",
    )(h, second, *weights)


CHUNKS_PER_STEP = 4


def _rwkv_chunk_kernel(*refs, nb):
    ins, s0_ref, o_ref, st_ref = refs[0:12], refs[12], refs[13], refs[14]
    yo_sc, ti_sc, mb_sc, ut_sc, vs_sc, dec_sc = refs[15:21]
    n = pl.program_id(1)
    c = CHUNK
    c2 = 2 * c
    lane = lax.broadcasted_iota(jnp.int32, (1, LANES), 1)
    m_lo = (lane < R_HD).astype(F32)
    m_hi = 1.0 - m_lo
    row_c = lax.broadcasted_iota(jnp.int32, (c, 4 * c), 0)
    col_c = lax.broadcasted_iota(jnp.int32, (c, 4 * c), 1)
    tri3 = ((row_c >= col_c % c) & (col_c < 3 * c)).astype(BF16)
    row2 = lax.broadcasted_iota(jnp.int32, (c2, c2), 0)
    col2 = lax.broadcasted_iota(jnp.int32, (c2, c2), 1)
    strict = row2 > col2
    incl = row2 >= col2
    eye = (row2 == col2).astype(F32)

    def stack(x):
        return jnp.concatenate([x * m_lo, x * m_hi], axis=0)

    def mm(a, b):
        return jnp.dot(a.astype(BF16), b.astype(BF16), preferred_element_type=F32)

    def nt(a, b):
        return lax.dot_general(a.astype(BF16), b.astype(BF16), (((1,), (1,)), ((), ())),
                               preferred_element_type=F32)

    each = lambda f, *xs: [f(*a) for a in zip(*xs)]
    chains = [(bi, slice(p * LANES, (p + 1) * LANES), p) for bi in range(nb) for p in range(N_PAIRS)]

    def cumsum(lw_all):
        hi = lw_all.astype(BF16)
        r1 = lw_all - hi.astype(F32)
        mid = r1.astype(BF16)
        lo = (r1 - mid.astype(F32)).astype(BF16)
        return jnp.dot(tri3, jnp.concatenate([hi, mid, lo, jnp.zeros_like(lo)], axis=0),
                       preferred_element_type=F32)

    def prepare(refs, row0, slot):
        pack_ref, lw_ref = refs
        rs = slice(row0, row0 + c)
        g_all = [cumsum(lw_ref[bi, rs, :]) for bi in range(nb)]
        yield
        load = lambda c0: [pack_ref[bi, rs, c0 + sl.start:c0 + sl.stop].astype(F32) for bi, sl, _ in chains]
        r, kp, v, kk, b = (load(c0) for c0 in (PACK_R, PACK_K, PACK_V, PACK_KK, PACK_B))
        lw = [lw_ref[bi, rs, sl] for bi, sl, _ in chains]
        g = [g_all[bi][:, sl] for bi, sl, _ in chains]
        g_last = each(lambda x: x[c - 1:c, :], g)
        e_g = each(jnp.exp, g)
        e_ng = each(lambda x: jnp.exp(-x), g)
        e_gm = each(lambda x, y: jnp.exp(x - y), g, lw)
        e_cg = each(lambda x, y: jnp.exp(y - x), g, g_last)
        ad_s = each(lambda x, y: stack(-(x * y)), kk, e_gm)
        rd_s = each(lambda x, y: stack(x * y), r, e_g)
        bd_s = each(lambda x, y: stack(x * y), b, e_ng)
        kd_s = each(lambda x, y: stack(x * y), kp, e_ng)
        be_s = each(lambda x, y: stack(x * y), b, e_cg)
        ke_s = each(lambda x, y: stack(x * y), kp, e_cg)
        for idx in range(len(chains)):
            yo_sc[slot, idx, 0:c2, 0:c2] = ad_s[idx].astype(BF16)
            yo_sc[slot, idx, c2:2 * c2, 0:c2] = rd_s[idx].astype(BF16)
            ut_sc[slot, idx, :, 0:c2] = jnp.transpose(be_s[idx]).astype(BF16)
            ut_sc[slot, idx, :, c2:2 * c2] = jnp.transpose(ke_s[idx]).astype(BF16)
            vs_sc[slot, idx] = stack(v[idx]).astype(BF16)
            dec_sc[slot, idx] = jnp.transpose(jnp.broadcast_to(jnp.exp(g_last[idx]), (LANES, LANES)))
        yield
        z = each(lambda a1, a2, b1, b2: nt(jnp.concatenate([a1, a2], axis=0),
                                           jnp.concatenate([b1, b2], axis=0)), ad_s, rd_s, bd_s, kd_s)
        yield
        for idx, zz in enumerate(z):
            yo_sc[slot, idx, 0:c2, c2:2 * c2] = jnp.where(strict, zz[0:c2, c2:2 * c2], 0.0).astype(BF16)
            yo_sc[slot, idx, c2:2 * c2, c2:2 * c2] = jnp.where(
                incl, zz[c2:2 * c2, c2:2 * c2], 0.0).astype(BF16)
            mb_sc[slot, idx] = jnp.where(incl, zz[c2:2 * c2, 0:c2], 0.0).astype(BF16)
        n_b = each(lambda x: jnp.where(strict, x[0:c2, 0:c2], 0.0), z)
        t_inv = each(lambda x: eye + x, n_b)
        q = each(lambda x: mm(x, x), n_b)
        yield
        steps = int(math.log2(c))
        for i in range(1, steps):
            if i < steps - 1:
                res = each(lambda x, tt: mm(x, jnp.concatenate([x, tt], axis=1)), q, t_inv)
                q = each(lambda x: x[:, 0:c2], res)
                t_inv = each(lambda tt, x: tt + x[:, c2:2 * c2], t_inv, res)
            else:
                t_inv = each(lambda tt, x: tt + mm(x, tt), t_inv, q)
            yield
        for idx, tt in enumerate(t_inv):
            ti_sc[slot, idx] = tt.astype(BF16)

    def advance(slot, row0):
        idxs = list(range(len(chains)))
        st = [st_ref[bi, p] for bi, _, p in chains]
        vs = [vs_sc[slot, i] for i in idxs]
        yo = [jnp.dot(yo_sc[slot, i], jnp.concatenate([s.astype(BF16), w], axis=0),
                      preferred_element_type=F32) for i, s, w in zip(idxs, st, vs)]
        yield
        u = [jnp.dot(ti_sc[slot, i], x[0:c2].astype(BF16), preferred_element_type=F32)
             for i, x in zip(idxs, yo)]
        ub = [x.astype(BF16) for x in u]
        yield
        o_bd = [x[c2:2 * c2] + jnp.dot(mb_sc[slot, i], w, preferred_element_type=F32)
                for i, x, w in zip(idxs, yo, ub)]
        for (bi, sl, _), o in zip(chains, o_bd):
            o_ref[bi, row0:row0 + c, sl] = o[0:c] + o[c:c2]
        yield
        upd = [jnp.dot(ut_sc[slot, i], jnp.concatenate([w, x], axis=0), preferred_element_type=F32)
               for i, w, x in zip(idxs, ub, vs)]
        for i, ((bi, _, p), s, d) in enumerate(zip(chains, st, upd)):
            st_ref[bi, p] = dec_sc[slot, i] * s + d

    def interleave(*stages):
        live = list(stages)
        while live:
            for gen in list(live):
                if next(gen, StopIteration) is StopIteration:
                    live.remove(gen)

    @pl.when(n == 0)
    def _():
        st_ref[...] = s0_ref[...]
        interleave(prepare(ins[0:2], 0, 0), prepare(ins[2:4], 0, 1))

    def in_turn(*stages):
        for gen in stages:
            yield from gen

    interleave(in_turn(advance(0, 0), advance(1, c)), prepare(ins[4:6], 0, 2), prepare(ins[6:8], 0, 3))
    interleave(in_turn(advance(2, 2 * c), advance(3, 3 * c)), prepare(ins[8:10], 0, 0), prepare(ins[10:12], 0, 1))


def _rwkv_recurrence(pack, lw, s0, nb):
    bsz, t, _ = lw.shape
    assert CHUNKS_PER_STEP == 4
    rows = CHUNKS_PER_STEP * CHUNK
    nsteps = t // rows
    last = t // CHUNK - 1
    chunk = lambda which: [pl.BlockSpec((nb, CHUNK, w), lambda i, n: (i, which(n), 0))
                           for w in (PACK_W, R_WIDTH)]
    chunk_specs = (chunk(lambda n: 0) + chunk(lambda n: 1) + chunk(lambda n: 4 * n + 2)
                   + chunk(lambda n: 4 * n + 3) + chunk(lambda n: jnp.minimum(4 * n + 4, last))
                   + chunk(lambda n: jnp.minimum(4 * n + 5, last)))
    seq_spec = pl.BlockSpec((nb, rows, R_WIDTH), lambda i, n: (i, n, 0))
    st_spec = pl.BlockSpec((nb, N_PAIRS, LANES, LANES), lambda i, n: (i, 0, 0, 0))
    nchain = nb * N_PAIRS
    c2 = 2 * CHUNK
    slots = CHUNKS_PER_STEP
    scratch = [pltpu.VMEM((slots, nchain, 2 * c2, 2 * c2), BF16),
               pltpu.VMEM((slots, nchain, c2, c2), BF16),
               pltpu.VMEM((slots, nchain, c2, c2), BF16),
               pltpu.VMEM((slots, nchain, c2, 2 * c2), BF16),
               pltpu.VMEM((slots, nchain, c2, c2), BF16),
               pltpu.VMEM((slots, nchain, LANES, LANES), F32)]
    return pl.pallas_call(
        functools.partial(_rwkv_chunk_kernel, nb=nb),
        grid=(bsz // nb, nsteps),
        in_specs=chunk_specs + [st_spec],
        out_specs=[seq_spec, st_spec],
        out_shape=[jax.ShapeDtypeStruct((bsz, t, R_WIDTH), F32),
                   jax.ShapeDtypeStruct(s0.shape, F32)],
        scratch_shapes=scratch,
        compiler_params=_params("parallel", "arbitrary"),
        name="rwkv_chunks",
    )(*([pack, lw] * 6), s0)


def _pairs_to_state(st):
    bsz = st.shape[0]
    s6 = st.reshape(bsz, N_PAIRS, 2, R_HD, 2, R_HD)
    s = jnp.stack([s6[:, :, 0, :, 0, :], s6[:, :, 1, :, 1, :]], axis=2)
    return jnp.swapaxes(s.reshape(bsz, R_HEADS, R_HD, R_HD), -1, -2)


SHORT_ROWS = 64


def _rwkv_short_kernel(pack_ref, lw_ref, s_ref, o_ref, sn_ref, *, t_len):
    rows = SHORT_ROWS
    ng = rows // t_len
    r2 = 2 * rows
    assert 2 * t_len == 8 and r2 == LANES
    lane = lax.broadcasted_iota(jnp.int32, (1, LANES), 1)
    m_lo = (lane < R_HD).astype(F32)
    m_hi = 1.0 - m_lo
    lane_blk = lane // t_len
    row_c = lax.broadcasted_iota(jnp.int32, (rows, 4 * rows), 0)
    col_c = lax.broadcasted_iota(jnp.int32, (rows, 4 * rows), 1)
    j_c = col_c % rows
    same_c = ((row_c // t_len) == (j_c // t_len)) & (col_c < 3 * rows)
    cum_lhs = jnp.concatenate([(same_c & (row_c >= j_c)).astype(BF16), same_c.astype(BF16)], axis=0)
    row2 = lax.broadcasted_iota(jnp.int32, (r2, r2), 0)
    col2 = lax.broadcasted_iota(jnp.int32, (r2, r2), 1)
    blk = (row2 // t_len) == (col2 // t_len)
    strict = blk & (row2 > col2)
    incl = blk & (row2 >= col2)
    first_seq = lax.broadcasted_iota(jnp.int32, (8, LANES), 0) < t_len
    zero_half = jnp.zeros((R_HD, R_HD), F32)

    def stack(x):
        return jnp.concatenate([x * m_lo, x * m_hi], axis=0)

    def half(x):
        return jnp.concatenate([x * m_lo, pltpu.roll(x, R_HD, 1) * m_lo], axis=0)

    def mm(a, b):
        return jnp.dot(a.astype(BF16), b.astype(BF16), preferred_element_type=F32)

    def nt(a, b):
        return lax.dot_general(a.astype(BF16), b.astype(BF16), (((1,), (1,)), ((), ())),
                               preferred_element_type=F32)

    each = lambda f, *xs: [f(*a) for a in zip(*xs)]

    lw_all = lw_ref[...]
    hi = lw_all.astype(BF16)
    r1 = lw_all - hi.astype(F32)
    mid = r1.astype(BF16)
    lo = (r1 - mid.astype(F32)).astype(BF16)
    g_both = jnp.dot(cum_lhs, jnp.concatenate([hi, mid, lo, jnp.zeros_like(lo)], axis=0),
                     preferred_element_type=F32)

    pairs = list(range(N_PAIRS))
    sls = [slice(p * LANES, (p + 1) * LANES) for p in pairs]
    load = lambda c0: [pack_ref[:, c0 + sl.start:c0 + sl.stop].astype(F32) for sl in sls]
    r, kp, v, kk, b = (load(c0) for c0 in (PACK_R, PACK_K, PACK_V, PACK_KK, PACK_B))
    lw = [lw_ref[:, sl] for sl in sls]
    g = [g_both[0:rows, sl] for sl in sls]
    g_tot = [g_both[rows:r2, sl] for sl in sls]
    e_g = each(jnp.exp, g)
    e_ng = each(lambda x: jnp.exp(-x), g)
    e_gm = each(lambda x, y: jnp.exp(x - y), g, lw)
    e_cg = each(lambda x, y: jnp.exp(y - x), g, g_tot)
    ad = each(lambda x, y: -(x * y), kk, e_gm)
    rd = each(lambda x, y: x * y, r, e_g)
    bd = each(lambda x, y: x * y, b, e_ng)
    kd = each(lambda x, y: x * y, kp, e_ng)
    be_h = each(lambda x, y: half(x * y), b, e_cg)
    ke_h = each(lambda x, y: half(x * y), kp, e_cg)
    v_h = each(half, v)
    ad_h = each(half, ad)
    rd_h = each(half, rd)
    dec_h = each(lambda x: half(jnp.exp(x)), g_tot)

    z = each(lambda a1, a2, b1, b2: nt(jnp.concatenate([stack(a1), stack(a2)], axis=0),
                                       jnp.concatenate([stack(b1), stack(b2)], axis=0)), ad, rd, bd, kd)
    n_b = each(lambda x: jnp.where(strict, x[0:r2, 0:r2], 0.0), z)
    n_k = each(lambda x: jnp.where(strict, x[0:r2, r2:2 * r2], 0.0), z)
    m_b = each(lambda x: jnp.where(incl, x[r2:2 * r2, 0:r2], 0.0), z)
    m_k = each(lambda x: jnp.where(incl, x[r2:2 * r2, r2:2 * r2], 0.0), z)

    y0, o0 = [], []
    for p in pairs:
        y_tiles, o_tiles = [], []
        for hh in range(2):
            for i in range(ng // 2):
                rs = slice(hh * rows + 8 * i, hh * rows + 8 * i + 8)
                lhs = jnp.concatenate([ad_h[p][rs], rd_h[p][rs]], axis=0)[:, 0:R_HD]
                res = [nt(lhs, jnp.concatenate([s_ref[2 * i + w, 2 * p + hh], zero_half], axis=0))
                       for w in range(2)]
                y_tiles.append(jnp.where(first_seq, res[0][0:8], res[1][0:8]))
                o_tiles.append(jnp.where(first_seq, res[0][8:16], res[1][8:16]))
        y0.append(jnp.concatenate(y_tiles, axis=0))
        o0.append(jnp.concatenate(o_tiles, axis=0))

    y = each(lambda a, nk, vh: a + mm(nk, vh), y0, n_k, v_h)
    q = n_b
    steps = int(math.log2(t_len))
    for i in range(steps):
        if i < steps - 1:
            res = each(lambda x, yy: mm(x, jnp.concatenate([x, yy], axis=1)), q, y)
            q = each(lambda x: x[:, 0:r2], res)
            y = each(lambda yy, x: yy + x[:, r2:2 * r2], y, res)
        else:
            y = each(lambda yy, x: yy + mm(x, yy), y, q)
    u = y

    o_h = each(lambda a, mb, uu, mk, vh: a + mm(mb, uu) + mm(mk, vh), o0, m_b, u, m_k, v_h)
    for sl, o in zip(sls, o_h):
        o_ref[:, sl] = o[0:rows] + pltpu.roll(o[rows:r2], R_HD, 1)

    for p in pairs:
        ut = jnp.transpose(u[p])[0:R_HD]
        vt = jnp.transpose(v_h[p])[0:R_HD]
        rhs = jnp.concatenate([be_h[p], ke_h[p]], axis=0)
        keys = [(hh, s) for hh in range(2) for s in range(ng)]
        sel = [(lane_blk == hh * ng + s).astype(F32) for hh, s in keys]
        lhs = jnp.concatenate([jnp.concatenate([ut * m, vt * m], axis=1) for m in sel], axis=0)
        upd = mm(lhs, rhs)
        for idx, (hh, s) in enumerate(keys):
            row0 = hh * rows + s * t_len
            dec = dec_h[p][row0:row0 + 1, 0:R_HD]
            sn_ref[s, 2 * p + hh] = (s_ref[s, 2 * p + hh] * dec
                                     + upd[idx * R_HD:(idx + 1) * R_HD, 0:R_HD])


def _rwkv_short(pack, lw, state, t_len):
    m = lw.shape[0]
    ng = SHORT_ROWS // t_len
    seq_spec = pl.BlockSpec((SHORT_ROWS, R_WIDTH), lambda i: (i, 0))
    pack_spec = pl.BlockSpec((SHORT_ROWS, PACK_W), lambda i: (i, 0))
    st_spec = pl.BlockSpec((ng, R_HEADS, R_HD, R_HD), lambda i: (i, 0, 0, 0))
    return pl.pallas_call(
        functools.partial(_rwkv_short_kernel, t_len=t_len),
        grid=(m // SHORT_ROWS,),
        in_specs=[pack_spec, seq_spec, st_spec],
        out_specs=[seq_spec, st_spec],
        out_shape=[jax.ShapeDtypeStruct((m, R_WIDTH), F32), jax.ShapeDtypeStruct(state.shape, F32)],
        compiler_params=_params("parallel"),
        name="rwkv_short",
    )(pack, lw, state)


ATTN_ROWS = SPAN * max(d for _, d in A_GROUPS)
ATTN_BATCH = 8


def _attn_prompt_kernel(*refs):
    ins = refs[:5 * N_GROUPS]
    o_ref = refs[5 * N_GROUPS]
    o_sc = refs[5 * N_GROUPS + 1:5 * N_GROUPS + 1 + N_GROUPS]
    l_sc = refs[5 * N_GROUPS + 1 + N_GROUPS:]
    n = pl.program_id(1)
    i = lax.broadcasted_iota(jnp.int32, (SPAN, 2 * SPAN), 0)
    j = lax.broadcasted_iota(jnp.int32, (SPAN, 2 * SPAN), 1)
    dist = SPAN + i - j
    band = (dist >= 0) & (dist <= SPAN)
    band_first = band & ((n > 0) | (j >= SPAN))
    scale = 1.0 / math.sqrt(A_HD)
    cat = lambda a, b: jnp.concatenate([a, b], axis=0).astype(BF16)

    for g, (_, dil) in enumerate(A_GROUPS):
        q_ref, kp_ref, kc_ref, vp_ref, vc_ref = ins[5 * g:5 * g + 5]

        def rows(run, rho):
            start = run * SPAN * dil + rho
            return pl.ds(start, SPAN, stride=dil) if dil > 1 else pl.ds(start, SPAN)

        all_probs = [(run, rho) for run in range(ATTN_ROWS // (SPAN * dil)) for rho in range(dil)]
        for b0 in range(0, len(all_probs), ATTN_BATCH):
            probs = all_probs[b0:b0 + ATTN_BATCH]
            q = [q_ref[rows(run, rho), :].astype(BF16) for run, rho in probs]
            k = [cat(kp_ref[rows(0, rho), :] if run == 0 else kc_ref[rows(run - 1, rho), :],
                     kc_ref[rows(run, rho), :]) for run, rho in probs]
            v = [cat(vp_ref[rows(0, rho), :] if run == 0 else vc_ref[rows(run - 1, rho), :],
                     vc_ref[rows(run, rho), :]) for run, rho in probs]
            s = [lax.dot_general(a, b, (((1,), (1,)), ((), ())), preferred_element_type=F32) * scale
                 for a, b in zip(q, k)]
            s = [jnp.where(band_first if run == 0 else band, x, NEG) for x, (run, _) in zip(s, probs)]
            m = [jnp.max(x, axis=-1, keepdims=True) for x in s]
            p = [jnp.exp(x - y) for x, y in zip(s, m)]
            l = [jnp.sum(x, axis=-1, keepdims=True) for x in p]
            o = [jnp.dot(x.astype(BF16), y, preferred_element_type=F32) for x, y in zip(p, v)]
            for (run, rho), oo, ll, mx in zip(probs, o, l, m):
                o_sc[g][rows(run, rho), :] = oo / ll
                l_sc[g][rows(run, rho), :] = jnp.broadcast_to(mx + jnp.log(ll), (SPAN, A_HD))

    lse = [ref[...] for ref in l_sc]
    top = functools.reduce(jnp.maximum, lse)
    e = [jnp.exp(x - top) for x in lse]
    den = functools.reduce(jnp.add, e)
    o_ref[...] = functools.reduce(jnp.add, [(w / den) * ref[...] for w, ref in zip(e, o_sc)])


def _attn_prompt(h, bsz, t):
    nblk = t // ATTN_ROWS
    in_specs = []
    for g, (_, dil) in enumerate(A_GROUPS):
        prev_rows = SPAN * dil
        per_blk = ATTN_ROWS // prev_rows
        col = lambda which, hd, g=g: (_col_q, _col_k, _col_v)[which](g) // A_HD + hd
        cur = lambda which, col=col: pl.BlockSpec(
            (ATTN_ROWS, A_HD), lambda b, n, hd: (b * nblk + n, col(which, hd)))
        prev = lambda which, col=col, per_blk=per_blk, prev_rows=prev_rows: pl.BlockSpec(
            (prev_rows, A_HD),
            lambda b, n, hd: (jnp.maximum((b * nblk + n) * per_blk - 1, 0), col(which, hd)))
        in_specs += [cur(0), prev(1), cur(1), prev(2), cur(2)]
    return pl.pallas_call(
        _attn_prompt_kernel,
        grid=(bsz, nblk, A_HEADS),
        in_specs=in_specs,
        out_specs=pl.BlockSpec((ATTN_ROWS, A_HD), lambda b, n, hd: (b * nblk + n, hd)),
        out_shape=jax.ShapeDtypeStruct((bsz * t, A_WIDTH), F32),
        scratch_shapes=[pltpu.VMEM((ATTN_ROWS, A_HD), F32)] * (2 * N_GROUPS),
        compiler_params=_params("parallel", "arbitrary", "arbitrary"),
        name="attn_prompt",
    )(*([h] * (5 * N_GROUPS)))


SAMPLE_ROWS = 8


def _attn_sample_kernel(*refs, t_new):
    h_ref, c_refs, o_ref = refs[0], refs[1:4], refs[4]
    scale = 1.0 / math.sqrt(A_HD)
    kvh_n = 2 * A_HEADS
    n_seq = SAMPLE_ROWS // t_new
    n_q = A_HEADS * SAMPLE_ROWS
    n_c = SPAN * kvh_n
    n_l = n_c + LANES
    row = lax.broadcasted_iota(jnp.int32, (n_q, n_l), 0)
    lane = lax.broadcasted_iota(jnp.int32, (n_q, n_l), 1)
    h_r, seq_r, t_r = row // SAMPLE_ROWS, (row % SAMPLE_ROWS) // t_new, row % t_new
    in_buf = lane < n_c
    j_l = lane // kvh_n
    new = lane - n_c
    t_l = new % t_new
    own = (in_buf & ((lane % kvh_n) == h_r)) | (
        ~in_buf & ((new // SAMPLE_ROWS) == h_r) & (((new % SAMPLE_ROWS) // t_new) == seq_r))
    n_pad = LANES - kvh_n * SAMPLE_ROWS
    pad = [jnp.zeros((n_pad, A_HD), F32)] if n_pad else []
    heads = lambda c0: [h_ref[:, c0 + hd * A_HD:c0 + (hd + 1) * A_HD] for hd in range(A_HEADS)]
    bias = {}
    for seq in range(n_seq):
        dense = own & (seq_r == seq) & ((in_buf & (j_l >= t_r)) | (~in_buf & (t_l <= t_r)))
        bias[seq, None] = jnp.where(dense, 0.0, NEG)
        for t in range(t_new):
            dilated = own & (seq_r == seq) & (t_r == t) & (in_buf | (t_l == t))
            bias[seq, t] = jnp.where(dilated, 0.0, NEG)
    groups = range(N_GROUPS)
    kv2_all, s_all = [], []
    for g in groups:
        q = jnp.concatenate(heads(_col_q(g)), axis=0).astype(BF16)
        kv_new = jnp.concatenate(heads(_col_k(g)) + heads(_col_v(g)) + pad, axis=0)
        kv2, s = [], []
        for seq in range(n_seq):
            for t in ([None] if g == 0 else range(t_new)):
                buf = c_refs[g][seq, :, 0 if t is None else t].reshape(n_c, A_HD)
                kv2.append(jnp.concatenate([buf, kv_new], axis=0).astype(BF16))
                st = lax.dot_general(q, kv2[-1], (((1,), (1,)), ((), ())), preferred_element_type=F32)
                s.append(st * scale + bias[seq, t])
        kv2_all.append(kv2)
        s_all.append(s)
    m_all = [functools.reduce(jnp.maximum, [jnp.max(x, axis=-1, keepdims=True) for x in s]) for s in s_all]
    p_all = [[jnp.exp(x - m) for x in s] for s, m in zip(s_all, m_all)]
    l_all = [functools.reduce(jnp.add, [jnp.sum(x, axis=-1, keepdims=True) for x in p]) for p in p_all]
    shift = lambda x: jnp.concatenate(
        [pltpu.roll(x[:, 0:n_c], A_HEADS, 1), pltpu.roll(x[:, n_c:n_l], A_HEADS * SAMPLE_ROWS, 1)],
        axis=1)
    o_all = [functools.reduce(jnp.add, [jnp.dot(shift(x).astype(BF16), y, preferred_element_type=F32)
                                        for x, y in zip(p, kv2)]) for p, kv2 in zip(p_all, kv2_all)]
    outs = [o / l for o, l in zip(o_all, l_all)]
    lses = [m + jnp.log(l) for m, l in zip(m_all, l_all)]
    top = functools.reduce(jnp.maximum, lses)
    e = [jnp.exp(x - top) for x in lses]
    den = functools.reduce(jnp.add, e)
    o = functools.reduce(jnp.add, [(w / den) * x for w, x in zip(e, outs)])
    for hd in range(A_HEADS):
        o_ref[:, hd * A_HD:(hd + 1) * A_HD] = o[hd * SAMPLE_ROWS:(hd + 1) * SAMPLE_ROWS]


def _attn_sample(h, caches, bsz, t_new):
    kvh_n = 2 * A_HEADS
    n_seq = SAMPLE_ROWS // t_new
    assert SAMPLE_ROWS % t_new == 0 and bsz % n_seq == 0
    views = []
    specs = []
    for (win, dil), cache in zip(A_GROUPS, caches):
        assert cache.shape[1] == win and win // dil == SPAN and (dil == 1 or t_new <= dil)
        views.append(cache.reshape(bsz, SPAN, dil, kvh_n, A_HD))
        specs.append(pl.BlockSpec((n_seq, SPAN, min(dil, t_new), kvh_n, A_HD), lambda i: (i, 0, 0, 0, 0)))
    return pl.pallas_call(
        functools.partial(_attn_sample_kernel, t_new=t_new),
        grid=(bsz // n_seq,),
        in_specs=[pl.BlockSpec((SAMPLE_ROWS, N_PROJ32), lambda i: (i, 0))] + specs,
        out_specs=pl.BlockSpec((SAMPLE_ROWS, A_WIDTH), lambda i: (i, 0)),
        out_shape=jax.ShapeDtypeStruct((bsz * t_new, A_WIDTH), F32),
        compiler_params=_params("parallel"),
        name="attn_sample",
    )(h, *views)


def _final_kernel(x_ref, o_ref, bon_ref, zr_ref, oa_ref, zgg_ref, seg_ref, lnxg_ref, lnxb_ref, bg_ref,
                  woa_ref, wob_ref, wout_ref, lng_ref, lnb_ref, y_ref):
    seg2 = seg_ref[...]
    o = o_ref[...]
    mu = _seg_sum(o, seg2) * (1.0 / R_HD)
    d = o - mu
    var = _seg_sum(d * d, seg2) * (1.0 / R_HD)
    on = d * lax.rsqrt(var + GN_EPS) * lnxg_ref[...] + lnxb_ref[...]
    zr = zr_ref[...].astype(F32)
    y_r = (on + bon_ref[...].astype(F32)) * (zr * jax.nn.sigmoid(zr))

    z_a = zgg_ref[:, 0:A_WIDTH].astype(F32)
    y_a = oa_ref[...] * (z_a * jax.nn.sigmoid(z_a))

    g_r = zgg_ref[:, A_WIDTH:A_WIDTH + D_MODEL].astype(F32)
    g_a = zgg_ref[:, A_WIDTH + D_MODEL:A_WIDTH + 2 * D_MODEL].astype(F32)
    gate_r = jax.nn.sigmoid(g_r + bg_ref[:, 0:D_MODEL])
    gate_a = jax.nn.sigmoid(g_a + bg_ref[:, D_MODEL:2 * D_MODEL])
    mix = (gate_r * jnp.dot(y_r.astype(BF16), woa_ref[...], preferred_element_type=F32)
           + gate_a * jnp.dot(y_a.astype(BF16), wob_ref[...], preferred_element_type=F32))
    yy = ALPHA * x_ref[...] + jnp.dot(mix.astype(BF16), wout_ref[...], preferred_element_type=F32)
    mean = jnp.mean(yy, axis=-1, keepdims=True)
    cen = yy - mean
    variance = jnp.mean(cen * cen, axis=-1, keepdims=True)
    y_ref[...] = cen * lax.rsqrt(variance + LN_EPS) * lng_ref[...] + lnb_ref[...]


def _final(x, o, bonus, h16, o_attn, weights, tm):
    m = x.shape[0]
    row = lambda w, j=0: pl.BlockSpec((tm, w), lambda i: (i, j))
    const = lambda a: pl.BlockSpec(a.shape, lambda i: (0, 0))
    in_specs = ([row(D_MODEL), row(R_WIDTH), row(R_WIDTH), row(R_WIDTH, COL_ZR // R_WIDTH), row(A_WIDTH),
                 row(ZGG_W, COL_ZGG // ZGG_W)] + [const(a) for a in weights])
    return pl.pallas_call(
        _final_kernel,
        grid=(m // tm,),
        in_specs=in_specs,
        out_specs=row(D_MODEL),
        out_shape=jax.ShapeDtypeStruct((m, D_MODEL), F32),
        compiler_params=_params("parallel"),
        name="merge_out_ln",
    )(x, o, bonus, h16, o_attn, h16, *weights)


def _layer(x2, bsz, t, shift_prev, wkv0, caches, wts):
    m = x2.shape[0]
    h, h16 = _in_proj(x2, wts["w_proj"], 2048 if m % 2048 == 0 else m)

    tm = 1024 if m % 1024 == 0 and t % 1024 == 0 else (512 if m % 512 == 0 else m)
    if caches is None:
        first = None
    else:
        first = jnp.zeros((bsz, t, SHIFT_COLS), F32).at[:, 0].set(shift_prev).reshape(m, SHIFT_COLS)
    pack, lw, bonus = _rwkv_prep(h, first, t, wts["prep"], tm)

    if t % CHUNK == 0 and wkv0 is None:
        s0 = jnp.zeros((bsz, N_PAIRS, LANES, LANES), F32)
        o_seq, st = _rwkv_recurrence(pack.reshape(bsz, t, PACK_W), lw.reshape(bsz, t, R_WIDTH), s0, 2)
        o_rwkv = o_seq.reshape(m, R_WIDTH)
        wkv_new = _pairs_to_state(st)
    else:
        o_rwkv, wkv_new = _rwkv_short(pack, lw, wkv0, t)

    def rows_cols(r0, c0, width):
        if t % 8 == 0:
            return h.reshape(bsz, t, N_PROJ32)[:, r0:, c0:c0 + width]
        return h[:, c0:c0 + width].reshape(bsz, t, width)[:, r0:]

    if caches is None:
        o_attn = _attn_prompt(h, bsz, t)
    else:
        o_attn = _attn_sample(h, caches, bsz, t)

    tmf = 512 if m % 512 == 0 else m
    y = _final(x2, o_rwkv, bonus, h16, o_attn, wts["final"], tmf)

    new_kv = []
    for g, (win, _) in enumerate(A_GROUPS):
        keep = min(win, t) if caches is None else t
        new_kv.append(rows_cols(t - keep, _col_k(g), 2 * A_WIDTH).reshape(bsz, keep, 2, A_HEADS, A_HD))
    shift_new = rows_cols(t - 1, COL_ZS, SHIFT_COLS)[:, 0]
    return y.reshape(bsz, t, D_MODEL), new_kv, wkv_new, shift_new


def kernel(x_prompt, x_sample, cache_kv_g1, cache_kv_g2, cache_kv_g3, state_rwkv_wkv, state_rwkv_shift,
           w_in, b_gate, mu_shift, w0, w_w2, a0, w_a2, k_k, k_a, r_k, lnx_g, lnx_b,
           w_oa, w_ob, w_out, ln_g, ln_b):
    assert w_in.shape[0] == DEPTH
    bp, tp, _ = x_prompt.shape
    bs, ts, _ = x_sample.shape
    xp = x_prompt.reshape(bp * tp, D_MODEL)
    xs = x_sample.reshape(bs * ts, D_MODEL)
    head = lax.broadcasted_iota(jnp.int32, (LANES, LANES), 0) // R_HD
    seg = (head == head.T).astype(BF16)
    seg2 = jnp.concatenate([seg, seg], axis=0)
    row = lambda a: a.reshape(1, -1)
    acc = [[] for _ in range(12)]
    for l in range(DEPTH):
        wf = w_in[l]
        c0 = SHIFT_COLS
        c1 = c0 + R_WIDTH
        c2 = c1 + 3 * A_QKV_W
        qkv_col = lambda which, g: wf[:, c1 + (which * N_GROUPS + g) * A_WIDTH:
                                      c1 + (which * N_GROUPS + g + 1) * A_WIDTH]
        w_proj = jnp.concatenate(
            [wf[:, :c0], jnp.zeros((D_MODEL, COL_Q - c0), F32)] + [qkv_col(0, g) for g in range(N_GROUPS)]
            + [qkv_col(which, g) for g in range(N_GROUPS) for which in (1, 2)]
            + [jnp.zeros((D_MODEL, N_PROJ32 - COL_KV - 2 * A_QKV_W), F32), wf[:, c2:], wf[:, c0:c1]],
            axis=1).astype(BF16)
        zero = jnp.zeros((LORA, R_WIDTH), F32)
        w_lora = jnp.concatenate([jnp.concatenate([w_w2[l], zero], axis=1),
                                  jnp.concatenate([zero, w_a2[l]], axis=1)], axis=0)
        wl_hi = w_lora.astype(BF16)
        wl_lo = (w_lora - wl_hi.astype(F32)).astype(BF16)
        wts = {
            "w_proj": w_proj,
            "prep": (row(mu_shift[l]), row(w0[l]), row(a0[l]), jnp.concatenate([wl_hi, wl_hi], axis=0),
                     wl_lo, row(k_k[l]), row(k_a[l]), row(r_k[l]), seg2),
            "final": (seg2, row(lnx_g[l]), row(lnx_b[l]), row(b_gate[l]), w_oa[l].astype(BF16),
                      w_ob[l].astype(BF16), w_out[l].astype(BF16), row(ln_g[l]), row(ln_b[l])),
        }
        yp, kv_p, s_p, sh_p = _layer(xp, bp, tp, None, None, None, wts)
        ys, kv_s, s_s, sh_s = _layer(xs, bs, ts, state_rwkv_shift[l], state_rwkv_wkv[l],
                                     (cache_kv_g1[l], cache_kv_g2[l], cache_kv_g3[l]), wts)
        xp = yp.reshape(bp * tp, D_MODEL)
        xs = ys.reshape(bs * ts, D_MODEL)
        for idx, val in zip(range(2, 12), (kv_p[0], kv_s[0], kv_p[1], kv_s[1], kv_p[2], kv_s[2],
                                           s_p, s_s, sh_p, sh_s)):
            acc[idx].append(val)
    outs = [xp.reshape(bp, tp, D_MODEL), xs.reshape(bs, ts, D_MODEL)]
    outs += [jnp.stack(a) for a in acc[2:]]
    return tuple(outs)
```
